```python
import math
import jax
import jax.numpy as jnp
from jax import lax
import numpy as np

D_MODEL = 1024
BATCH = 8
SEQ = 2048
DEPTH = 4

GRID_W = 64
CTX_LEN = 256
N_MIXERS = 3
N_LAYERS_A = (DEPTH + 2) // 3
N_LAYERS_B = (DEPTH + 1) // 3
N_LAYERS_C = DEPTH // 3

ALPHA = (2.0 * DEPTH) ** 0.25
BETA = (8.0 * DEPTH) ** -0.25
LN_EPS = 1e-6

D_RNN = 1408
RG_BLOCKS = 16
RG_BS = D_RNN // RG_BLOCKS
RG_CONV = 4
RG_C = 8.0

D_M = 2 * D_MODEL
M_HEADS = 8
M_DK = 128
M_DV = D_M // M_HEADS
M_CONV = 4
CHUNK = 128

H_CONV = 3
H_EMB = 33
H_BANDS = (H_EMB - 1) // 2
H_FW = 64
H_DECAY_TARGET = 1e-2
H_FAST = 0.3
H_SLOW = 1.5

N_EXPERTS = 16
N_GROUPS = 4
EXP_PER_GROUP = N_EXPERTS // N_GROUPS
TOP_K = 2
D_EXPERT = 512

kernel_name = "hybrid_rglru_mlstm_hyena_moe_diffusion"


def layer_norm(x, g, b):
    xf = x.astype(jnp.float32)
    mu = jnp.mean(xf, -1, keepdims=True)
    var = jnp.mean(jnp.square(xf - mu), -1, keepdims=True)
    return ((xf - mu) * lax.rsqrt(var + LN_EPS) * g + b).astype(x.dtype)


def dwconv(x, w, b):
    k = w.shape[0]
    left = k // 2
    y = lax.conv_general_dilated(x, w[:, None, :], window_strides=(1,), padding=[(left, k - 1 - left)],
                                 dimension_numbers=("NWC", "WIO", "NWC"), feature_group_count=x.shape[-1])
    return y + b


def sincos_2d(n_tok, dtype):
    rows = n_tok // GRID_W
    quarter = D_MODEL // 4
    omega = 1.0 / (10000.0 ** (jnp.arange(quarter, dtype=jnp.float32) / quarter))
    ar = jnp.arange(rows, dtype=jnp.float32)[:, None] * omega
    ac = jnp.arange(GRID_W, dtype=jnp.float32)[:, None] * omega
    er = jnp.concatenate([jnp.sin(ar), jnp.cos(ar)], -1)
    ec = jnp.concatenate([jnp.sin(ac), jnp.cos(ac)], -1)
    half = D_MODEL // 2
    pos = jnp.concatenate([jnp.broadcast_to(er[:, None], (rows, GRID_W, half)),
                           jnp.broadcast_to(ec[None], (rows, GRID_W, half))], -1)
    return pos.reshape(rows * GRID_W, D_MODEL).astype(dtype)


def linear_scan(a, u, h0):
    def comb(lhs, rhs):
        a1, b1 = lhs
        a2, b2 = rhs
        return a1 * a2, a2 * b1 + b2
    a_cum, u_cum = lax.associative_scan(comb, (a, u), axis=1)
    return a_cum * h0[:, None, :] + u_cum


def rglru_scan(xc, w_a, b_a, w_x, b_x, lam, h0):
    bsz, L, _ = xc.shape
    xf = xc.astype(jnp.float32)
    xb = xf.reshape(bsz, L, RG_BLOCKS, RG_BS)
    r = jax.nn.sigmoid(jnp.einsum("blnk,nkj->blnj", xb, w_a.astype(jnp.float32)).reshape(bsz, L, D_RNN) + b_a)
    i = jax.nn.sigmoid(jnp.einsum("blnk,nkj->blnj", xb, w_x.astype(jnp.float32)).reshape(bsz, L, D_RNN) + b_x)
    log_a = -RG_C * r * jax.nn.softplus(-lam.astype(jnp.float32))
    u = jnp.sqrt(-jnp.expm1(2.0 * log_a)) * (i * xf)
    h = linear_scan(jnp.exp(log_a), u, h0)
    return h, h[:, -1]


def rglru_mixer(hc, hx, w_in, conv_w, conv_b, ga_w, ga_b, gx_w, gx_b, lam, w_out, need_ctx):
    def branches(h):
        gate, rec = jnp.split(h @ w_in, 2, axis=-1)
        return jax.nn.gelu(gate), dwconv(rec, conv_w, conv_b)

    gate_x, rx = branches(hx)
    if need_ctx:
        gate_c, rc = branches(hc)
    else:
        rc = dwconv(hc @ w_in[:, D_RNN:], conv_w, conv_b)
    z0 = jnp.zeros((hx.shape[0], D_RNN), jnp.float32)

    def direction(z, seq, h0, reverse):
        s = jnp.flip(seq, 1) if reverse else seq
        h, last = rglru_scan(s, ga_w[z], ga_b[z], gx_w[z], gx_b[z], lam[z], h0)
        return (jnp.flip(h, 1) if reverse else h), last

    hc_f, sc_f = direction(0, rc, z0, False)
    hx_f, _ = direction(0, rx, sc_f, False)
    hc_b, sc_b = direction(1, rc, z0, True)
    hx_b, _ = direction(1, rx, sc_b, True)
    yx = (gate_x * (hx_f + hx_b).astype(hx.dtype)) @ w_out
    yc = (gate_c * (hc_f + hc_b).astype(hc.dtype)) @ w_out if need_ctx else None
    return yc, yx


def mlstm_chunkwise(q, k, v, ig, lf, state0, return_h):
    bsz, nh, L, dk = q.shape
    dv = v.shape[-1]
    nc = L // CHUNK
    f32 = jnp.float32
    qc = q.astype(f32).reshape(bsz, nh, nc, CHUNK, dk)
    kc = k.astype(f32).reshape(bsz, nh, nc, CHUNK, dk)
    vc = v.astype(f32).reshape(bsz, nh, nc, CHUNK, dv)
    igc = ig.reshape(bsz, nh, nc, CHUNK)
    bcum = jnp.cumsum(lf.reshape(bsz, nh, nc, CHUNK), axis=-1)
    btot = bcum[..., -1]
    wlog = btot[..., None] - bcum + igc
    mloc = jnp.max(wlog, -1)
    wgt = jnp.exp(wlog - mloc[..., None])
    c_loc = jnp.einsum("bhntv,bhntk->bhnvk", vc * wgt[..., None], kc)
    n_loc = jnp.einsum("bhnt,bhntk->bhnk", wgt, kc)

    def step(carry, inp):
        C, n, m = carry
        cl, nl, ml, bt = inp
        m_new = jnp.maximum(bt + m, ml)
        sp = jnp.exp(bt + m - m_new)
        sl = jnp.exp(ml - m_new)
        new = (sp[..., None, None] * C + sl[..., None, None] * cl, sp[..., None] * n + sl[..., None] * nl, m_new)
        return new, (C, n, m)

    to_t = lambda a: jnp.moveaxis(a, 2, 0)
    final, prev = lax.scan(step, state0, (to_t(c_loc), to_t(n_loc), to_t(mloc), to_t(btot)))
    if not return_h:
        return None, final
    c_prev, n_prev, m_prev = [jnp.moveaxis(a, 0, 2) for a in prev]
    causal = jnp.tril(jnp.ones((CHUNK, CHUNK), dtype=bool))
    dlog = jnp.where(causal, bcum[..., :, None] - bcum[..., None, :] + igc[..., None, :], -jnp.inf)
    m_inter = bcum + m_prev[..., None]
    m_comb = jnp.maximum(m_inter, jnp.max(dlog, -1))
    s = jnp.einsum("bhntk,bhnsk->bhnts", qc, kc) * jnp.exp(dlog - m_comb[..., None])
    inter = jnp.exp(m_inter - m_comb)
    num = jnp.einsum("bhnts,bhnsv->bhntv", s, vc) + inter[..., None] * jnp.einsum("bhntk,bhnvk->bhntv", qc, c_prev)
    den = jnp.sum(s, -1) + inter * jnp.einsum("bhntk,bhnk->bhnt", qc, n_prev)
    h = num / jnp.maximum(jnp.abs(den), jnp.exp(-m_comb))[..., None]
    return h.reshape(bsz, nh, L, dv), final


def head_norm(h, g):
    mu = jnp.mean(h, -1, keepdims=True)
    var = jnp.mean(jnp.square(h - mu), -1, keepdims=True)
    hn = (h - mu) * lax.rsqrt(var + LN_EPS)
    bsz, nh, L, dv = h.shape
    return hn.transpose(0, 2, 1, 3).reshape(bsz, L, nh * dv) * g


def mlstm_mixer(hc, hx, w_up, conv_w, conv_b, w_q, w_k, w_v, w_o, w_if, b_if, norm_g, skip, w_down, need_ctx):
    def project(h):
        bsz, L, _ = h.shape
        xm = h @ w_up
        xc = jax.nn.silu(dwconv(xm, conv_w, conv_b))
        heads = lambda t, d: t.reshape(bsz, L, M_HEADS, d).transpose(0, 2, 1, 3)
        q = heads(xc @ w_q, M_DK)
        k = heads(xc @ w_k, M_DK) * (M_DK ** -0.5)
        v = heads(xm @ w_v, M_DV)
        g = (jnp.einsum("bld,zdg->zblg", xm, w_if) + b_if[:, None, None, :]).astype(jnp.float32)
        g = g.transpose(0, 1, 3, 2)
        ig = g[:, :, :M_HEADS]
        lf = jax.nn.log_sigmoid(g[:, :, M_HEADS:])
        return xm, xc, q, k, v, ig, lf

    xm_c, xc_c, qc, kc, vc, igc, lfc = project(hc)
    xm_x, xc_x, qx, kx, vx, igx, lfx = project(hx)
    bsz = hx.shape[0]
    st0 = (jnp.zeros((bsz, M_HEADS, M_DV, M_DK), jnp.float32), jnp.zeros((bsz, M_HEADS, M_DK), jnp.float32),
           jnp.zeros((bsz, M_HEADS), jnp.float32))

    def run(z, q, k, v, ig, lf, st, reverse, return_h):
        ig, lf = ig[z], lf[z]
        if reverse:
            q, k, v, ig, lf = [jnp.flip(a, 2) for a in (q, k, v, ig, lf)]
        h, last = mlstm_chunkwise(q, k, v, ig, lf, st, return_h)
        if reverse and return_h:
            h = jnp.flip(h, 2)
        return h, last

    hc_f, st_f = run(0, qc, kc, vc, igc, lfc, st0, False, need_ctx)
    hx_f, _ = run(0, qx, kx, vx, igx, lfx, st_f, False, True)
    hc_b, st_b = run(1, qc, kc, vc, igc, lfc, st0, True, need_ctx)
    hx_b, _ = run(1, qx, kx, vx, igx, lfx, st_b, True, True)

    def output(h, xm, xc):
        o = jax.nn.sigmoid(xm @ w_o)
        return (o * head_norm(h, norm_g).astype(xm.dtype) + skip * xc) @ w_down

    yx = output(hx_f + hx_b, xm_x, xc_x)
    yc = output(hc_f + hc_b, xm_c, xc_c) if need_ctx else None
    return yc, yx


def hyena_filters(L, w1, b1, fq1, w2, b2, fq2, w3):
    f32 = jnp.float32
    t01 = jnp.linspace(0.0, 1.0, L, dtype=f32)
    bands = jnp.linspace(1e-4, H_BANDS - 1, H_BANDS, dtype=f32)
    ang = (2.0 * math.pi / L) * jnp.arange(L, dtype=f32)[:, None] * bands[None, :]
    z = jnp.concatenate([t01[:, None], jnp.cos(ang), -jnp.sin(ang)], -1).astype(w1.dtype)
    hdn = jnp.sin(fq1 * (z @ w1 + b1))
    hdn = jnp.sin(fq2 * (hdn @ w2 + b2))
    filt = (hdn @ w3).reshape(L, 2, D_MODEL)
    dist = jnp.abs(jnp.arange(L) - L // 2).astype(f32) * (2.0 / L)
    d_max = math.log(H_DECAY_TARGET) / H_FAST
    d_min = math.log(H_DECAY_TARGET) / H_SLOW
    deltas = jnp.abs(jnp.linspace(d_min, d_max, D_MODEL, dtype=f32))
    window = jnp.exp(-dist[:, None] * deltas[None, :])
    return filt * window[:, None, :].astype(filt.dtype)


def fft_conv_centred(u, h, skip):
    L = u.shape[1]
    n = 2 * L
    uf = jnp.fft.rfft(u.astype(jnp.float32), n=n, axis=1)
    hf = jnp.fft.rfft(h.astype(jnp.float32), n=n, axis=0)
    y = jnp.fft.irfft(uf * hf[None], n=n, axis=1)[:, L // 2: L // 2 + L]
    return y.astype(u.dtype) + u * skip


def hyena_seq(h, w_in, b_in, conv_w, conv_b, f_w1, f_b1, f_fq1, f_w2, f_b2, f_fq2, f_w3, skip, w_out):
    L = h.shape[1]
    u = dwconv(h @ w_in + b_in, conv_w, conv_b)
    v, g1, g2 = jnp.split(u, 3, axis=-1)
    filt = hyena_filters(L, f_w1, f_b1, f_fq1, f_w2, f_b2, f_fq2, f_w3)
    z = g1 * fft_conv_centred(v, filt[:, 0], skip[0])
    z = g2 * fft_conv_centred(z, filt[:, 1], skip[1])
    return z @ w_out


def moe(h, router_w, router_b, w_gate, w_up, w_down):
    T = h.shape[0]
    scores = jax.nn.sigmoid((h @ router_w).astype(jnp.float32))
    sel = (scores + router_b).reshape(T, N_GROUPS, EXP_PER_GROUP)
    group_score = jnp.sum(lax.top_k(sel, TOP_K)[0], -1)
    g_best = jnp.argmax(group_score, -1)
    sel_in = jnp.take_along_axis(sel, g_best[:, None, None], axis=1)[:, 0]
    _, loc = lax.top_k(sel_in, TOP_K)
    idx = g_best[:, None] * EXP_PER_GROUP + loc
    w = jnp.take_along_axis(scores, idx, -1)
    w = w / jnp.sum(w, -1, keepdims=True)
    gates = jnp.sum(jax.nn.one_hot(idx, N_EXPERTS, dtype=jnp.float32) * w[..., None], axis=1).astype(h.dtype)
    out = jnp.zeros_like(h)
    for e in range(N_EXPERTS):
        ye = (jax.nn.silu(h @ w_gate[e]) * (h @ w_up[e])) @ w_down[e]
        out = out + gates[:, e:e + 1] * ye
    return out


def setup_inputs(seed: int = 0) -> dict:
    key = jax.random.key(seed)
    ks = iter(jax.random.split(key, 64))
    nrm = lambda shape, s: jax.random.normal(next(ks), shape, jnp.float32) * s
    D = D_MODEL
    NA, NB, NC = N_LAYERS_A, N_LAYERS_B, N_LAYERS_C
    inp = {}
    inp["x"] = nrm((BATCH, SEQ, D), 1.0)
    inp["c"] = nrm((BATCH, D), 1.0)
    inp["ctx"] = nrm((BATCH, CTX_LEN, D), 1.0)
    inp["c_ctx"] = nrm((D,), 1.0)
    inp["router_w"] = nrm((D, N_EXPERTS), D ** -0.5)
    inp["router_b"] = nrm((N_EXPERTS,), 0.01)
    inp["ada_w"] = nrm((DEPTH, D, 6 * D), 0.5 * D ** -0.5)
    inp["ada_b"] = nrm((DEPTH, 6 * D), 0.02)
    inp["ln_g"] = 1.0 + nrm((DEPTH, 2, D), 0.02)
    inp["ln_b"] = nrm((DEPTH, 2, D), 0.02)
    inp["moe_w_gate"] = nrm((DEPTH, N_EXPERTS, D, D_EXPERT), D ** -0.5)
    inp["moe_w_up"] = nrm((DEPTH, N_EXPERTS, D, D_EXPERT), D ** -0.5)
    inp["moe_w_down"] = nrm((DEPTH, N_EXPERTS, D_EXPERT, D), BETA * D_EXPERT ** -0.5)
    inp["rg_w_in"] = nrm((NA, D, 2 * D_RNN), D ** -0.5)
    inp["rg_conv_w"] = nrm((NA, RG_CONV, D_RNN), RG_CONV ** -0.5)
    inp["rg_conv_b"] = nrm((NA, D_RNN), 0.02)
    inp["rg_gate_a_w"] = nrm((NA, 2, RG_BLOCKS, RG_BS, RG_BS), RG_BS ** -0.5)
    inp["rg_gate_a_b"] = nrm((NA, 2, D_RNN), 0.02)
    inp["rg_gate_x_w"] = nrm((NA, 2, RG_BLOCKS, RG_BS, RG_BS), RG_BS ** -0.5)
    inp["rg_gate_x_b"] = nrm((NA, 2, D_RNN), 0.02)
    a8 = jax.random.uniform(next(ks), (NA, 2, D_RNN), jnp.float32, minval=0.9, maxval=0.999)
    a = a8 ** (1.0 / RG_C)
    inp["rg_lambda"] = jnp.log(a) - jnp.log1p(-a)
    inp["rg_w_out"] = nrm((NA, D_RNN, D), BETA * D_RNN ** -0.5)
    inp["ml_w_up"] = nrm((NB, D, D_M), D ** -0.5)
    inp["ml_conv_w"] = nrm((NB, M_CONV, D_M), M_CONV ** -0.5)
    inp["ml_conv_b"] = nrm((NB, D_M), 0.02)
    inp["ml_w_q"] = nrm((NB, D_M, M_HEADS * M_DK), D_M ** -0.5)
    inp["ml_w_k"] = nrm((NB, D_M, M_HEADS * M_DK), D_M ** -0.5)
    inp["ml_w_v"] = nrm((NB, D_M, M_HEADS * M_DV), D_M ** -0.5)
    inp["ml_w_o"] = nrm((NB, D_M, D_M), D_M ** -0.5)
    inp["ml_w_if"] = nrm((NB, 2, D_M, 2 * M_HEADS), D_M ** -0.5)
    inp["ml_b_if"] = jnp.concatenate([nrm((NB, 2, M_HEADS), 0.1),
                                      jnp.linspace(3.0, 6.0, M_HEADS, dtype=jnp.float32) + nrm((NB, 2, M_HEADS), 0.1)], -1)
    inp["ml_norm_g"] = 1.0 + nrm((NB, D_M), 0.02)
    inp["ml_skip"] = 1.0 + nrm((NB, D_M), 0.02)
    inp["ml_w_down"] = nrm((NB, D_M, D), BETA * D_M ** -0.5)
    inp["hy_w_in"] = nrm((NC, D, 3 * D), D ** -0.5)
    inp["hy_b_in"] = nrm((NC, 3 * D), 0.02)
    inp["hy_conv_w"] = nrm((NC, H_CONV, 3 * D), H_CONV ** -0.5)
    inp["hy_conv_b"] = nrm((NC, 3 * D), 0.02)
    inp["hy_f_w1"] = nrm((NC, H_EMB, H_FW), H_EMB ** -0.5)
    inp["hy_f_b1"] = nrm((NC, H_FW), 0.02)
    inp["hy_f_freq1"] = 1.0 + nrm((NC, H_FW), 0.02)
    inp["hy_f_w2"] = nrm((NC, H_FW, H_FW), H_FW ** -0.5)
    inp["hy_f_b2"] = nrm((NC, H_FW), 0.02)
    inp["hy_f_freq2"] = 1.0 + nrm((NC, H_FW), 0.02)
    inp["hy_f_w3"] = nrm((NC, H_FW, 2 * D), 0.1 * H_FW ** -0.5)
    inp["hy_skip"] = nrm((NC, 2, D), 0.1)
    inp["hy_w_out"] = nrm((NC, D, D), BETA * D ** -0.5)
    return inp


def reference(x, c, ctx, c_ctx, router_w, router_b, ada_w, ada_b, ln_g, ln_b, moe_w_gate, moe_w_up, moe_w_down,
              rg_w_in, rg_conv_w, rg_conv_b, rg_gate_a_w, rg_gate_a_b, rg_gate_x_w, rg_gate_x_b, rg_lambda, rg_w_out,
              ml_w_up, ml_conv_w, ml_conv_b, ml_w_q, ml_w_k, ml_w_v, ml_w_o, ml_w_if, ml_b_if, ml_norm_g, ml_skip,
              ml_w_down, hy_w_in, hy_b_in, hy_conv_w, hy_conv_b, hy_f_w1, hy_f_b1, hy_f_freq1, hy_f_w2, hy_f_b2,
              hy_f_freq2, hy_f_w3, hy_skip, hy_w_out):
    hx = x + sincos_2d(x.shape[1], x.dtype)[None]
    hc = ctx
    for i in range(DEPTH):
        last = i == DEPTH - 1
        need_ctx = not last
        kind, j = i % N_MIXERS, i // N_MIXERS
        mod_x = (jax.nn.silu(c) @ ada_w[i] + ada_b[i])[:, None, :]
        mod_c = jax.nn.silu(c_ctx) @ ada_w[i] + ada_b[i]
        shx, scx, gtx, shx2, scx2, gtx2 = jnp.split(mod_x, 6, axis=-1)
        shc, scc, gtc, shc2, scc2, gtc2 = jnp.split(mod_c, 6, axis=-1)
        in_x = hx * (1.0 + scx) + shx
        in_c = hc * (1.0 + scc) + shc if (need_ctx or kind != 2) else None
        if kind == 0:
            yc, yx = rglru_mixer(in_c, in_x, rg_w_in[j], rg_conv_w[j], rg_conv_b[j], rg_gate_a_w[j], rg_gate_a_b[j],
                                 rg_gate_x_w[j], rg_gate_x_b[j], rg_lambda[j], rg_w_out[j], need_ctx)
        elif kind == 1:
            yc, yx = mlstm_mixer(in_c, in_x, ml_w_up[j], ml_conv_w[j], ml_conv_b[j], ml_w_q[j], ml_w_k[j], ml_w_v[j],
                                 ml_w_o[j], ml_w_if[j], ml_b_if[j], ml_norm_g[j], ml_skip[j], ml_w_down[j], need_ctx)
        else:
            hp = (hy_w_in[j], hy_b_in[j], hy_conv_w[j], hy_conv_b[j], hy_f_w1[j], hy_f_b1[j], hy_f_freq1[j],
                  hy_f_w2[j], hy_f_b2[j], hy_f_freq2[j], hy_f_w3[j], hy_skip[j], hy_w_out[j])
            yx = hyena_seq(in_x, *hp)
            yc = hyena_seq(in_c, *hp) if need_ctx else None
        hx = layer_norm(ALPHA * hx + gtx * yx, ln_g[i, 0], ln_b[i, 0])
        fx = hx * (1.0 + scx2) + shx2
        if last:
            yx2 = moe(fx.reshape(-1, D_MODEL), router_w, router_b, moe_w_gate[i], moe_w_up[i],
                      moe_w_down[i]).reshape(fx.shape)
        else:
            hc = layer_norm(ALPHA * hc + gtc * yc, ln_g[i, 0], ln_b[i, 0])
            fc = hc * (1.0 + scc2) + shc2
            n_c = fc.shape[0] * fc.shape[1]
            y2 = moe(jnp.concatenate([fc.reshape(-1, D_MODEL), fx.reshape(-1, D_MODEL)], 0), router_w, router_b,
                     moe_w_gate[i], moe_w_up[i], moe_w_down[i])
            hc = layer_norm(ALPHA * hc + gtc2 * y2[:n_c].reshape(fc.shape), ln_g[i, 1], ln_b[i, 1])
            yx2 = y2[n_c:].reshape(fx.shape)
        hx = layer_norm(ALPHA * hx + gtx2 * yx2, ln_g[i, 1], ln_b[i, 1])
    return hx
```

```python
import functools
import math

import numpy as np
import jax
import jax.numpy as jnp
from jax import lax
from jax.experimental import pallas as pl
from jax.experimental.pallas import tpu as pltpu

F32 = jnp.float32
BF16 = jnp.bfloat16

D_MODEL = 1024
BATCH = 8
SEQ = 2048
DEPTH = 4
GRID_W = 64
CTX_LEN = 256
L_JOINT = CTX_LEN + SEQ
N_MIXERS = 3
ALPHA = (2.0 * DEPTH) ** 0.25
LN_EPS = 1e-6

D_RNN = 1408
RG_BLOCKS = 16
RG_BS = D_RNN // RG_BLOCKS
RG_C = 8.0

D_M = 2 * D_MODEL
M_HEADS = 8
M_DK = 128
M_DV = D_M // M_HEADS
CHUNK = 128

H_EMB = 33
H_BANDS = (H_EMB - 1) // 2
H_DECAY_TARGET = 1e-2
H_FAST = 0.3
H_SLOW = 1.5

N_EXPERTS = 16
N_GROUPS = 4
EXP_PER_GROUP = N_EXPERTS // N_GROUPS
TOP_K = 2
D_EXPERT = 512

VMEM_LIMIT_BYTES = 56 * 1024 * 1024
MOE_TILE = 256
SCAN_TBLK = 64


def _params(*sem):
    return pltpu.CompilerParams(dimension_semantics=sem, vmem_limit_bytes=VMEM_LIMIT_BYTES)


def _mm_body(x_ref, w_ref, o_ref):
    o_ref[...] = jnp.dot(x_ref[...].astype(BF16), w_ref[...], preferred_element_type=F32).astype(o_ref.dtype)


def mm(x, w, *, tm=512, tn=None, out_dtype=F32):
    m, k = x.shape
    n = w.shape[1]
    tn = n if tn is None else tn
    assert m % tm == 0 and n % tn == 0
    return pl.pallas_call(
        _mm_body,
        out_shape=jax.ShapeDtypeStruct((m, n), out_dtype),
        grid=(n // tn, m // tm),
        in_specs=[pl.BlockSpec((tm, k), lambda j, i: (i, 0)),
                  pl.BlockSpec((k, tn), lambda j, i: (0, j))],
        out_specs=pl.BlockSpec((tm, tn), lambda j, i: (i, j)),
        compiler_params=_params("parallel", "parallel"),
        name="mm",
    )(x, w)


def _bmm_body(a_ref, x_ref, o_ref):
    o_ref[...] = jnp.dot(a_ref[...], x_ref[...].astype(BF16), preferred_element_type=F32).astype(o_ref.dtype)


def bmm_left(a, x, *, tn=256, out_dtype=F32):
    mo, k = a.shape
    b, _, d = x.shape
    assert d % tn == 0
    return pl.pallas_call(
        _bmm_body,
        out_shape=jax.ShapeDtypeStruct((b, mo, d), out_dtype),
        grid=(b, d // tn),
        in_specs=[pl.BlockSpec((mo, k), lambda i, j: (0, 0)),
                  pl.BlockSpec((None, k, tn), lambda i, j: (i, 0, j))],
        out_specs=pl.BlockSpec((None, mo, tn), lambda i, j: (i, 0, j)),
        compiler_params=_params("parallel", "parallel"),
        name="dft_mm",
    )(a, x)


def _scan_body(af_ref, uf_ref, ab_ref, ub_ref, of_ref, ob_ref, st_ref):
    @pl.when(pl.program_id(0) == 0)
    def _():
        st_ref[...] = jnp.zeros_like(st_ref)

    tblk = af_ref.shape[0]

    def body(t, carry):
        hf, hb = carry
        hf = af_ref[t] * hf + uf_ref[t]
        of_ref[t] = hf
        tb = tblk - 1 - t
        hb = ab_ref[tb] * hb + ub_ref[tb]
        ob_ref[tb] = hb
        return hf, hb

    hf, hb = lax.fori_loop(0, tblk, body, (st_ref[0], st_ref[1]), unroll=8)
    st_ref[0] = hf
    st_ref[1] = hb


def rglru_scan(a_f, u_f, a_b, u_b):
    lj, b, c = a_f.shape
    nblk = lj // SCAN_TBLK
    nctx = CTX_LEN // SCAN_TBLK

    def bwd_map(i):
        return (jnp.where(i < nctx, nctx - 1 - i, nblk - 1 + nctx - i), 0, 0)

    fwd = pl.BlockSpec((SCAN_TBLK, b, c), lambda i: (i, 0, 0))
    bwd = pl.BlockSpec((SCAN_TBLK, b, c), bwd_map)
    return pl.pallas_call(
        _scan_body,
        out_shape=(jax.ShapeDtypeStruct((lj, b, c), F32), jax.ShapeDtypeStruct((lj, b, c), F32)),
        grid=(nblk,),
        in_specs=[fwd, fwd, bwd, bwd],
        out_specs=(fwd, bwd),
        scratch_shapes=[pltpu.VMEM((2, b, c), F32)],
        compiler_params=_params("arbitrary"),
        name="rglru_scan",
    )(a_f, u_f, a_b, u_b)


def _split_dot(tri, x, lhs_tri):
    hi = x.astype(BF16)
    lo = (x - hi.astype(F32)).astype(BF16)
    if lhs_tri:
        return jnp.dot(tri, hi, preferred_element_type=F32) + jnp.dot(tri, lo, preferred_element_type=F32)
    return jnp.dot(hi, tri, preferred_element_type=F32) + jnp.dot(lo, tri, preferred_element_type=F32)


def _mlstm_body(q_ref, k_ref, v_ref, gc_ref, gr_ref, ng_ref, o_ref, ct_ref, n_ref, m_ref):
    row = lax.broadcasted_iota(jnp.int32, (CHUNK, CHUNK), 0)
    col = lax.broadcasted_iota(jnp.int32, (CHUNK, CHUNK), 1)
    lower = col <= row
    upper = col >= row
    tri_lower = jnp.where(lower, 1.0, 0.0).astype(BF16)
    tri_upper = jnp.where(upper, 1.0, 0.0).astype(BF16)
    ng = ng_ref[...]

    def chunk(c, z, reverse):
        c0 = pl.multiple_of(c * CHUNK, CHUNK)
        q = q_ref[pl.ds(c0, CHUNK), :]
        k = k_ref[pl.ds(c0, CHUNK), :]
        v = v_ref[pl.ds(c0, CHUNK), :]
        gc = gc_ref[pl.ds(c0, CHUNK), :]
        gr = gr_ref[:, pl.ds(c0, CHUNK)]
        ig_c = gc[:, 2 * z:2 * z + 1]
        lf_c = gc[:, 2 * z + 1:2 * z + 2]
        ig_r = gr[2 * z:2 * z + 1, :]
        if reverse:
            bc_c = _split_dot(tri_upper, gc, True)[:, 2 * z + 1:2 * z + 2]
            bc_r = _split_dot(tri_lower, gr, False)[2 * z + 1:2 * z + 2, :]
            mask = upper
        else:
            bc_c = _split_dot(tri_lower, gc, True)[:, 2 * z + 1:2 * z + 2]
            bc_r = _split_dot(tri_upper, gr, False)[2 * z + 1:2 * z + 2, :]
            mask = lower
        btot = jnp.sum(lf_c, axis=0, keepdims=True)
        m_prev = m_ref[...]
        n_prev = n_ref[...]
        ct_prev = ct_ref[...]
        dlog = jnp.where(mask, bc_c - bc_r + ig_r, -jnp.inf)
        m_inter = bc_c + m_prev
        m_comb = jnp.maximum(m_inter, jnp.max(dlog, axis=1, keepdims=True))
        qk = lax.dot_general(q, k, (((1,), (1,)), ((), ())), preferred_element_type=F32)
        s = qk * jnp.exp(dlog - m_comb)
        inter = jnp.exp(m_inter - m_comb)
        num = (jnp.dot(s.astype(BF16), v, preferred_element_type=F32)
               + inter * jnp.dot(q, ct_prev.astype(BF16), preferred_element_type=F32))
        den = (jnp.sum(s, axis=1, keepdims=True)
               + inter * jnp.sum(q.astype(F32) * n_prev, axis=1, keepdims=True))
        h = num / jnp.maximum(jnp.abs(den), jnp.exp(-m_comb))
        wlog = btot - bc_c + ig_c
        mloc = jnp.max(wlog, axis=0, keepdims=True)
        wgt = jnp.exp(wlog - mloc)
        m_new = jnp.maximum(btot + m_prev, mloc)
        sp = jnp.exp(btot + m_prev - m_new)
        sl = jnp.exp(mloc - m_new)
        kf = k.astype(F32)
        vw = (v.astype(F32) * wgt).astype(BF16)
        ct_loc = jnp.dot(kf.T.astype(BF16), vw, preferred_element_type=F32)
        ct_ref[...] = sp * ct_prev + sl * ct_loc
        n_ref[...] = sp * n_prev + sl * jnp.sum(kf * wgt, axis=0, keepdims=True)
        m_ref[...] = m_new
        return c0, h

    def reset():
        ct_ref[...] = jnp.zeros_like(ct_ref)
        n_ref[...] = jnp.zeros_like(n_ref)
        m_ref[...] = jnp.zeros_like(m_ref)

    n_chunks = o_ref.shape[0] // CHUNK
    n_ctx = CTX_LEN // CHUNK

    reset()

    def fwd(c, carry):
        c0, h = chunk(c, 0, False)
        o_ref[pl.ds(c0, CHUNK), :] = h
        return carry

    lax.fori_loop(0, n_chunks, fwd, 0)

    reset()

    def bwd(c):
        c0, h = chunk(c, 1, True)
        tot = o_ref[pl.ds(c0, CHUNK), :] + h
        mu = jnp.mean(tot, axis=1, keepdims=True)
        var = jnp.mean(jnp.square(tot - mu), axis=1, keepdims=True)
        o_ref[pl.ds(c0, CHUNK), :] = (tot - mu) * lax.rsqrt(var + LN_EPS) * ng

    def bwd_ctx(i, carry):
        bwd(n_ctx - 1 - i)
        return carry

    def bwd_lat(i, carry):
        bwd(n_chunks - 1 - i)
        return carry

    lax.fori_loop(0, n_ctx, bwd_ctx, 0)
    lax.fori_loop(0, n_chunks - n_ctx, bwd_lat, 0)


def mlstm_cell(q, k, v, g_col, g_row, norm_g):
    b, lj, _ = q.shape
    return pl.pallas_call(
        _mlstm_body,
        out_shape=jax.ShapeDtypeStruct((b, lj, M_HEADS * M_DV), F32),
        grid=(b, M_HEADS),
        in_specs=[pl.BlockSpec((None, lj, M_DK), lambda i, h: (i, 0, h)),
                  pl.BlockSpec((None, lj, M_DK), lambda i, h: (i, 0, h)),
                  pl.BlockSpec((None, lj, M_DV), lambda i, h: (i, 0, h)),
                  pl.BlockSpec((None, None, lj, 4), lambda i, h: (i, h, 0, 0)),
                  pl.BlockSpec((None, None, 4, lj), lambda i, h: (i, h, 0, 0)),
                  pl.BlockSpec((1, M_DV), lambda i, h: (0, h))],
        out_specs=pl.BlockSpec((None, lj, M_DV), lambda i, h: (i, 0, h)),
        scratch_shapes=[pltpu.VMEM((M_DK, M_DV), F32), pltpu.VMEM((1, M_DK), F32), pltpu.VMEM((1, 1), F32)],
        compiler_params=_params("parallel", "parallel"),
        name="mlstm_cell",
    )(q, k, v, g_col, g_row, norm_g)


def _moe_body(te_ref, tf_ref, nu_ref, x_ref, g_ref, wg_ref, wu_ref, wd_ref, o_ref, wg_s, wu_s, wd_s):
    i = pl.program_id(0)

    @pl.when(i < nu_ref[0])
    def _():
        @pl.when(tf_ref[i] == 1)
        def _():
            wg_s[...] = wg_ref[...].astype(BF16)
            wu_s[...] = wu_ref[...].astype(BF16)
            wd_s[...] = wd_ref[...].astype(BF16)

        x = x_ref[...]
        a = jnp.dot(x, wg_s[...], preferred_element_type=F32)
        u = jnp.dot(x, wu_s[...], preferred_element_type=F32)
        hid = (a * jax.nn.sigmoid(a)) * u
        y = jnp.dot(hid.astype(BF16), wd_s[...], preferred_element_type=F32)
        o_ref[...] = y * g_ref[...]

    @pl.when(i >= nu_ref[0])
    def _():
        o_ref[...] = jnp.zeros_like(o_ref)


def moe_experts(layer, tile_expert, tile_first, n_used, xs, row_gate, w_gate, w_up, w_down):
    n_rows, d = xs.shape
    n_tiles = n_rows // MOE_TILE
    wmap_in = lambda i, te, tf, nu: (layer, te[i], 0, 0)
    grid_spec = pltpu.PrefetchScalarGridSpec(
        num_scalar_prefetch=3,
        grid=(n_tiles,),
        in_specs=[pl.BlockSpec((MOE_TILE, d), lambda i, te, tf, nu: (i, 0)),
                  pl.BlockSpec((MOE_TILE, 1), lambda i, te, tf, nu: (i, 0)),
                  pl.BlockSpec((None, None, d, D_EXPERT), wmap_in),
                  pl.BlockSpec((None, None, d, D_EXPERT), wmap_in),
                  pl.BlockSpec((None, None, D_EXPERT, d), wmap_in)],
        out_specs=pl.BlockSpec((MOE_TILE, d), lambda i, te, tf, nu: (i, 0)),
        scratch_shapes=[pltpu.VMEM((d, D_EXPERT), BF16), pltpu.VMEM((d, D_EXPERT), BF16),
                        pltpu.VMEM((D_EXPERT, d), BF16)],
    )
    return pl.pallas_call(
        _moe_body,
        out_shape=jax.ShapeDtypeStruct((n_rows, d), F32),
        grid_spec=grid_spec,
        compiler_params=_params("arbitrary"),
        name="moe_experts",
    )(tile_expert, tile_first, n_used, xs, row_gate, w_gate, w_up, w_down)


def moe(layer, h, router_w, router_b, w_gate, w_up, w_down):
    t = h.shape[0]
    scores = jax.nn.sigmoid(jnp.dot(h, router_w, precision=lax.Precision.HIGHEST))
    sel = (scores + router_b).reshape(t, N_GROUPS, EXP_PER_GROUP)
    group_score = jnp.sum(lax.top_k(sel, TOP_K)[0], -1)
    g_best = jnp.argmax(group_score, -1)
    sel_in = jnp.take_along_axis(sel, g_best[:, None, None], axis=1)[:, 0]
    _, loc = lax.top_k(sel_in, TOP_K)
    idx = g_best[:, None] * EXP_PER_GROUP + loc
    wts = jnp.take_along_axis(scores, idx, -1)
    wts = wts / jnp.sum(wts, -1, keepdims=True)

    n_pick = t * TOP_K
    n_tiles = n_pick // MOE_TILE + N_EXPERTS
    n_rows = n_tiles * MOE_TILE
    flat_e = idx.reshape(-1).astype(jnp.int32)
    onehot = (flat_e[:, None] == jnp.arange(N_EXPERTS, dtype=jnp.int32)[None, :]).astype(jnp.int32)
    csum = jnp.cumsum(onehot, axis=0)
    counts = csum[-1]
    rank = jnp.sum(csum * onehot, axis=1) - 1
    tiles_per = (counts + MOE_TILE - 1) // MOE_TILE
    tile_end = jnp.cumsum(tiles_per)
    tile_start = tile_end - tiles_per
    dest = tile_start[flat_e] * MOE_TILE + rank
    flat_t = jnp.arange(n_pick, dtype=jnp.int32) // TOP_K
    row_token = jnp.zeros((n_rows,), jnp.int32).at[dest].set(flat_t)
    row_gate = jnp.zeros((n_rows,), F32).at[dest].set(wts.reshape(-1))
    n_used = tile_end[-1]
    tile_ids = jnp.arange(n_tiles, dtype=jnp.int32)
    tile_expert = jnp.searchsorted(tile_end, jnp.minimum(tile_ids, n_used - 1), side="right").astype(jnp.int32)
    tile_expert = jnp.minimum(tile_expert, N_EXPERTS - 1)
    tile_first = ((tile_ids == tile_start[tile_expert]) & (tile_ids < n_used)).astype(jnp.int32)

    xs = jnp.take(h.astype(BF16), row_token, axis=0)
    ys = moe_experts(layer, tile_expert, tile_first, n_used.reshape(1).astype(jnp.int32), xs,
                     row_gate[:, None], w_gate, w_up, w_down)
    pos = dest.reshape(t, TOP_K)
    return jnp.take(ys, pos[:, 0], axis=0) + jnp.take(ys, pos[:, 1], axis=0)


def layer_norm(x, g, b):
    mu = jnp.mean(x, -1, keepdims=True)
    var = jnp.mean(jnp.square(x - mu), -1, keepdims=True)
    return (x - mu) * lax.rsqrt(var + LN_EPS) * g + b


def sincos_2d(n_tok):
    rows = n_tok // GRID_W
    quarter = D_MODEL // 4
    omega = 1.0 / (10000.0 ** (jnp.arange(quarter, dtype=F32) / quarter))
    ar = jnp.arange(rows, dtype=F32)[:, None] * omega
    ac = jnp.arange(GRID_W, dtype=F32)[:, None] * omega
    er = jnp.concatenate([jnp.sin(ar), jnp.cos(ar)], -1)
    ec = jnp.concatenate([jnp.sin(ac), jnp.cos(ac)], -1)
    half = D_MODEL // 2
    pos = jnp.concatenate([jnp.broadcast_to(er[:, None], (rows, GRID_W, half)),
                           jnp.broadcast_to(ec[None], (rows, GRID_W, half))], -1)
    return pos.reshape(rows * GRID_W, D_MODEL)


def _dwconv_seg(x, w, b, axis):
    kk = w.shape[0]
    left = kk // 2
    n = x.shape[axis]
    pad = [(0, 0)] * x.ndim
    pad[axis] = (left, kk - 1 - left)
    xp = jnp.pad(x, pad)
    y = b
    for j in range(kk):
        y = y + lax.slice_in_dim(xp, j, j + n, axis=axis) * w[j]
    return y


def dwconv_joint(x, w, b, axis):
    xc = lax.slice_in_dim(x, 0, CTX_LEN, axis=axis)
    xl = lax.slice_in_dim(x, CTX_LEN, L_JOINT, axis=axis)
    return jnp.concatenate([_dwconv_seg(xc, w, b, axis), _dwconv_seg(xl, w, b, axis)], axis=axis)


def rglru_mixer(inp, w_in, conv_w, conv_b, ga_w, ga_b, gx_w, gx_b, lam, w_out):
    b = inp.shape[0]
    tm_in = jnp.transpose(inp, (1, 0, 2)).reshape(L_JOINT * b, D_MODEL)
    proj = mm(tm_in, w_in.astype(BF16), tn=D_RNN)
    gate = jax.nn.gelu(proj[:, :D_RNN])
    rec = proj[:, D_RNN:].reshape(L_JOINT, b, D_RNN)
    xc = dwconv_joint(rec, conv_w, conv_b, 0)
    xc_flat = xc.reshape(L_JOINT * b, D_RNN)

    def block_diag(w):
        eye = jnp.eye(RG_BLOCKS, dtype=w.dtype)
        return jnp.einsum("nkj,nm->nkmj", w, eye).reshape(D_RNN, D_RNN)

    w_gates = jnp.concatenate([block_diag(ga_w[0]), block_diag(gx_w[0]),
                               block_diag(ga_w[1]), block_diag(gx_w[1])], axis=1).astype(BF16)
    pre = mm(xc_flat, w_gates, tn=D_RNN).reshape(L_JOINT, b, 4, D_RNN)

    def gates(z):
        r = jax.nn.sigmoid(pre[:, :, 2 * z] + ga_b[z])
        i = jax.nn.sigmoid(pre[:, :, 2 * z + 1] + gx_b[z])
        log_a = -RG_C * r * jax.nn.softplus(-lam[z])
        u = jnp.sqrt(-jnp.expm1(2.0 * log_a)) * (i * xc)
        return jnp.exp(log_a), u

    a_f, u_f = gates(0)
    a_b, u_b = gates(1)
    h_f, h_b = rglru_scan(a_f, u_f, a_b, u_b)
    y = mm(gate * (h_f + h_b).reshape(L_JOINT * b, D_RNN), w_out.astype(BF16))
    return jnp.transpose(y.reshape(L_JOINT, b, D_MODEL), (1, 0, 2))


def mlstm_mixer(inp, w_up, conv_w, conv_b, w_q, w_k, w_v, w_o, w_if, b_if, norm_g, skip, w_down):
    b = inp.shape[0]
    rows = b * L_JOINT
    xm = mm(inp.reshape(rows, D_MODEL), w_up.astype(BF16))
    xc = jax.nn.silu(dwconv_joint(xm.reshape(b, L_JOINT, D_M), conv_w, conv_b, 1)).reshape(rows, D_M)
    qk = mm(xc, jnp.concatenate([w_q, w_k], axis=1).astype(BF16), tn=M_HEADS * M_DK)
    q = qk[:, :M_HEADS * M_DK].astype(BF16).reshape(b, L_JOINT, M_HEADS * M_DK)
    k = (qk[:, M_HEADS * M_DK:] * (M_DK ** -0.5)).astype(BF16).reshape(b, L_JOINT, M_HEADS * M_DK)
    n_gate = 4 * M_HEADS
    w_g = jnp.concatenate([w_if[0], w_if[1], jnp.zeros((D_M, 128 - n_gate), F32)], axis=1)
    vog = mm(xm, jnp.concatenate([w_v, w_o, w_g], axis=1).astype(BF16), tn=(2 * D_M + 128) // 3)
    v = vog[:, :D_M].astype(BF16).reshape(b, L_JOINT, D_M)
    o = jax.nn.sigmoid(vog[:, D_M:2 * D_M])
    g = vog[:, 2 * D_M:2 * D_M + n_gate].reshape(b, L_JOINT, 2, 2 * M_HEADS) + b_if
    ig = g[..., :M_HEADS]
    lf = jax.nn.log_sigmoid(g[..., M_HEADS:])
    g4 = jnp.stack([ig[:, :, 0], lf[:, :, 0], ig[:, :, 1], lf[:, :, 1]], axis=-1)
    g_col = jnp.transpose(g4, (0, 2, 1, 3))
    g_row = jnp.transpose(g4, (0, 2, 3, 1))
    hn = mlstm_cell(q, k, v, g_col, g_row, norm_g.reshape(1, D_M)).reshape(rows, D_M)
    y = mm(o * hn + skip * xc, w_down.astype(BF16))
    return y.reshape(b, L_JOINT, D_MODEL)


def _dft_mats(n_time):
    n = 3 * n_time // 2
    half = n // 2
    kk = np.arange(half, dtype=np.int64)[:, None]
    tt = np.arange(n_time, dtype=np.int64)[None, :]
    ang = 2.0 * np.pi * ((kk * tt) % n).astype(np.float64) / n
    top = np.cos(ang)
    bot = -np.sin(ang)
    bot[0] = np.cos(np.pi * tt[0])
    fwd = np.concatenate([top, bot], axis=0)
    mm_ = (np.arange(n_time, dtype=np.int64) + n_time // 2)[:, None]
    ang2 = 2.0 * np.pi * ((mm_ * kk.T) % n).astype(np.float64) / n
    wk = np.full((1, half), 2.0)
    wk[0, 0] = 1.0
    itop = wk * np.cos(ang2) / n
    ibot = -2.0 * np.sin(ang2) / n
    ibot[:, 0] = np.cos(np.pi * mm_[:, 0]) / n
    inv = np.concatenate([itop, ibot], axis=1)
    return jnp.asarray(fwd, dtype=BF16), jnp.asarray(inv, dtype=BF16)


def hyena_filters(n_time, w1, b1, fq1, w2, b2, fq2, w3):
    hp = lax.Precision.HIGHEST
    t01 = jnp.linspace(0.0, 1.0, n_time, dtype=F32)
    bands = jnp.linspace(1e-4, H_BANDS - 1, H_BANDS, dtype=F32)
    ang = (2.0 * math.pi / n_time) * jnp.arange(n_time, dtype=F32)[:, None] * bands[None, :]
    z = jnp.concatenate([t01[:, None], jnp.cos(ang), -jnp.sin(ang)], -1)
    hdn = jnp.sin(fq1 * (jnp.dot(z, w1, precision=hp) + b1))
    hdn = jnp.sin(fq2 * (jnp.dot(hdn, w2, precision=hp) + b2))
    filt = jnp.dot(hdn, w3, precision=hp).reshape(n_time, 2, D_MODEL)
    dist = jnp.abs(jnp.arange(n_time) - n_time // 2).astype(F32) * (2.0 / n_time)
    d_max = math.log(H_DECAY_TARGET) / H_FAST
    d_min = math.log(H_DECAY_TARGET) / H_SLOW
    deltas = jnp.abs(jnp.linspace(d_min, d_max, D_MODEL, dtype=F32))
    window = jnp.exp(-dist[:, None] * deltas[None, :])
    return filt * window[:, None, :]


def _spec_mul(x, hf):
    half = hf.shape[0] // 2
    xt, xb = x[:, :half], x[:, half:]
    ht, hb = hf[:half], hf[half:]
    yt = xt * ht - xb * hb
    yb = xt * hb + xb * ht
    first = (jnp.arange(half) == 0)[None, :, None]
    yt = jnp.where(first, xt * ht, yt)
    yb = jnp.where(first, xb * hb, yb)
    return jnp.concatenate([yt, yb], axis=1)


def hyena_seq(h, w_in_bf, b_in, conv_w, conv_b, fparams, skip, w_out_bf):
    b, n_time, _ = h.shape
    tm = 512 if (b * n_time) % 512 == 0 else 256
    u = mm(h.reshape(b * n_time, D_MODEL), w_in_bf, tm=tm, tn=D_MODEL).reshape(b, n_time, 3 * D_MODEL) + b_in
    u = _dwconv_seg(u, conv_w, conv_b, 1)
    v, g1, g2 = jnp.split(u, 3, axis=-1)
    filt = hyena_filters(n_time, *fparams)
    fwd, inv = _dft_mats(n_time)
    hf = bmm_left(fwd, filt.reshape(1, n_time, 2 * D_MODEL))[0]

    def conv(sig, j):
        spec = bmm_left(fwd, sig)
        prod = _spec_mul(spec, hf[:, j * D_MODEL:(j + 1) * D_MODEL])
        return bmm_left(inv, prod) + sig * skip[j]

    z = g1 * conv(v, 0)
    z = g2 * conv(z, 1)
    return mm(z.reshape(b * n_time, D_MODEL), w_out_bf, tm=tm).reshape(b, n_time, D_MODEL)


def kernel(x, c, ctx, c_ctx, router_w, router_b, ada_w, ada_b, ln_g, ln_b, moe_w_gate, moe_w_up, moe_w_down, rg_w_in, rg_conv_w, rg_conv_b, rg_gate_a_w, rg_gate_a_b, rg_gate_x_w, rg_gate_x_b, rg_lambda, rg_w_out, ml_w_up, ml_conv_w, ml_conv_b, ml_w_q, ml_w_k, ml_w_v, ml_w_o, ml_w_if, ml_b_if, ml_norm_g, ml_skip, ml_w_down, hy_w_in, hy_b_in, hy_conv_w, hy_conv_b, hy_f_w1, hy_f_b1, hy_f_freq1, hy_f_w2, hy_f_b2, hy_f_freq2, hy_f_w3, hy_skip, hy_w_out):
    bsz = x.shape[0]
    hx = x + sincos_2d(SEQ)[None]
    hj = jnp.concatenate([ctx, hx], axis=1)
    is_ctx = (jnp.arange(L_JOINT) < CTX_LEN)[None, :, None]

    cond = jnp.concatenate([jax.nn.silu(c), jax.nn.silu(c_ctx)[None],
                            jnp.zeros((16 - bsz - 1, D_MODEL), F32)], axis=0)

    for i in range(DEPTH):
        kind, j = i % N_MIXERS, i // N_MIXERS
        mod = mm(cond, ada_w[i].astype(BF16), tm=16, tn=D_MODEL) + ada_b[i]
        mod_x = mod[:bsz, None, :]
        mod_c = mod[bsz][None, None, :]
        sh, sc, gt, sh2, sc2, gt2 = [jnp.where(is_ctx, mc, mx) for mc, mx in
                                     zip(jnp.split(mod_c, 6, axis=-1), jnp.split(mod_x, 6, axis=-1))]
        inp = hj * (1.0 + sc) + sh
        if kind == 0:
            y = rglru_mixer(inp, rg_w_in[j], rg_conv_w[j], rg_conv_b[j], rg_gate_a_w[j], rg_gate_a_b[j],
                            rg_gate_x_w[j], rg_gate_x_b[j], rg_lambda[j], rg_w_out[j])
        elif kind == 1:
            y = mlstm_mixer(inp, ml_w_up[j], ml_conv_w[j], ml_conv_b[j], ml_w_q[j], ml_w_k[j], ml_w_v[j],
                            ml_w_o[j], ml_w_if[j], ml_b_if[j], ml_norm_g[j], ml_skip[j], ml_w_down[j])
        else:
            fparams = (hy_f_w1[j], hy_f_b1[j], hy_f_freq1[j], hy_f_w2[j], hy_f_b2[j], hy_f_freq2[j], hy_f_w3[j])
            args = (hy_w_in[j].astype(BF16), hy_b_in[j], hy_conv_w[j], hy_conv_b[j], fparams, hy_skip[j],
                    hy_w_out[j].astype(BF16))
            y = jnp.concatenate([hyena_seq(inp[:, :CTX_LEN], *args), hyena_seq(inp[:, CTX_LEN:], *args)], axis=1)
        hj = layer_norm(ALPHA * hj + gt * y, ln_g[i, 0], ln_b[i, 0])
        f = hj * (1.0 + sc2) + sh2
        y2 = moe(i, f.reshape(bsz * L_JOINT, D_MODEL), router_w, router_b, moe_w_gate, moe_w_up, moe_w_down)
        hj = layer_norm(ALPHA * hj + gt2 * y2.reshape(bsz, L_JOINT, D_MODEL), ln_g[i, 1], ln_b[i, 1])
    return hj[:, CTX_LEN:]
```

```python
import functools
import math

import numpy as np
import jax
import jax.numpy as jnp
from jax import lax
from jax.experimental import pallas as pl
from jax.experimental.pallas import tpu as pltpu

F32 = jnp.float32
BF16 = jnp.bfloat16

D_MODEL = 1024
BATCH = 8
SEQ = 2048
DEPTH = 4
GRID_W = 64
CTX_LEN = 256
L_JOINT = SEQ + CTX_LEN
N_MIXERS = 3
ALPHA = (2.0 * DEPTH) ** 0.25
LN_EPS = 1e-6

D_RNN = 1408
RG_BLOCKS = 16
RG_BS = D_RNN // RG_BLOCKS
RG_C = 8.0

D_M = 2 * D_MODEL
M_HEADS = 8
M_DK = 128
M_DV = D_M // M_HEADS
CHUNK = 128

H_EMB = 33
H_BANDS = (H_EMB - 1) // 2
H_DECAY_TARGET = 1e-2
H_FAST = 0.3
H_SLOW = 1.5

N_EXPERTS = 16
N_GROUPS = 4
EXP_PER_GROUP = N_EXPERTS // N_GROUPS
TOP_K = 2
D_EXPERT = 512
ROUTER_PAD = 128

VMEM_LIMIT_BYTES = 56 * 1024 * 1024
ROW_TILE = 256
MOE_TILE = 256
SCAN_TBLK = 64
TILES_PER_BATCH = L_JOINT // ROW_TILE
LATENT_TILES = SEQ // ROW_TILE

MOD_SH, MOD_SC, MOD_GT, MOD_SH2, MOD_SC2, MOD_GT2 = range(6)


def _params(*sem):
    return pltpu.CompilerParams(dimension_semantics=sem, vmem_limit_bytes=VMEM_LIMIT_BYTES)


def _mod_index(i):
    return (2 * (i // TILES_PER_BATCH) + (i % TILES_PER_BATCH) // LATENT_TILES, 0, 0)


def _layer_norm_rows(r, g, b):
    mu = jnp.mean(r, axis=-1, keepdims=True)
    var = jnp.mean(jnp.square(r - mu), axis=-1, keepdims=True)
    return (r - mu) * lax.rsqrt(var + LN_EPS) * g + b


def _mm_body(x_ref, w_ref, o_ref):
    o_ref[...] = jnp.dot(x_ref[...].astype(BF16), w_ref[...], preferred_element_type=F32).astype(o_ref.dtype)


def mm(x, w, *, tm=512, tn=None, out_dtype=F32):
    m, k = x.shape
    n = w.shape[1]
    tn = n if tn is None else tn
    assert m % tm == 0 and n % tn == 0
    return pl.pallas_call(
        _mm_body,
        out_shape=jax.ShapeDtypeStruct((m, n), out_dtype),
        grid=(n // tn, m // tm),
        in_specs=[pl.BlockSpec((tm, k), lambda j, i: (i, 0)),
                  pl.BlockSpec((k, tn), lambda j, i: (0, j))],
        out_specs=pl.BlockSpec((tm, tn), lambda j, i: (i, j)),
        compiler_params=_params("parallel", "parallel"),
        name="mm",
    )(x, w)


def _bmm_body(a_ref, x_ref, o_ref):
    o_ref[...] = jnp.dot(a_ref[...], x_ref[...].astype(BF16), preferred_element_type=F32).astype(o_ref.dtype)


def bmm_left(a, x, *, tn=256, out_dtype=F32):
    mo, k = a.shape
    b, _, d = x.shape
    assert d % tn == 0
    return pl.pallas_call(
        _bmm_body,
        out_shape=jax.ShapeDtypeStruct((b, mo, d), out_dtype),
        grid=(b, d // tn),
        in_specs=[pl.BlockSpec((mo, k), lambda i, j: (0, 0)),
                  pl.BlockSpec((None, k, tn), lambda i, j: (i, 0, j))],
        out_specs=pl.BlockSpec((None, mo, tn), lambda i, j: (i, 0, j)),
        compiler_params=_params("parallel", "parallel"),
        name="dft_mm",
    )(a, x)


def _post_body(z_ref, w_ref, h_ref, mod_ref, lng_ref, lnb_ref, rwh_ref, rwl_ref, ho_ref, f_ref, lg_ref):
    y = jnp.dot(z_ref[...].astype(BF16), w_ref[...], preferred_element_type=F32)
    gt = mod_ref[MOD_GT:MOD_GT + 1, :]
    hn = _layer_norm_rows(ALPHA * h_ref[...] + gt * y, lng_ref[...], lnb_ref[...])
    ho_ref[...] = hn
    f = hn * (1.0 + mod_ref[MOD_SC2:MOD_SC2 + 1, :]) + mod_ref[MOD_SH2:MOD_SH2 + 1, :]
    f_ref[...] = f
    f_hi = f.astype(BF16)
    f_lo = (f - f_hi.astype(F32)).astype(BF16)
    rwh = rwh_ref[...]
    lg_ref[...] = (jnp.dot(f_hi, rwh, preferred_element_type=F32)
                   + jnp.dot(f_lo, rwh, preferred_element_type=F32)
                   + jnp.dot(f_hi, rwl_ref[...], preferred_element_type=F32))


def post_mixer(z, w_out, h, modtab, ln_g, ln_b, rw_hi, rw_lo):
    t, k = z.shape
    d = w_out.shape[1]
    row = lambda i: (i, 0)
    fixed = lambda i: (0, 0)
    return pl.pallas_call(
        _post_body,
        out_shape=(jax.ShapeDtypeStruct((t, d), F32), jax.ShapeDtypeStruct((t, d), F32),
                   jax.ShapeDtypeStruct((t, ROUTER_PAD), F32)),
        grid=(t // ROW_TILE,),
        in_specs=[pl.BlockSpec((ROW_TILE, k), row),
                  pl.BlockSpec((k, d), fixed),
                  pl.BlockSpec((ROW_TILE, d), row),
                  pl.BlockSpec((None, 6, d), _mod_index),
                  pl.BlockSpec((1, d), fixed),
                  pl.BlockSpec((1, d), fixed),
                  pl.BlockSpec((d, ROUTER_PAD), fixed),
                  pl.BlockSpec((d, ROUTER_PAD), fixed)],
        out_specs=(pl.BlockSpec((ROW_TILE, d), row), pl.BlockSpec((ROW_TILE, d), row),
                   pl.BlockSpec((ROW_TILE, ROUTER_PAD), row)),
        compiler_params=_params("parallel"),
        name="post_mixer",
    )(z, w_out, h, modtab, ln_g, ln_b, rw_hi, rw_lo)


def _combine_body(h_ref, ya_ref, yb_ref, w_ref, mod_ref, lng_ref, lnb_ref, o_ref):
    w = w_ref[...]
    y2 = w[:, 0:1] * ya_ref[...] + w[:, 1:2] * yb_ref[...]
    gt2 = mod_ref[MOD_GT2:MOD_GT2 + 1, :]
    o_ref[...] = _layer_norm_rows(ALPHA * h_ref[...] + gt2 * y2, lng_ref[...], lnb_ref[...])


def moe_combine(h, ya, yb, wts, modtab, ln_g, ln_b):
    t, d = h.shape
    row = lambda i: (i, 0)
    fixed = lambda i: (0, 0)
    return pl.pallas_call(
        _combine_body,
        out_shape=jax.ShapeDtypeStruct((t, d), F32),
        grid=(t // ROW_TILE,),
        in_specs=[pl.BlockSpec((ROW_TILE, d), row), pl.BlockSpec((ROW_TILE, d), row),
                  pl.BlockSpec((ROW_TILE, d), row), pl.BlockSpec((ROW_TILE, TOP_K), row),
                  pl.BlockSpec((None, 6, d), _mod_index),
                  pl.BlockSpec((1, d), fixed), pl.BlockSpec((1, d), fixed)],
        out_specs=pl.BlockSpec((ROW_TILE, d), row),
        compiler_params=_params("parallel"),
        name="moe_combine",
    )(h, ya, yb, wts, modtab, ln_g, ln_b)


def _scan_body(af_ref, uf_ref, ab_ref, ub_ref, of_ref, ob_ref, st_ref):
    @pl.when(pl.program_id(0) == 0)
    def _():
        st_ref[...] = jnp.zeros_like(st_ref)

    tblk = af_ref.shape[0]

    def body(t, carry):
        hf, hb = carry
        hf = af_ref[t] * hf + uf_ref[t]
        of_ref[t] = hf
        tb = tblk - 1 - t
        hb = ab_ref[tb] * hb + ub_ref[tb]
        ob_ref[tb] = hb
        return hf, hb

    hf, hb = lax.fori_loop(0, tblk, body, (st_ref[0], st_ref[1]), unroll=8)
    st_ref[0] = hf
    st_ref[1] = hb


def rglru_scan(a_f, u_f, a_b, u_b):
    lj, b, c = a_f.shape
    nblk = lj // SCAN_TBLK
    nlat = SEQ // SCAN_TBLK
    fwd = pl.BlockSpec((SCAN_TBLK, b, c), lambda i: ((i + nlat) % nblk, 0, 0))
    bwd = pl.BlockSpec((SCAN_TBLK, b, c), lambda i: (nblk - 1 - i, 0, 0))
    return pl.pallas_call(
        _scan_body,
        out_shape=(jax.ShapeDtypeStruct((lj, b, c), F32), jax.ShapeDtypeStruct((lj, b, c), F32)),
        grid=(nblk,),
        in_specs=[fwd, fwd, bwd, bwd],
        out_specs=(fwd, bwd),
        scratch_shapes=[pltpu.VMEM((2, b, c), F32)],
        compiler_params=_params("arbitrary"),
        name="rglru_scan",
    )(a_f, u_f, a_b, u_b)


def _split_dot(tri, x, lhs_tri):
    hi = x.astype(BF16)
    lo = (x - hi.astype(F32)).astype(BF16)
    if lhs_tri:
        return jnp.dot(tri, hi, preferred_element_type=F32) + jnp.dot(tri, lo, preferred_element_type=F32)
    return jnp.dot(hi, tri, preferred_element_type=F32) + jnp.dot(lo, tri, preferred_element_type=F32)


def _mlstm_body(q_ref, k_ref, v_ref, gc_ref, gr_ref, ng_ref, o_ref, ct_ref, n_ref, m_ref):
    row = lax.broadcasted_iota(jnp.int32, (CHUNK, CHUNK), 0)
    col = lax.broadcasted_iota(jnp.int32, (CHUNK, CHUNK), 1)
    lower = col <= row
    upper = col >= row
    tri_lower = jnp.where(lower, 1.0, 0.0).astype(BF16)
    tri_upper = jnp.where(upper, 1.0, 0.0).astype(BF16)
    ng = ng_ref[...]

    def chunk(c, z, reverse):
        c0 = pl.multiple_of(c * CHUNK, CHUNK)
        q = q_ref[pl.ds(c0, CHUNK), :]
        k = k_ref[pl.ds(c0, CHUNK), :]
        v = v_ref[pl.ds(c0, CHUNK), :]
        gc = gc_ref[pl.ds(c0, CHUNK), :]
        gr = gr_ref[:, pl.ds(c0, CHUNK)]
        ig_c = gc[:, 2 * z:2 * z + 1]
        lf_c = gc[:, 2 * z + 1:2 * z + 2]
        ig_r = gr[2 * z:2 * z + 1, :]
        if reverse:
            bc_c = _split_dot(tri_upper, gc, True)[:, 2 * z + 1:2 * z + 2]
            bc_r = _split_dot(tri_lower, gr, False)[2 * z + 1:2 * z + 2, :]
            mask = upper
        else:
            bc_c = _split_dot(tri_lower, gc, True)[:, 2 * z + 1:2 * z + 2]
            bc_r = _split_dot(tri_upper, gr, False)[2 * z + 1:2 * z + 2, :]
            mask = lower
        btot = jnp.sum(lf_c, axis=0, keepdims=True)
        m_prev = m_ref[...]
        n_prev = n_ref[...]
        ct_prev = ct_ref[...]
        dlog = jnp.where(mask, bc_c - bc_r + ig_r, -jnp.inf)
        m_inter = bc_c + m_prev
        m_comb = jnp.maximum(m_inter, jnp.max(dlog, axis=1, keepdims=True))
        qk = lax.dot_general(q, k, (((1,), (1,)), ((), ())), preferred_element_type=F32)
        s = qk * jnp.exp(dlog - m_comb)
        inter = jnp.exp(m_inter - m_comb)
        num = (jnp.dot(s.astype(BF16), v, preferred_element_type=F32)
               + inter * jnp.dot(q, ct_prev.astype(BF16), preferred_element_type=F32))
        den = (jnp.sum(s, axis=1, keepdims=True)
               + inter * jnp.sum(q.astype(F32) * n_prev, axis=1, keepdims=True))
        h = num / jnp.maximum(jnp.abs(den), jnp.exp(-m_comb))
        wlog = btot - bc_c + ig_c
        mloc = jnp.max(wlog, axis=0, keepdims=True)
        wgt = jnp.exp(wlog - mloc)
        m_new = jnp.maximum(btot + m_prev, mloc)
        sp = jnp.exp(btot + m_prev - m_new)
        sl = jnp.exp(mloc - m_new)
        kf = k.astype(F32)
        vw = (v.astype(F32) * wgt).astype(BF16)
        ct_loc = jnp.dot(kf.T.astype(BF16), vw, preferred_element_type=F32)
        ct_ref[...] = sp * ct_prev + sl * ct_loc
        n_ref[...] = sp * n_prev + sl * jnp.sum(kf * wgt, axis=0, keepdims=True)
        m_ref[...] = m_new
        return c0, h

    def reset():
        ct_ref[...] = jnp.zeros_like(ct_ref)
        n_ref[...] = jnp.zeros_like(n_ref)
        m_ref[...] = jnp.zeros_like(m_ref)

    n_chunks = o_ref.shape[0] // CHUNK
    n_lat = SEQ // CHUNK

    reset()

    def fwd(i, carry):
        c0, h = chunk((i + n_lat) % n_chunks, 0, False)
        o_ref[pl.ds(c0, CHUNK), :] = h
        return carry

    lax.fori_loop(0, n_chunks, fwd, 0)

    reset()

    def bwd(i, carry):
        c0, h = chunk(n_chunks - 1 - i, 1, True)
        tot = o_ref[pl.ds(c0, CHUNK), :] + h
        mu = jnp.mean(tot, axis=1, keepdims=True)
        var = jnp.mean(jnp.square(tot - mu), axis=1, keepdims=True)
        o_ref[pl.ds(c0, CHUNK), :] = (tot - mu) * lax.rsqrt(var + LN_EPS) * ng
        return carry

    lax.fori_loop(0, n_chunks, bwd, 0)


def mlstm_cell(q, k, v, g_col, g_row, norm_g):
    b, lj, _ = q.shape
    return pl.pallas_call(
        _mlstm_body,
        out_shape=jax.ShapeDtypeStruct((b, lj, M_HEADS * M_DV), F32),
        grid=(b, M_HEADS),
        in_specs=[pl.BlockSpec((None, lj, M_DK), lambda i, h: (i, 0, h)),
                  pl.BlockSpec((None, lj, M_DK), lambda i, h: (i, 0, h)),
                  pl.BlockSpec((None, lj, M_DV), lambda i, h: (i, 0, h)),
                  pl.BlockSpec((None, None, lj, 4), lambda i, h: (i, h, 0, 0)),
                  pl.BlockSpec((None, None, 4, lj), lambda i, h: (i, h, 0, 0)),
                  pl.BlockSpec((1, M_DV), lambda i, h: (0, h))],
        out_specs=pl.BlockSpec((None, lj, M_DV), lambda i, h: (i, 0, h)),
        scratch_shapes=[pltpu.VMEM((M_DK, M_DV), F32), pltpu.VMEM((1, M_DK), F32), pltpu.VMEM((1, 1), F32)],
        compiler_params=_params("parallel", "parallel"),
        name="mlstm_cell",
    )(q, k, v, g_col, g_row, norm_g)


def _moe_body(te_ref, tf_ref, nu_ref, x_ref, wg_ref, wu_ref, wd_ref, o_ref, wg_s, wu_s, wd_s):
    i = pl.program_id(0)

    @pl.when(i < nu_ref[0])
    def _():
        @pl.when(tf_ref[i] == 1)
        def _():
            wg_s[...] = wg_ref[...].astype(BF16)
            wu_s[...] = wu_ref[...].astype(BF16)
            wd_s[...] = wd_ref[...].astype(BF16)

        x = x_ref[...].astype(BF16)
        a = jnp.dot(x, wg_s[...], preferred_element_type=F32)
        u = jnp.dot(x, wu_s[...], preferred_element_type=F32)
        hid = (a * jax.nn.sigmoid(a)) * u
        o_ref[...] = jnp.dot(hid.astype(BF16), wd_s[...], preferred_element_type=F32)

    @pl.when(i >= nu_ref[0])
    def _():
        o_ref[...] = jnp.zeros_like(o_ref)


def moe_experts(layer, tile_expert, tile_first, n_used, xs, w_gate, w_up, w_down):
    n_rows, d = xs.shape
    n_tiles = n_rows // MOE_TILE
    wmap_in = lambda i, te, tf, nu: (layer, te[i], 0, 0)
    grid_spec = pltpu.PrefetchScalarGridSpec(
        num_scalar_prefetch=3,
        grid=(n_tiles,),
        in_specs=[pl.BlockSpec((MOE_TILE, d), lambda i, te, tf, nu: (i, 0)),
                  pl.BlockSpec((None, None, d, D_EXPERT), wmap_in),
                  pl.BlockSpec((None, None, d, D_EXPERT), wmap_in),
                  pl.BlockSpec((None, None, D_EXPERT, d), wmap_in)],
        out_specs=pl.BlockSpec((MOE_TILE, d), lambda i, te, tf, nu: (i, 0)),
        scratch_shapes=[pltpu.VMEM((d, D_EXPERT), BF16), pltpu.VMEM((d, D_EXPERT), BF16),
                        pltpu.VMEM((D_EXPERT, d), BF16)],
    )
    return pl.pallas_call(
        _moe_body,
        out_shape=jax.ShapeDtypeStruct((n_rows, d), F32),
        grid_spec=grid_spec,
        compiler_params=_params("arbitrary"),
        name="moe_experts",
    )(tile_expert, tile_first, n_used, xs, w_gate, w_up, w_down)


def moe(layer, f, logits, router_b, w_gate, w_up, w_down):
    t = f.shape[0]
    scores = jax.nn.sigmoid(logits[:, :N_EXPERTS])
    sel = (scores + router_b).reshape(t, N_GROUPS, EXP_PER_GROUP)
    group_score = jnp.sum(lax.top_k(sel, TOP_K)[0], -1)
    g_best = jnp.argmax(group_score, -1)
    sel_in = jnp.take_along_axis(sel, g_best[:, None, None], axis=1)[:, 0]
    _, loc = lax.top_k(sel_in, TOP_K)
    idx = g_best[:, None] * EXP_PER_GROUP + loc
    wts = jnp.take_along_axis(scores, idx, -1)
    wts = wts / jnp.sum(wts, -1, keepdims=True)

    n_pick = t * TOP_K
    n_tiles = n_pick // MOE_TILE + N_EXPERTS
    n_rows = n_tiles * MOE_TILE
    flat_e = idx.reshape(-1).astype(jnp.int32)
    onehot = (flat_e[:, None] == jnp.arange(N_EXPERTS, dtype=jnp.int32)[None, :]).astype(jnp.int32)
    csum = jnp.cumsum(onehot, axis=0)
    counts = csum[-1]
    rank = jnp.sum(csum * onehot, axis=1) - 1
    tiles_per = (counts + MOE_TILE - 1) // MOE_TILE
    tile_end = jnp.cumsum(tiles_per)
    tile_start = tile_end - tiles_per
    dest = tile_start[flat_e] * MOE_TILE + rank
    flat_t = jnp.arange(n_pick, dtype=jnp.int32) // TOP_K
    row_token = jnp.zeros((n_rows,), jnp.int32).at[dest].set(flat_t)
    n_used = tile_end[-1]
    tile_ids = jnp.arange(n_tiles, dtype=jnp.int32)
    tile_expert = jnp.searchsorted(tile_end, jnp.minimum(tile_ids, n_used - 1), side="right").astype(jnp.int32)
    tile_expert = jnp.minimum(tile_expert, N_EXPERTS - 1)
    tile_first = ((tile_ids == tile_start[tile_expert]) & (tile_ids < n_used)).astype(jnp.int32)

    xs = jnp.take(f, row_token, axis=0)
    ys = moe_experts(layer, tile_expert, tile_first, n_used.reshape(1).astype(jnp.int32), xs, w_gate, w_up, w_down)
    pos = dest.reshape(t, TOP_K)
    return jnp.take(ys, pos[:, 0], axis=0), jnp.take(ys, pos[:, 1], axis=0), wts


def sincos_2d(n_tok):
    rows = n_tok // GRID_W
    quarter = D_MODEL // 4
    omega = 1.0 / (10000.0 ** (jnp.arange(quarter, dtype=F32) / quarter))
    ar = jnp.arange(rows, dtype=F32)[:, None] * omega
    ac = jnp.arange(GRID_W, dtype=F32)[:, None] * omega
    er = jnp.concatenate([jnp.sin(ar), jnp.cos(ar)], -1)
    ec = jnp.concatenate([jnp.sin(ac), jnp.cos(ac)], -1)
    half = D_MODEL // 2
    pos = jnp.concatenate([jnp.broadcast_to(er[:, None], (rows, GRID_W, half)),
                           jnp.broadcast_to(ec[None], (rows, GRID_W, half))], -1)
    return pos.reshape(rows * GRID_W, D_MODEL)


def _dwconv_seg(x, w, b, axis):
    kk = w.shape[0]
    left = kk // 2
    n = x.shape[axis]
    pad = [(0, 0)] * x.ndim
    pad[axis] = (left, kk - 1 - left)
    xp = jnp.pad(x, pad)
    y = b
    for j in range(kk):
        y = y + lax.slice_in_dim(xp, j, j + n, axis=axis) * w[j]
    return y


def dwconv_joint(x, w, b, axis):
    xl = lax.slice_in_dim(x, 0, SEQ, axis=axis)
    xc = lax.slice_in_dim(x, SEQ, L_JOINT, axis=axis)
    return jnp.concatenate([_dwconv_seg(xl, w, b, axis), _dwconv_seg(xc, w, b, axis)], axis=axis)


def rglru_mixer(inp, w_in, conv_w, conv_b, ga_w, ga_b, gx_w, gx_b, lam):
    b = inp.shape[0]
    tm_in = jnp.transpose(inp, (1, 0, 2)).reshape(L_JOINT * b, D_MODEL)
    proj = mm(tm_in, w_in.astype(BF16), tn=D_RNN)
    gate = jax.nn.gelu(proj[:, :D_RNN])
    rec = proj[:, D_RNN:].reshape(L_JOINT, b, D_RNN)
    xc = dwconv_joint(rec, conv_w, conv_b, 0)
    xc_flat = xc.reshape(L_JOINT * b, D_RNN)

    def block_diag(w):
        eye = jnp.eye(RG_BLOCKS, dtype=w.dtype)
        return jnp.einsum("nkj,nm->nkmj", w, eye).reshape(D_RNN, D_RNN)

    w_gates = jnp.concatenate([block_diag(ga_w[0]), block_diag(gx_w[0]),
                               block_diag(ga_w[1]), block_diag(gx_w[1])], axis=1).astype(BF16)
    pre = mm(xc_flat, w_gates, tn=D_RNN).reshape(L_JOINT, b, 4, D_RNN)

    def gates(z):
        r = jax.nn.sigmoid(pre[:, :, 2 * z] + ga_b[z])
        i = jax.nn.sigmoid(pre[:, :, 2 * z + 1] + gx_b[z])
        log_a = -RG_C * r * jax.nn.softplus(-lam[z])
        u = jnp.sqrt(-jnp.expm1(2.0 * log_a)) * (i * xc)
        return jnp.exp(log_a), u

    a_f, u_f = gates(0)
    a_b, u_b = gates(1)
    h_f, h_b = rglru_scan(a_f, u_f, a_b, u_b)
    z = gate.reshape(L_JOINT, b, D_RNN) * (h_f + h_b)
    return jnp.transpose(z, (1, 0, 2)).reshape(b * L_JOINT, D_RNN)


def mlstm_mixer(inp, w_up, conv_w, conv_b, w_q, w_k, w_v, w_o, w_if, b_if, norm_g, skip):
    b = inp.shape[0]
    rows = b * L_JOINT
    xm = mm(inp.reshape(rows, D_MODEL), w_up.astype(BF16))
    xc = jax.nn.silu(dwconv_joint(xm.reshape(b, L_JOINT, D_M), conv_w, conv_b, 1)).reshape(rows, D_M)
    qk = mm(xc, jnp.concatenate([w_q, w_k], axis=1).astype(BF16), tn=M_HEADS * M_DK)
    q = qk[:, :M_HEADS * M_DK].astype(BF16).reshape(b, L_JOINT, M_HEADS * M_DK)
    k = (qk[:, M_HEADS * M_DK:] * (M_DK ** -0.5)).astype(BF16).reshape(b, L_JOINT, M_HEADS * M_DK)
    n_gate = 4 * M_HEADS
    w_g = jnp.concatenate([w_if[0], w_if[1], jnp.zeros((D_M, 128 - n_gate), F32)], axis=1)
    vog = mm(xm, jnp.concatenate([w_v, w_o, w_g], axis=1).astype(BF16), tn=(2 * D_M + 128) // 3)
    v = vog[:, :D_M].astype(BF16).reshape(b, L_JOINT, D_M)
    o = jax.nn.sigmoid(vog[:, D_M:2 * D_M])
    g = vog[:, 2 * D_M:2 * D_M + n_gate].reshape(b, L_JOINT, 2, 2 * M_HEADS) + b_if
    ig = g[..., :M_HEADS]
    lf = jax.nn.log_sigmoid(g[..., M_HEADS:])
    g4 = jnp.stack([ig[:, :, 0], lf[:, :, 0], ig[:, :, 1], lf[:, :, 1]], axis=-1)
    g_col = jnp.transpose(g4, (0, 2, 1, 3))
    g_row = jnp.transpose(g4, (0, 2, 3, 1))
    hn = mlstm_cell(q, k, v, g_col, g_row, norm_g.reshape(1, D_M)).reshape(rows, D_M)
    return o * hn + skip * xc


def _dft_mats(n_time):
    n = 3 * n_time // 2
    half = n // 2
    kk = np.arange(half, dtype=np.int64)[:, None]
    tt = np.arange(n_time, dtype=np.int64)[None, :]
    ang = 2.0 * np.pi * ((kk * tt) % n).astype(np.float64) / n
    top = np.cos(ang)
    bot = -np.sin(ang)
    bot[0] = np.cos(np.pi * tt[0])
    fwd = np.concatenate([top, bot], axis=0)
    mm_ = (np.arange(n_time, dtype=np.int64) + n_time // 2)[:, None]
    ang2 = 2.0 * np.pi * ((mm_ * kk.T) % n).astype(np.float64) / n
    wk = np.full((1, half), 2.0)
    wk[0, 0] = 1.0
    itop = wk * np.cos(ang2) / n
    ibot = -2.0 * np.sin(ang2) / n
    ibot[:, 0] = np.cos(np.pi * mm_[:, 0]) / n
    inv = np.concatenate([itop, ibot], axis=1)
    return jnp.asarray(fwd, dtype=BF16), jnp.asarray(inv, dtype=BF16)


def hyena_filters(n_time, w1, b1, fq1, w2, b2, fq2, w3):
    hp = lax.Precision.HIGHEST
    t01 = jnp.linspace(0.0, 1.0, n_time, dtype=F32)
    bands = jnp.linspace(1e-4, H_BANDS - 1, H_BANDS, dtype=F32)
    ang = (2.0 * math.pi / n_time) * jnp.arange(n_time, dtype=F32)[:, None] * bands[None, :]
    z = jnp.concatenate([t01[:, None], jnp.cos(ang), -jnp.sin(ang)], -1)
    hdn = jnp.sin(fq1 * (jnp.dot(z, w1, precision=hp) + b1))
    hdn = jnp.sin(fq2 * (jnp.dot(hdn, w2, precision=hp) + b2))
    filt = jnp.dot(hdn, w3, precision=hp).reshape(n_time, 2, D_MODEL)
    dist = jnp.abs(jnp.arange(n_time) - n_time // 2).astype(F32) * (2.0 / n_time)
    d_max = math.log(H_DECAY_TARGET) / H_FAST
    d_min = math.log(H_DECAY_TARGET) / H_SLOW
    deltas = jnp.abs(jnp.linspace(d_min, d_max, D_MODEL, dtype=F32))
    window = jnp.exp(-dist[:, None] * deltas[None, :])
    return filt * window[:, None, :]


def _spec_mul(x, hf):
    half = hf.shape[0] // 2
    xt, xb = x[:, :half], x[:, half:]
    ht, hb = hf[:half], hf[half:]
    yt = xt * ht - xb * hb
    yb = xt * hb + xb * ht
    first = (jnp.arange(half) == 0)[None, :, None]
    yt = jnp.where(first, xt * ht, yt)
    yb = jnp.where(first, xb * hb, yb)
    return jnp.concatenate([yt, yb], axis=1)


def hyena_seq(h, w_in_bf, b_in, conv_w, conv_b, fparams, skip):
    b, n_time, _ = h.shape
    u = mm(h.reshape(b * n_time, D_MODEL), w_in_bf, tn=D_MODEL).reshape(b, n_time, 3 * D_MODEL) + b_in
    u = _dwconv_seg(u, conv_w, conv_b, 1)
    v, g1, g2 = jnp.split(u, 3, axis=-1)
    filt = hyena_filters(n_time, *fparams)
    fwd, inv = _dft_mats(n_time)
    hf = bmm_left(fwd, filt.reshape(1, n_time, 2 * D_MODEL))[0]

    def conv(sig, j):
        spec = bmm_left(fwd, sig)
        prod = _spec_mul(spec, hf[:, j * D_MODEL:(j + 1) * D_MODEL])
        return bmm_left(inv, prod) + sig * skip[j]

    z = g1 * conv(v, 0)
    return g2 * conv(z, 1)


def kernel(x, c, ctx, c_ctx, router_w, router_b, ada_w, ada_b, ln_g, ln_b, moe_w_gate, moe_w_up, moe_w_down, rg_w_in, rg_conv_w, rg_conv_b, rg_gate_a_w, rg_gate_a_b, rg_gate_x_w, rg_gate_x_b, rg_lambda, rg_w_out, ml_w_up, ml_conv_w, ml_conv_b, ml_w_q, ml_w_k, ml_w_v, ml_w_o, ml_w_if, ml_b_if, ml_norm_g, ml_skip, ml_w_down, hy_w_in, hy_b_in, hy_conv_w, hy_conv_b, hy_f_w1, hy_f_b1, hy_f_freq1, hy_f_w2, hy_f_b2, hy_f_freq2, hy_f_w3, hy_skip, hy_w_out):
    bsz = x.shape[0]
    rows = bsz * L_JOINT
    hx = x + sincos_2d(SEQ)[None]
    hj = jnp.concatenate([hx, ctx], axis=1).reshape(rows, D_MODEL)
    is_ctx = (jnp.arange(L_JOINT) >= SEQ)[None, :, None]

    cond = jnp.concatenate([jax.nn.silu(c), jax.nn.silu(c_ctx)[None],
                            jnp.zeros((16 - bsz - 1, D_MODEL), F32)], axis=0)
    rw_pad = jnp.concatenate([router_w, jnp.zeros((D_MODEL, ROUTER_PAD - N_EXPERTS), F32)], axis=1)
    rw_hi = rw_pad.astype(BF16)
    rw_lo = (rw_pad - rw_hi.astype(F32)).astype(BF16)

    for i in range(DEPTH):
        kind, j = i % N_MIXERS, i // N_MIXERS
        mod = (mm(cond, ada_w[i].astype(BF16), tm=16, tn=D_MODEL) + ada_b[i]).reshape(16, 6, D_MODEL)
        mod_x = mod[:bsz]
        mod_c = jnp.broadcast_to(mod[bsz][None], (bsz, 6, D_MODEL))
        modtab = jnp.stack([mod_x, mod_c], axis=1).reshape(2 * bsz, 6, D_MODEL)
        sh = jnp.where(is_ctx, mod_c[:, None, MOD_SH], mod_x[:, None, MOD_SH])
        sc = jnp.where(is_ctx, mod_c[:, None, MOD_SC], mod_x[:, None, MOD_SC])
        inp = hj.reshape(bsz, L_JOINT, D_MODEL) * (1.0 + sc) + sh
        if kind == 0:
            z = rglru_mixer(inp, rg_w_in[j], rg_conv_w[j], rg_conv_b[j], rg_gate_a_w[j], rg_gate_a_b[j],
                            rg_gate_x_w[j], rg_gate_x_b[j], rg_lambda[j])
            w_out = rg_w_out[j]
        elif kind == 1:
            z = mlstm_mixer(inp, ml_w_up[j], ml_conv_w[j], ml_conv_b[j], ml_w_q[j], ml_w_k[j], ml_w_v[j],
                            ml_w_o[j], ml_w_if[j], ml_b_if[j], ml_norm_g[j], ml_skip[j])
            w_out = ml_w_down[j]
        else:
            fparams = (hy_f_w1[j], hy_f_b1[j], hy_f_freq1[j], hy_f_w2[j], hy_f_b2[j], hy_f_freq2[j], hy_f_w3[j])
            args = (hy_w_in[j].astype(BF16), hy_b_in[j], hy_conv_w[j], hy_conv_b[j], fparams, hy_skip[j])
            z = jnp.concatenate([hyena_seq(inp[:, :SEQ], *args), hyena_seq(inp[:, SEQ:], *args)],
                                axis=1).reshape(rows, D_MODEL)
            w_out = hy_w_out[j]
        hj, f, logits = post_mixer(z, w_out.astype(BF16), hj, modtab, ln_g[i, 0][None], ln_b[i, 0][None],
                                   rw_hi, rw_lo)
        ya, yb, wts = moe(i, f, logits, router_b, moe_w_gate, moe_w_up, moe_w_down)
        hj = moe_combine(hj, ya, yb, wts, modtab, ln_g[i, 1][None], ln_b[i, 1][None])
    return hj.reshape(bsz, L_JOINT, D_MODEL)[:, :SEQ]
```

```python
import functools
import math

import numpy as np
import jax
import jax.numpy as jnp
from jax import lax
from jax.experimental import pallas as pl
from jax.experimental.pallas import tpu as pltpu

F32 = jnp.float32
BF16 = jnp.bfloat16

D_MODEL = 1024
BATCH = 8
SEQ = 2048
DEPTH = 4
GRID_W = 64
CTX_LEN = 256
L_JOINT = SEQ + CTX_LEN
N_MIXERS = 3
ALPHA = (2.0 * DEPTH) ** 0.25
LN_EPS = 1e-6

D_RNN = 1408
RG_BLOCKS = 16
RG_BS = D_RNN // RG_BLOCKS
RG_C = 8.0

D_M = 2 * D_MODEL
M_HEADS = 8
M_DK = 128
M_DV = D_M // M_HEADS
CHUNK = 128

H_EMB = 33
H_BANDS = (H_EMB - 1) // 2
H_DECAY_TARGET = 1e-2
H_FAST = 0.3
H_SLOW = 1.5

N_EXPERTS = 16
N_GROUPS = 4
EXP_PER_GROUP = N_EXPERTS // N_GROUPS
TOP_K = 2
D_EXPERT = 512
ROUTER_PAD = 128

VMEM_LIMIT_BYTES = 56 * 1024 * 1024
ROW_TILE = 256
MOE_TILE = 256
RG_TBLK = 32
RG_LANES = D_RNN // 128
RG_CONV = 4
TILES_PER_BATCH = L_JOINT // ROW_TILE
LATENT_TILES = SEQ // ROW_TILE

MOD_SH, MOD_SC, MOD_GT, MOD_SH2, MOD_SC2, MOD_GT2 = range(6)


def _params(*sem):
    return pltpu.CompilerParams(dimension_semantics=sem, vmem_limit_bytes=VMEM_LIMIT_BYTES)


def _mod_index(i):
    return (2 * (i // TILES_PER_BATCH) + (i % TILES_PER_BATCH) // LATENT_TILES, 0, 0)


def _layer_norm_rows(r, g, b):
    mu = jnp.mean(r, axis=-1, keepdims=True)
    var = jnp.mean(jnp.square(r - mu), axis=-1, keepdims=True)
    return (r - mu) * lax.rsqrt(var + LN_EPS) * g + b


def _mm_body(x_ref, w_ref, o_ref):
    o_ref[...] = jnp.dot(x_ref[...].astype(BF16), w_ref[...], preferred_element_type=F32).astype(o_ref.dtype)


def mm(x, w, *, tm=512, tn=None, out_dtype=F32):
    m, k = x.shape
    n = w.shape[1]
    tn = n if tn is None else tn
    assert m % tm == 0 and n % tn == 0
    return pl.pallas_call(
        _mm_body,
        out_shape=jax.ShapeDtypeStruct((m, n), out_dtype),
        grid=(n // tn, m // tm),
        in_specs=[pl.BlockSpec((tm, k), lambda j, i: (i, 0)),
                  pl.BlockSpec((k, tn), lambda j, i: (0, j))],
        out_specs=pl.BlockSpec((tm, tn), lambda j, i: (i, j)),
        compiler_params=_params("parallel", "parallel"),
        name="mm",
    )(x, w)


def _bmm_body(a_ref, x_ref, o_ref):
    o_ref[...] = jnp.dot(a_ref[...], x_ref[...].astype(BF16), preferred_element_type=F32).astype(o_ref.dtype)


def bmm_left(a, x, *, tn=256, out_dtype=F32):
    mo, k = a.shape
    b, _, d = x.shape
    assert d % tn == 0
    return pl.pallas_call(
        _bmm_body,
        out_shape=jax.ShapeDtypeStruct((b, mo, d), out_dtype),
        grid=(b, d // tn),
        in_specs=[pl.BlockSpec((mo, k), lambda i, j: (0, 0)),
                  pl.BlockSpec((None, k, tn), lambda i, j: (i, 0, j))],
        out_specs=pl.BlockSpec((None, mo, tn), lambda i, j: (i, 0, j)),
        compiler_params=_params("parallel", "parallel"),
        name="dft_mm",
    )(a, x)


def _pack_bf16_pairs(f):
    half = f.shape[1] // 2
    hi = lax.bitcast_convert_type(f[:, :half].astype(BF16).astype(F32), jnp.uint32)
    lo = lax.bitcast_convert_type(f[:, half:].astype(BF16).astype(F32), jnp.uint32)
    return hi | (lo >> 16)


def _unpack_bf16_pairs(u):
    hi = lax.bitcast_convert_type(u & jnp.uint32(0xFFFF0000), F32).astype(BF16)
    lo = lax.bitcast_convert_type(u << 16, F32).astype(BF16)
    return jnp.concatenate([hi, lo], axis=1)


def _post_epilogue(y, h, gt, sc2, sh2, ln_g, ln_b, rw_hi, rw_lo):
    hn = _layer_norm_rows(ALPHA * h + gt * y, ln_g, ln_b)
    f = hn * (1.0 + sc2) + sh2
    f_hi = f.astype(BF16)
    f_lo = (f - f_hi.astype(F32)).astype(BF16)
    lg = (jnp.dot(f_hi, rw_hi, preferred_element_type=F32) + jnp.dot(f_lo, rw_hi, preferred_element_type=F32)
          + jnp.dot(f_hi, rw_lo, preferred_element_type=F32))
    return hn, _pack_bf16_pairs(f), lg


def _post_body(z_ref, w_ref, h_ref, mod_ref, lng_ref, lnb_ref, rwh_ref, rwl_ref, ho_ref, f_ref, lg_ref):
    y = jnp.dot(z_ref[...].astype(BF16), w_ref[...], preferred_element_type=F32)
    hn, fpk, lg = _post_epilogue(y, h_ref[...], mod_ref[MOD_GT:MOD_GT + 1, :], mod_ref[MOD_SC2:MOD_SC2 + 1, :],
                                 mod_ref[MOD_SH2:MOD_SH2 + 1, :], lng_ref[...], lnb_ref[...], rwh_ref[...],
                                 rwl_ref[...])
    ho_ref[...] = hn
    f_ref[...] = fpk
    lg_ref[...] = lg


def post_mixer(z, w_out, h, modtab, ln_g, ln_b, rw_hi, rw_lo):
    t, k = z.shape
    d = w_out.shape[1]
    row = lambda i: (i, 0)
    fixed = lambda i: (0, 0)
    return pl.pallas_call(
        _post_body,
        out_shape=(jax.ShapeDtypeStruct((t, d), F32), jax.ShapeDtypeStruct((t, d // 2), jnp.uint32),
                   jax.ShapeDtypeStruct((t, ROUTER_PAD), F32)),
        grid=(t // ROW_TILE,),
        in_specs=[pl.BlockSpec((ROW_TILE, k), row),
                  pl.BlockSpec((k, d), fixed),
                  pl.BlockSpec((ROW_TILE, d), row),
                  pl.BlockSpec((None, 6, d), _mod_index),
                  pl.BlockSpec((1, d), fixed),
                  pl.BlockSpec((1, d), fixed),
                  pl.BlockSpec((d, ROUTER_PAD), fixed),
                  pl.BlockSpec((d, ROUTER_PAD), fixed)],
        out_specs=(pl.BlockSpec((ROW_TILE, d), row), pl.BlockSpec((ROW_TILE, d // 2), row),
                   pl.BlockSpec((ROW_TILE, ROUTER_PAD), row)),
        compiler_params=_params("parallel"),
        name="post_mixer",
    )(z, w_out, h, modtab, ln_g, ln_b, rw_hi, rw_lo)


def _combine_body(h_ref, ya_ref, yb_ref, w_ref, mod_ref, lng_ref, lnb_ref, o_ref):
    w = w_ref[...]
    y2 = w[:, 0:1] * ya_ref[...] + w[:, 1:2] * yb_ref[...]
    gt2 = mod_ref[MOD_GT2:MOD_GT2 + 1, :]
    o_ref[...] = _layer_norm_rows(ALPHA * h_ref[...] + gt2 * y2, lng_ref[...], lnb_ref[...])


def moe_combine(h, ya, yb, wts, modtab, ln_g, ln_b):
    t, d = h.shape
    row = lambda i: (i, 0)
    fixed = lambda i: (0, 0)
    return pl.pallas_call(
        _combine_body,
        out_shape=jax.ShapeDtypeStruct((t, d), F32),
        grid=(t // ROW_TILE,),
        in_specs=[pl.BlockSpec((ROW_TILE, d), row), pl.BlockSpec((ROW_TILE, d), row),
                  pl.BlockSpec((ROW_TILE, d), row), pl.BlockSpec((ROW_TILE, TOP_K), row),
                  pl.BlockSpec((None, 6, d), _mod_index),
                  pl.BlockSpec((1, d), fixed), pl.BlockSpec((1, d), fixed)],
        out_specs=pl.BlockSpec((ROW_TILE, d), row),
        compiler_params=_params("parallel"),
        name="moe_combine",
    )(h, ya, yb, wts, modtab, ln_g, ln_b)


def _rg_in_body(h_ref, mod_ref, w_ref, gate_ref, rec_ref):
    tt = h_ref.shape[1]
    sc = mod_ref[MOD_SC][:, None, :]
    sh = mod_ref[MOD_SH][:, None, :]
    inp = (h_ref[...] * (1.0 + sc) + sh).reshape(BATCH * tt, D_MODEL)
    p = jnp.dot(inp.astype(BF16), w_ref[...], preferred_element_type=F32)
    gate_ref[...] = jax.nn.gelu(p[:, :D_RNN]).astype(gate_ref.dtype).reshape(BATCH, tt, D_RNN)
    for b in range(BATCH):
        rec_b = p[b * tt:(b + 1) * tt, D_RNN:]
        for j in range(RG_LANES):
            rec_ref[j, pl.ds(b, tt, stride=BATCH), :] = rec_b[:, j * 128:(j + 1) * 128]


def rg_in(h3, modtab_t, w_in):
    nblk = L_JOINT // RG_TBLK
    return pl.pallas_call(
        _rg_in_body,
        out_shape=(jax.ShapeDtypeStruct((BATCH, L_JOINT, D_RNN), BF16),
                   jax.ShapeDtypeStruct((RG_LANES, L_JOINT * BATCH, 128), F32)),
        grid=(nblk,),
        in_specs=[pl.BlockSpec((BATCH, RG_TBLK, D_MODEL), lambda i: (0, i, 0)),
                  pl.BlockSpec((None, 6, BATCH, D_MODEL), lambda i: (i // (SEQ // RG_TBLK), 0, 0, 0)),
                  pl.BlockSpec((D_MODEL, 2 * D_RNN), lambda i: (0, 0))],
        out_specs=(pl.BlockSpec((BATCH, RG_TBLK, D_RNN), lambda i: (0, i, 0)),
                   pl.BlockSpec((RG_LANES, RG_TBLK * BATCH, 128), lambda i: (0, i, 0))),
        compiler_params=_params("parallel"),
        name="rg_in",
    )(h3, modtab_t, w_in)


def _rg_block(z, i):
    nblk = L_JOINT // RG_TBLK
    nlat = SEQ // RG_TBLK
    return jnp.where(z == 0, (i + nlat) % nblk, nblk - 1 - i)


def _rg_scan_body(prev_ref, main_ref, next_ref, cw_ref, cb_ref, wg_ref, ba_ref, bx_ref, lam_ref, ho_ref,
                  a_s, u_s, st_ref):
    z = pl.program_id(0)
    i = pl.program_id(1)
    nblk = L_JOINT // RG_TBLK
    nlat = SEQ // RG_TBLK
    blk = _rg_block(z, i)
    rows = RG_TBLK * BATCH

    @pl.when(i == 0)
    def _():
        st_ref[...] = jnp.zeros_like(st_ref)

    has_prev = jnp.where((blk == 0) | (blk == nlat), 0.0, 1.0)
    has_next = jnp.where((blk == nlat - 1) | (blk == nblk - 1), 0.0, 1.0)
    cols = []
    for j in range(RG_LANES):
        ext = jnp.concatenate([prev_ref[j] * has_prev, main_ref[j], next_ref[j] * has_next], axis=0)
        lane = slice(j * 128, (j + 1) * 128)
        acc = cb_ref[:, lane] + cw_ref[0:1, lane] * ext[0:rows]
        for k in range(1, RG_CONV):
            acc = acc + cw_ref[k:k + 1, lane] * ext[k * BATCH:k * BATCH + rows]
        cols.append(acc)
    xc = jnp.concatenate(cols, axis=1)

    pre = jnp.dot(xc.astype(BF16), wg_ref[...], preferred_element_type=F32)
    r = 0.5 * (jnp.tanh(0.5 * (pre[:, :D_RNN] + ba_ref[...])) + 1.0)
    g = 0.5 * (jnp.tanh(0.5 * (pre[:, D_RNN:] + bx_ref[...])) + 1.0)
    a = jnp.exp((-RG_C * jax.nn.softplus(-lam_ref[...])) * r)
    a_s[...] = a
    u_s[...] = jnp.sqrt(1.0 - a * a) * (g * xc)

    def step(t, carry):
        te = jnp.where(z == 0, t, RG_TBLK - 1 - t)
        r0 = pl.multiple_of(te * BATCH, BATCH)
        out = []
        for j in range(RG_LANES):
            lane = slice(j * 128, (j + 1) * 128)
            h = a_s[pl.ds(r0, BATCH), lane] * carry[j] + u_s[pl.ds(r0, BATCH), lane]
            ho_ref[j, pl.ds(r0, BATCH), :] = h
            out.append(h)
        return tuple(out)

    final = lax.fori_loop(0, RG_TBLK, step, tuple(st_ref[j] for j in range(RG_LANES)), unroll=4)
    for j in range(RG_LANES):
        st_ref[j] = final[j]


def rg_scan(rec_tm, conv_w, conv_b, w_gates, ga_b, gx_b, lam):
    nblk = L_JOINT // RG_TBLK
    rows = RG_TBLK * BATCH
    halo_prev = (RG_CONV // 2) * BATCH
    per_prev = rows // halo_prev
    fixed2 = lambda z, i: (0, 0)
    per_dir = lambda z, i: (z, 0, 0)
    return pl.pallas_call(
        _rg_scan_body,
        out_shape=jax.ShapeDtypeStruct((2, RG_LANES, L_JOINT * BATCH, 128), F32),
        grid=(2, nblk),
        in_specs=[pl.BlockSpec((RG_LANES, halo_prev, 128),
                               lambda z, i: (0, jnp.maximum(_rg_block(z, i) * per_prev - 1, 0), 0)),
                  pl.BlockSpec((RG_LANES, rows, 128), lambda z, i: (0, _rg_block(z, i), 0)),
                  pl.BlockSpec((RG_LANES, BATCH, 128),
                               lambda z, i: (0, jnp.minimum((_rg_block(z, i) + 1) * RG_TBLK, L_JOINT - 1), 0)),
                  pl.BlockSpec((RG_CONV, D_RNN), fixed2),
                  pl.BlockSpec((1, D_RNN), fixed2),
                  pl.BlockSpec((None, D_RNN, 2 * D_RNN), per_dir),
                  pl.BlockSpec((None, 1, D_RNN), per_dir),
                  pl.BlockSpec((None, 1, D_RNN), per_dir),
                  pl.BlockSpec((None, 1, D_RNN), per_dir)],
        out_specs=pl.BlockSpec((None, RG_LANES, rows, 128), lambda z, i: (z, 0, _rg_block(z, i), 0)),
        scratch_shapes=[pltpu.VMEM((rows, D_RNN), F32), pltpu.VMEM((rows, D_RNN), F32),
                        pltpu.VMEM((RG_LANES, BATCH, 128), F32)],
        compiler_params=_params("arbitrary", "arbitrary"),
        name="rg_scan",
    )(rec_tm, rec_tm, rec_tm, conv_w, conv_b, w_gates, ga_b, gx_b, lam)


def _rg_out_body(gate_ref, hf_ref, hb_ref, w_ref, h_ref, mod_ref, lng_ref, lnb_ref, rwh_ref, rwl_ref,
                 ho_ref, f_ref, lg_ref):
    tt = h_ref.shape[1]
    parts = []
    for b in range(BATCH):
        hsum = jnp.concatenate([hf_ref[j, pl.ds(b, tt, stride=BATCH), :] + hb_ref[j, pl.ds(b, tt, stride=BATCH), :]
                                for j in range(RG_LANES)], axis=1)
        parts.append((gate_ref[b].astype(F32) * hsum).astype(BF16))
    zz = jnp.concatenate(parts, axis=0)
    y = jnp.dot(zz, w_ref[...], preferred_element_type=F32)
    h = h_ref[...].reshape(BATCH * tt, D_MODEL)
    rep = lambda m: jnp.broadcast_to(mod_ref[m][:, None, :], (BATCH, tt, D_MODEL)).reshape(BATCH * tt, D_MODEL)
    hn, fpk, lg = _post_epilogue(y, h, rep(MOD_GT), rep(MOD_SC2), rep(MOD_SH2), lng_ref[...], lnb_ref[...],
                                 rwh_ref[...], rwl_ref[...])
    ho_ref[...] = hn.reshape(BATCH, tt, D_MODEL)
    f_ref[...] = fpk.reshape(BATCH, tt, D_MODEL // 2)
    lg_ref[...] = lg.reshape(BATCH, tt, ROUTER_PAD)


def rg_out(gate, h_tm, w_out, h3, modtab_t, ln_g, ln_b, rw_hi, rw_lo):
    nblk = L_JOINT // RG_TBLK
    rows = RG_TBLK * BATCH
    blk3 = lambda i: (0, i, 0)
    fixed = lambda i: (0, 0)
    return pl.pallas_call(
        _rg_out_body,
        out_shape=(jax.ShapeDtypeStruct((BATCH, L_JOINT, D_MODEL), F32),
                   jax.ShapeDtypeStruct((BATCH, L_JOINT, D_MODEL // 2), jnp.uint32),
                   jax.ShapeDtypeStruct((BATCH, L_JOINT, ROUTER_PAD), F32)),
        grid=(nblk,),
        in_specs=[pl.BlockSpec((BATCH, RG_TBLK, D_RNN), blk3),
                  pl.BlockSpec((None, RG_LANES, rows, 128), lambda i: (0, 0, i, 0)),
                  pl.BlockSpec((None, RG_LANES, rows, 128), lambda i: (1, 0, i, 0)),
                  pl.BlockSpec((D_RNN, D_MODEL), fixed),
                  pl.BlockSpec((BATCH, RG_TBLK, D_MODEL), blk3),
                  pl.BlockSpec((None, 6, BATCH, D_MODEL), lambda i: (i // (SEQ // RG_TBLK), 0, 0, 0)),
                  pl.BlockSpec((1, D_MODEL), fixed), pl.BlockSpec((1, D_MODEL), fixed),
                  pl.BlockSpec((D_MODEL, ROUTER_PAD), fixed), pl.BlockSpec((D_MODEL, ROUTER_PAD), fixed)],
        out_specs=(pl.BlockSpec((BATCH, RG_TBLK, D_MODEL), blk3),
                   pl.BlockSpec((BATCH, RG_TBLK, D_MODEL // 2), blk3),
                   pl.BlockSpec((BATCH, RG_TBLK, ROUTER_PAD), blk3)),
        compiler_params=_params("parallel"),
        name="rg_out",
    )(gate, h_tm, h_tm, w_out, h3, modtab_t, ln_g, ln_b, rw_hi, rw_lo)


def _split_dot(tri, x, lhs_tri):
    hi = x.astype(BF16)
    lo = (x - hi.astype(F32)).astype(BF16)
    if lhs_tri:
        return jnp.dot(tri, hi, preferred_element_type=F32) + jnp.dot(tri, lo, preferred_element_type=F32)
    return jnp.dot(hi, tri, preferred_element_type=F32) + jnp.dot(lo, tri, preferred_element_type=F32)


def _mlstm_body(q_ref, k_ref, v_ref, gc_ref, gr_ref, ng_ref, o_ref, ct_ref, n_ref, m_ref):
    row = lax.broadcasted_iota(jnp.int32, (CHUNK, CHUNK), 0)
    col = lax.broadcasted_iota(jnp.int32, (CHUNK, CHUNK), 1)
    lower = col <= row
    upper = col >= row
    tri_lower = jnp.where(lower, 1.0, 0.0).astype(BF16)
    tri_upper = jnp.where(upper, 1.0, 0.0).astype(BF16)
    ng = ng_ref[...]

    def chunk(c, z, reverse):
        c0 = pl.multiple_of(c * CHUNK, CHUNK)
        q = q_ref[pl.ds(c0, CHUNK), :]
        k = k_ref[pl.ds(c0, CHUNK), :]
        v = v_ref[pl.ds(c0, CHUNK), :]
        gc = gc_ref[pl.ds(c0, CHUNK), :]
        gr = gr_ref[:, pl.ds(c0, CHUNK)]
        ig_c = gc[:, 2 * z:2 * z + 1]
        lf_c = gc[:, 2 * z + 1:2 * z + 2]
        ig_r = gr[2 * z:2 * z + 1, :]
        if reverse:
            bc_c = _split_dot(tri_upper, gc, True)[:, 2 * z + 1:2 * z + 2]
            bc_r = _split_dot(tri_lower, gr, False)[2 * z + 1:2 * z + 2, :]
            mask = upper
        else:
            bc_c = _split_dot(tri_lower, gc, True)[:, 2 * z + 1:2 * z + 2]
            bc_r = _split_dot(tri_upper, gr, False)[2 * z + 1:2 * z + 2, :]
            mask = lower
        btot = jnp.sum(lf_c, axis=0, keepdims=True)
        m_prev = m_ref[...]
        n_prev = n_ref[...]
        ct_prev = ct_ref[...]
        dlog = jnp.where(mask, bc_c - bc_r + ig_r, -jnp.inf)
        m_inter = bc_c + m_prev
        m_comb = jnp.maximum(m_inter, jnp.max(dlog, axis=1, keepdims=True))
        qk = lax.dot_general(q, k, (((1,), (1,)), ((), ())), preferred_element_type=F32)
        s = qk * jnp.exp(dlog - m_comb)
        inter = jnp.exp(m_inter - m_comb)
        num = (jnp.dot(s.astype(BF16), v, preferred_element_type=F32)
               + inter * jnp.dot(q, ct_prev.astype(BF16), preferred_element_type=F32))
        den = (jnp.sum(s, axis=1, keepdims=True)
               + inter * jnp.sum(q.astype(F32) * n_prev, axis=1, keepdims=True))
        h = num / jnp.maximum(jnp.abs(den), jnp.exp(-m_comb))
        wlog = btot - bc_c + ig_c
        mloc = jnp.max(wlog, axis=0, keepdims=True)
        wgt = jnp.exp(wlog - mloc)
        m_new = jnp.maximum(btot + m_prev, mloc)
        sp = jnp.exp(btot + m_prev - m_new)
        sl = jnp.exp(mloc - m_new)
        kf = k.astype(F32)
        vw = (v.astype(F32) * wgt).astype(BF16)
        ct_loc = jnp.dot(kf.T.astype(BF16), vw, preferred_element_type=F32)
        ct_ref[...] = sp * ct_prev + sl * ct_loc
        n_ref[...] = sp * n_prev + sl * jnp.sum(kf * wgt, axis=0, keepdims=True)
        m_ref[...] = m_new
        return c0, h

    def reset():
        ct_ref[...] = jnp.zeros_like(ct_ref)
        n_ref[...] = jnp.zeros_like(n_ref)
        m_ref[...] = jnp.zeros_like(m_ref)

    n_chunks = o_ref.shape[0] // CHUNK
    n_lat = SEQ // CHUNK

    reset()

    def fwd(i, carry):
        c0, h = chunk((i + n_lat) % n_chunks, 0, False)
        o_ref[pl.ds(c0, CHUNK), :] = h
        return carry

    lax.fori_loop(0, n_chunks, fwd, 0)

    reset()

    def bwd(i, carry):
        c0, h = chunk(n_chunks - 1 - i, 1, True)
        tot = o_ref[pl.ds(c0, CHUNK), :] + h
        mu = jnp.mean(tot, axis=1, keepdims=True)
        var = jnp.mean(jnp.square(tot - mu), axis=1, keepdims=True)
        o_ref[pl.ds(c0, CHUNK), :] = (tot - mu) * lax.rsqrt(var + LN_EPS) * ng
        return carry

    lax.fori_loop(0, n_chunks, bwd, 0)


def mlstm_cell(q, k, v, g_col, g_row, norm_g):
    b, lj, _ = q.shape
    return pl.pallas_call(
        _mlstm_body,
        out_shape=jax.ShapeDtypeStruct((b, lj, M_HEADS * M_DV), F32),
        grid=(b, M_HEADS),
        in_specs=[pl.BlockSpec((None, lj, M_DK), lambda i, h: (i, 0, h)),
                  pl.BlockSpec((None, lj, M_DK), lambda i, h: (i, 0, h)),
                  pl.BlockSpec((None, lj, M_DV), lambda i, h: (i, 0, h)),
                  pl.BlockSpec((None, None, lj, 4), lambda i, h: (i, h, 0, 0)),
                  pl.BlockSpec((None, None, 4, lj), lambda i, h: (i, h, 0, 0)),
                  pl.BlockSpec((1, M_DV), lambda i, h: (0, h))],
        out_specs=pl.BlockSpec((None, lj, M_DV), lambda i, h: (i, 0, h)),
        scratch_shapes=[pltpu.VMEM((M_DK, M_DV), F32), pltpu.VMEM((1, M_DK), F32), pltpu.VMEM((1, 1), F32)],
        compiler_params=_params("parallel", "parallel"),
        name="mlstm_cell",
    )(q, k, v, g_col, g_row, norm_g)


def _moe_body(te_ref, tf_ref, nu_ref, x_ref, wg_ref, wu_ref, wd_ref, o_ref, wg_s, wu_s, wd_s):
    i = pl.program_id(0)

    @pl.when(i < nu_ref[0])
    def _():
        @pl.when(tf_ref[i] == 1)
        def _():
            wg_s[...] = wg_ref[...].astype(BF16)
            wu_s[...] = wu_ref[...].astype(BF16)
            wd_s[...] = wd_ref[...].astype(BF16)

        x = _unpack_bf16_pairs(x_ref[...])
        a = jnp.dot(x, wg_s[...], preferred_element_type=F32)
        u = jnp.dot(x, wu_s[...], preferred_element_type=F32)
        hid = (a * jax.nn.sigmoid(a)) * u
        o_ref[...] = jnp.dot(hid.astype(BF16), wd_s[...], preferred_element_type=F32)

    @pl.when(i >= nu_ref[0])
    def _():
        o_ref[...] = jnp.zeros_like(o_ref)


def moe_experts(layer, tile_expert, tile_first, n_used, xs, w_gate, w_up, w_down):
    n_rows = xs.shape[0]
    d = 2 * xs.shape[1]
    n_tiles = n_rows // MOE_TILE
    wmap_in = lambda i, te, tf, nu: (layer, te[i], 0, 0)
    grid_spec = pltpu.PrefetchScalarGridSpec(
        num_scalar_prefetch=3,
        grid=(n_tiles,),
        in_specs=[pl.BlockSpec((MOE_TILE, d // 2), lambda i, te, tf, nu: (i, 0)),
                  pl.BlockSpec((None, None, d, D_EXPERT), wmap_in),
                  pl.BlockSpec((None, None, d, D_EXPERT), wmap_in),
                  pl.BlockSpec((None, None, D_EXPERT, d), wmap_in)],
        out_specs=pl.BlockSpec((MOE_TILE, d), lambda i, te, tf, nu: (i, 0)),
        scratch_shapes=[pltpu.VMEM((d, D_EXPERT), BF16), pltpu.VMEM((d, D_EXPERT), BF16),
                        pltpu.VMEM((D_EXPERT, d), BF16)],
    )
    return pl.pallas_call(
        _moe_body,
        out_shape=jax.ShapeDtypeStruct((n_rows, d), F32),
        grid_spec=grid_spec,
        compiler_params=_params("arbitrary"),
        name="moe_experts",
    )(tile_expert, tile_first, n_used, xs, w_gate, w_up, w_down)


def moe(layer, f, logits, router_b, w_gate, w_up, w_down):
    t = f.shape[0]
    scores = jax.nn.sigmoid(logits[:, :N_EXPERTS])
    sel = (scores + router_b).reshape(t, N_GROUPS, EXP_PER_GROUP)
    lane = jnp.arange(EXP_PER_GROUP, dtype=jnp.int32)
    i1 = jnp.argmax(sel, -1).astype(jnp.int32)
    m1 = jnp.max(sel, -1)
    rest = jnp.where(lane == i1[..., None], -jnp.inf, sel)
    i2 = jnp.argmax(rest, -1).astype(jnp.int32)
    m2 = jnp.max(rest, -1)
    g_best = jnp.argmax(m1 + m2, -1).astype(jnp.int32)
    in_best = jnp.arange(N_GROUPS, dtype=jnp.int32) == g_best[:, None]
    pick = lambda a: jnp.sum(jnp.where(in_best, a, 0), axis=1)
    idx = g_best[:, None] * EXP_PER_GROUP + jnp.stack([pick(i1), pick(i2)], axis=-1)
    chosen = jnp.arange(N_EXPERTS, dtype=jnp.int32) == idx[..., None]
    wts = jnp.sum(jnp.where(chosen, scores[:, None, :], 0.0), axis=-1)
    wts = wts / jnp.sum(wts, -1, keepdims=True)

    n_pick = t * TOP_K
    n_tiles = n_pick // MOE_TILE + N_EXPERTS
    n_rows = n_tiles * MOE_TILE
    flat_e = idx.reshape(-1).astype(jnp.int32)
    onehot = (flat_e[:, None] == jnp.arange(N_EXPERTS, dtype=jnp.int32)[None, :]).astype(jnp.int32)
    csum = jnp.cumsum(onehot, axis=0)
    counts = csum[-1]
    rank = jnp.sum(csum * onehot, axis=1) - 1
    tiles_per = (counts + MOE_TILE - 1) // MOE_TILE
    tile_end = jnp.cumsum(tiles_per)
    tile_start = tile_end - tiles_per
    dest = tile_start[flat_e] * MOE_TILE + rank
    flat_t = jnp.arange(n_pick, dtype=jnp.int32) // TOP_K
    row_token = jnp.zeros((n_rows,), jnp.int32).at[dest].set(flat_t)
    n_used = tile_end[-1]
    tile_ids = jnp.arange(n_tiles, dtype=jnp.int32)
    tile_expert = jnp.searchsorted(tile_end, jnp.minimum(tile_ids, n_used - 1), side="right").astype(jnp.int32)
    tile_expert = jnp.minimum(tile_expert, N_EXPERTS - 1)
    tile_first = ((tile_ids == tile_start[tile_expert]) & (tile_ids < n_used)).astype(jnp.int32)

    xs = jnp.take(f, row_token, axis=0)
    ys = moe_experts(layer, tile_expert, tile_first, n_used.reshape(1).astype(jnp.int32), xs, w_gate, w_up, w_down)
    pos = dest.reshape(t, TOP_K)
    return jnp.take(ys, pos[:, 0], axis=0), jnp.take(ys, pos[:, 1], axis=0), wts


def sincos_2d(n_tok):
    rows = n_tok // GRID_W
    quarter = D_MODEL // 4
    omega = 1.0 / (10000.0 ** (jnp.arange(quarter, dtype=F32) / quarter))
    ar = jnp.arange(rows, dtype=F32)[:, None] * omega
    ac = jnp.arange(GRID_W, dtype=F32)[:, None] * omega
    er = jnp.concatenate([jnp.sin(ar), jnp.cos(ar)], -1)
    ec = jnp.concatenate([jnp.sin(ac), jnp.cos(ac)], -1)
    half = D_MODEL // 2
    pos = jnp.concatenate([jnp.broadcast_to(er[:, None], (rows, GRID_W, half)),
                           jnp.broadcast_to(ec[None], (rows, GRID_W, half))], -1)
    return pos.reshape(rows * GRID_W, D_MODEL)


def _dwconv_seg(x, w, b, axis):
    kk = w.shape[0]
    left = kk // 2
    n = x.shape[axis]
    pad = [(0, 0)] * x.ndim
    pad[axis] = (left, kk - 1 - left)
    xp = jnp.pad(x, pad)
    y = b
    for j in range(kk):
        y = y + lax.slice_in_dim(xp, j, j + n, axis=axis) * w[j]
    return y


def dwconv_joint(x, w, b, axis):
    xl = lax.slice_in_dim(x, 0, SEQ, axis=axis)
    xc = lax.slice_in_dim(x, SEQ, L_JOINT, axis=axis)
    return jnp.concatenate([_dwconv_seg(xl, w, b, axis), _dwconv_seg(xc, w, b, axis)], axis=axis)


def rglru_layer(h3, modtab_t, w_in, conv_w, conv_b, ga_w, ga_b, gx_w, gx_b, lam, w_out, ln_g, ln_b, rw_hi, rw_lo):
    def block_diag(w):
        eye = jnp.eye(RG_BLOCKS, dtype=w.dtype)
        return jnp.einsum("nkj,nm->nkmj", w, eye).reshape(D_RNN, D_RNN)

    w_gates = jnp.stack([jnp.concatenate([block_diag(ga_w[z]), block_diag(gx_w[z])], axis=1)
                         for z in range(2)]).astype(BF16)
    gate, rec_tm = rg_in(h3, modtab_t, w_in.astype(BF16))
    h_tm = rg_scan(rec_tm, conv_w, conv_b[None], w_gates, ga_b[:, None], gx_b[:, None], lam[:, None])
    hn, fpk, lg = rg_out(gate, h_tm, w_out.astype(BF16), h3, modtab_t, ln_g, ln_b, rw_hi, rw_lo)
    rows = BATCH * L_JOINT
    return hn.reshape(rows, D_MODEL), fpk.reshape(rows, D_MODEL // 2), lg.reshape(rows, ROUTER_PAD)


def mlstm_mixer(inp, w_up, conv_w, conv_b, w_q, w_k, w_v, w_o, w_if, b_if, norm_g, skip):
    b = inp.shape[0]
    rows = b * L_JOINT
    xm = mm(inp.reshape(rows, D_MODEL), w_up.astype(BF16))
    xc = jax.nn.silu(dwconv_joint(xm.reshape(b, L_JOINT, D_M), conv_w, conv_b, 1)).reshape(rows, D_M)
    qk = mm(xc, jnp.concatenate([w_q, w_k], axis=1).astype(BF16), tn=M_HEADS * M_DK)
    q = qk[:, :M_HEADS * M_DK].astype(BF16).reshape(b, L_JOINT, M_HEADS * M_DK)
    k = (qk[:, M_HEADS * M_DK:] * (M_DK ** -0.5)).astype(BF16).reshape(b, L_JOINT, M_HEADS * M_DK)
    n_gate = 4 * M_HEADS
    w_g = jnp.concatenate([w_if[0], w_if[1], jnp.zeros((D_M, 128 - n_gate), F32)], axis=1)
    vog = mm(xm, jnp.concatenate([w_v, w_o, w_g], axis=1).astype(BF16), tn=(2 * D_M + 128) // 3)
    v = vog[:, :D_M].astype(BF16).reshape(b, L_JOINT, D_M)
    o = jax.nn.sigmoid(vog[:, D_M:2 * D_M])
    g = vog[:, 2 * D_M:2 * D_M + n_gate].reshape(b, L_JOINT, 2, 2 * M_HEADS) + b_if
    ig = g[..., :M_HEADS]
    lf = jax.nn.log_sigmoid(g[..., M_HEADS:])
    g4 = jnp.stack([ig[:, :, 0], lf[:, :, 0], ig[:, :, 1], lf[:, :, 1]], axis=-1)
    g_col = jnp.transpose(g4, (0, 2, 1, 3))
    g_row = jnp.transpose(g4, (0, 2, 3, 1))
    hn = mlstm_cell(q, k, v, g_col, g_row, norm_g.reshape(1, D_M)).reshape(rows, D_M)
    return o * hn + skip * xc


def _dft_mats(n_time):
    n = 3 * n_time // 2
    half = n // 2
    kk = np.arange(half, dtype=np.int64)[:, None]
    tt = np.arange(n_time, dtype=np.int64)[None, :]
    ang = 2.0 * np.pi * ((kk * tt) % n).astype(np.float64) / n
    top = np.cos(ang)
    bot = -np.sin(ang)
    bot[0] = np.cos(np.pi * tt[0])
    fwd = np.concatenate([top, bot], axis=0)
    mm_ = (np.arange(n_time, dtype=np.int64) + n_time // 2)[:, None]
    ang2 = 2.0 * np.pi * ((mm_ * kk.T) % n).astype(np.float64) / n
    wk = np.full((1, half), 2.0)
    wk[0, 0] = 1.0
    itop = wk * np.cos(ang2) / n
    ibot = -2.0 * np.sin(ang2) / n
    ibot[:, 0] = np.cos(np.pi * mm_[:, 0]) / n
    inv = np.concatenate([itop, ibot], axis=1)
    return jnp.asarray(fwd, dtype=BF16), jnp.asarray(inv, dtype=BF16)


def hyena_filters(n_time, w1, b1, fq1, w2, b2, fq2, w3):
    hp = lax.Precision.HIGHEST
    t01 = jnp.linspace(0.0, 1.0, n_time, dtype=F32)
    bands = jnp.linspace(1e-4, H_BANDS - 1, H_BANDS, dtype=F32)
    ang = (2.0 * math.pi / n_time) * jnp.arange(n_time, dtype=F32)[:, None] * bands[None, :]
    z = jnp.concatenate([t01[:, None], jnp.cos(ang), -jnp.sin(ang)], -1)
    hdn = jnp.sin(fq1 * (jnp.dot(z, w1, precision=hp) + b1))
    hdn = jnp.sin(fq2 * (jnp.dot(hdn, w2, precision=hp) + b2))
    filt = jnp.dot(hdn, w3, precision=hp).reshape(n_time, 2, D_MODEL)
    dist = jnp.abs(jnp.arange(n_time) - n_time // 2).astype(F32) * (2.0 / n_time)
    d_max = math.log(H_DECAY_TARGET) / H_FAST
    d_min = math.log(H_DECAY_TARGET) / H_SLOW
    deltas = jnp.abs(jnp.linspace(d_min, d_max, D_MODEL, dtype=F32))
    window = jnp.exp(-dist[:, None] * deltas[None, :])
    return filt * window[:, None, :]


def _spec_mul(x, hf):
    half = hf.shape[0] // 2
    xt, xb = x[:, :half], x[:, half:]
    ht, hb = hf[:half], hf[half:]
    yt = xt * ht - xb * hb
    yb = xt * hb + xb * ht
    first = (jnp.arange(half) == 0)[None, :, None]
    yt = jnp.where(first, xt * ht, yt)
    yb = jnp.where(first, xb * hb, yb)
    return jnp.concatenate([yt, yb], axis=1)


def hyena_seq(h, w_in_bf, b_in, conv_w, conv_b, fparams, skip):
    b, n_time, _ = h.shape
    u = mm(h.reshape(b * n_time, D_MODEL), w_in_bf, tn=D_MODEL).reshape(b, n_time, 3 * D_MODEL) + b_in
    u = _dwconv_seg(u, conv_w, conv_b, 1)
    v, g1, g2 = jnp.split(u, 3, axis=-1)
    filt = hyena_filters(n_time, *fparams)
    fwd, inv = _dft_mats(n_time)
    hf = bmm_left(fwd, filt.reshape(1, n_time, 2 * D_MODEL))[0]

    def conv(sig, j):
        spec = bmm_left(fwd, sig)
        prod = _spec_mul(spec, hf[:, j * D_MODEL:(j + 1) * D_MODEL])
        return bmm_left(inv, prod) + sig * skip[j]

    z = g1 * conv(v, 0)
    return g2 * conv(z, 1)


def kernel(x, c, ctx, c_ctx, router_w, router_b, ada_w, ada_b, ln_g, ln_b, moe_w_gate, moe_w_up, moe_w_down, rg_w_in, rg_conv_w, rg_conv_b, rg_gate_a_w, rg_gate_a_b, rg_gate_x_w, rg_gate_x_b, rg_lambda, rg_w_out, ml_w_up, ml_conv_w, ml_conv_b, ml_w_q, ml_w_k, ml_w_v, ml_w_o, ml_w_if, ml_b_if, ml_norm_g, ml_skip, ml_w_down, hy_w_in, hy_b_in, hy_conv_w, hy_conv_b, hy_f_w1, hy_f_b1, hy_f_freq1, hy_f_w2, hy_f_b2, hy_f_freq2, hy_f_w3, hy_skip, hy_w_out):
    bsz = x.shape[0]
    rows = bsz * L_JOINT
    hx = x + sincos_2d(SEQ)[None]
    hj = jnp.concatenate([hx, ctx], axis=1).reshape(rows, D_MODEL)
    is_ctx = (jnp.arange(L_JOINT) >= SEQ)[None, :, None]

    cond = jnp.concatenate([jax.nn.silu(c), jax.nn.silu(c_ctx)[None],
                            jnp.zeros((16 - bsz - 1, D_MODEL), F32)], axis=0)
    rw_pad = jnp.concatenate([router_w, jnp.zeros((D_MODEL, ROUTER_PAD - N_EXPERTS), F32)], axis=1)
    rw_hi = rw_pad.astype(BF16)
    rw_lo = (rw_pad - rw_hi.astype(F32)).astype(BF16)

    for i in range(DEPTH):
        kind, j = i % N_MIXERS, i // N_MIXERS
        mod = (mm(cond, ada_w[i].astype(BF16), tm=16, tn=D_MODEL) + ada_b[i]).reshape(16, 6, D_MODEL)
        mod_x = mod[:bsz]
        mod_c = jnp.broadcast_to(mod[bsz][None], (bsz, 6, D_MODEL))
        modtab = jnp.stack([mod_x, mod_c], axis=1).reshape(2 * bsz, 6, D_MODEL)
        lng, lnb = ln_g[i, 0][None], ln_b[i, 0][None]
        if kind == 0:
            modtab_t = jnp.stack([jnp.transpose(mod_x, (1, 0, 2)), jnp.transpose(mod_c, (1, 0, 2))])
            hj, f, logits = rglru_layer(hj.reshape(bsz, L_JOINT, D_MODEL), modtab_t, rg_w_in[j], rg_conv_w[j],
                                        rg_conv_b[j], rg_gate_a_w[j], rg_gate_a_b[j], rg_gate_x_w[j],
                                        rg_gate_x_b[j], rg_lambda[j], rg_w_out[j], lng, lnb, rw_hi, rw_lo)
        else:
            sh = jnp.where(is_ctx, mod_c[:, None, MOD_SH], mod_x[:, None, MOD_SH])
            sc = jnp.where(is_ctx, mod_c[:, None, MOD_SC], mod_x[:, None, MOD_SC])
            inp = hj.reshape(bsz, L_JOINT, D_MODEL) * (1.0 + sc) + sh
            if kind == 1:
                z = mlstm_mixer(inp, ml_w_up[j], ml_conv_w[j], ml_conv_b[j], ml_w_q[j], ml_w_k[j], ml_w_v[j],
                                ml_w_o[j], ml_w_if[j], ml_b_if[j], ml_norm_g[j], ml_skip[j])
                w_out = ml_w_down[j]
            else:
                fparams = (hy_f_w1[j], hy_f_b1[j], hy_f_freq1[j], hy_f_w2[j], hy_f_b2[j], hy_f_freq2[j],
                           hy_f_w3[j])
                args = (hy_w_in[j].astype(BF16), hy_b_in[j], hy_conv_w[j], hy_conv_b[j], fparams, hy_skip[j])
                z = jnp.concatenate([hyena_seq(inp[:, :SEQ], *args), hyena_seq(inp[:, SEQ:], *args)],
                                    axis=1).reshape(rows, D_MODEL)
                w_out = hy_w_out[j]
            hj, f, logits = post_mixer(z, w_out.astype(BF16), hj, modtab, lng, lnb, rw_hi, rw_lo)
        ya, yb, wts = moe(i, f, logits, router_b, moe_w_gate, moe_w_up, moe_w_down)
        hj = moe_combine(hj, ya, yb, wts, modtab, ln_g[i, 1][None], ln_b[i, 1][None])
    return hj.reshape(bsz, L_JOINT, D_MODEL)[:, :SEQ]
```

```python
import functools
import math

import numpy as np
import jax
import jax.numpy as jnp
from jax import lax
from jax.experimental import pallas as pl
from jax.experimental.pallas import tpu as pltpu

F32 = jnp.float32
BF16 = jnp.bfloat16

D_MODEL = 1024
BATCH = 8
SEQ = 2048
DEPTH = 4
GRID_W = 64
CTX_LEN = 256
L_JOINT = SEQ + CTX_LEN
N_MIXERS = 3
ALPHA = (2.0 * DEPTH) ** 0.25
LN_EPS = 1e-6

D_RNN = 1408
RG_BLOCKS = 16
RG_BS = D_RNN // RG_BLOCKS
RG_C = 8.0

D_M = 2 * D_MODEL
M_HEADS = 8
M_DK = 128
M_DV = D_M // M_HEADS
CHUNK = 128

H_EMB = 33
H_BANDS = (H_EMB - 1) // 2
H_DECAY_TARGET = 1e-2
H_FAST = 0.3
H_SLOW = 1.5

N_EXPERTS = 16
N_GROUPS = 4
EXP_PER_GROUP = N_EXPERTS // N_GROUPS
TOP_K = 2
D_EXPERT = 512
ROUTER_PAD = 128

VMEM_LIMIT_BYTES = 56 * 1024 * 1024
ROW_TILE = 256
MOE_TILE = 256
RG_TBLK = 32
RG_LANES = D_RNN // 128
RG_CONV = 4
CONV_HALO = 16
ML_HEADS_PER_STEP = 2
HY_TN = 256
TILES_PER_BATCH = L_JOINT // ROW_TILE
LATENT_TILES = SEQ // ROW_TILE

MOD_SH, MOD_SC, MOD_GT, MOD_SH2, MOD_SC2, MOD_GT2 = range(6)


def _params(*sem):
    return pltpu.CompilerParams(dimension_semantics=sem, vmem_limit_bytes=VMEM_LIMIT_BYTES)


def _mod_index(i):
    return (2 * (i // TILES_PER_BATCH) + (i % TILES_PER_BATCH) // LATENT_TILES, 0, 0)


def _layer_norm_rows(r, g, b):
    mu = jnp.mean(r, axis=-1, keepdims=True)
    var = jnp.mean(jnp.square(r - mu), axis=-1, keepdims=True)
    return (r - mu) * lax.rsqrt(var + LN_EPS) * g + b


def _mm_body(x_ref, w_ref, o_ref):
    o_ref[...] = jnp.dot(x_ref[...].astype(BF16), w_ref[...], preferred_element_type=F32).astype(o_ref.dtype)


def mm(x, w, *, tm=512, tn=None, out_dtype=F32):
    m, k = x.shape
    n = w.shape[1]
    tn = n if tn is None else tn
    assert m % tm == 0 and n % tn == 0
    return pl.pallas_call(
        _mm_body,
        out_shape=jax.ShapeDtypeStruct((m, n), out_dtype),
        grid=(n // tn, m // tm),
        in_specs=[pl.BlockSpec((tm, k), lambda j, i: (i, 0)),
                  pl.BlockSpec((k, tn), lambda j, i: (0, j))],
        out_specs=pl.BlockSpec((tm, tn), lambda j, i: (i, j)),
        compiler_params=_params("parallel", "parallel"),
        name="mm",
    )(x, w)


def _bmm_body(a_ref, x_ref, o_ref):
    o_ref[...] = jnp.dot(a_ref[...], x_ref[...].astype(BF16), preferred_element_type=F32).astype(o_ref.dtype)


def bmm_left(a, x, *, tn=256, out_dtype=F32):
    mo, k = a.shape
    b, _, d = x.shape
    assert d % tn == 0
    return pl.pallas_call(
        _bmm_body,
        out_shape=jax.ShapeDtypeStruct((b, mo, d), out_dtype),
        grid=(b, d // tn),
        in_specs=[pl.BlockSpec((mo, k), lambda i, j: (0, 0)),
                  pl.BlockSpec((None, k, tn), lambda i, j: (i, 0, j))],
        out_specs=pl.BlockSpec((None, mo, tn), lambda i, j: (i, 0, j)),
        compiler_params=_params("parallel", "parallel"),
        name="dft_mm",
    )(a, x)


def _pack_bf16_pairs(f):
    half = f.shape[1] // 2
    hi = lax.bitcast_convert_type(f[:, :half].astype(BF16).astype(F32), jnp.uint32)
    lo = lax.bitcast_convert_type(f[:, half:].astype(BF16).astype(F32), jnp.uint32)
    return hi | (lo >> 16)


def _unpack_bf16_pairs(u):
    hi = lax.bitcast_convert_type(u & jnp.uint32(0xFFFF0000), F32).astype(BF16)
    lo = lax.bitcast_convert_type(u << 16, F32).astype(BF16)
    return jnp.concatenate([hi, lo], axis=1)


def _post_epilogue(y, h, gt, sc2, sh2, ln_g, ln_b, rw_hi, rw_lo):
    hn = _layer_norm_rows(ALPHA * h + gt * y, ln_g, ln_b)
    f = hn * (1.0 + sc2) + sh2
    f_hi = f.astype(BF16)
    f_lo = (f - f_hi.astype(F32)).astype(BF16)
    lg = (jnp.dot(f_hi, rw_hi, preferred_element_type=F32) + jnp.dot(f_lo, rw_hi, preferred_element_type=F32)
          + jnp.dot(f_hi, rw_lo, preferred_element_type=F32))
    return hn, _pack_bf16_pairs(f), lg


def _post_body(z_ref, w_ref, h_ref, mod_ref, lng_ref, lnb_ref, rwh_ref, rwl_ref, ho_ref, f_ref, lg_ref):
    y = jnp.dot(z_ref[...].astype(BF16), w_ref[...], preferred_element_type=F32)
    hn, fpk, lg = _post_epilogue(y, h_ref[...], mod_ref[MOD_GT:MOD_GT + 1, :], mod_ref[MOD_SC2:MOD_SC2 + 1, :],
                                 mod_ref[MOD_SH2:MOD_SH2 + 1, :], lng_ref[...], lnb_ref[...], rwh_ref[...],
                                 rwl_ref[...])
    ho_ref[...] = hn
    f_ref[...] = fpk
    lg_ref[...] = lg


def post_mixer(z, w_out, h, modtab, ln_g, ln_b, rw_hi, rw_lo):
    t, k = z.shape
    d = w_out.shape[1]
    row = lambda i: (i, 0)
    fixed = lambda i: (0, 0)
    return pl.pallas_call(
        _post_body,
        out_shape=(jax.ShapeDtypeStruct((t, d), F32), jax.ShapeDtypeStruct((t, d // 2), jnp.uint32),
                   jax.ShapeDtypeStruct((t, ROUTER_PAD), F32)),
        grid=(t // ROW_TILE,),
        in_specs=[pl.BlockSpec((ROW_TILE, k), row),
                  pl.BlockSpec((k, d), fixed),
                  pl.BlockSpec((ROW_TILE, d), row),
                  pl.BlockSpec((None, 6, d), _mod_index),
                  pl.BlockSpec((1, d), fixed),
                  pl.BlockSpec((1, d), fixed),
                  pl.BlockSpec((d, ROUTER_PAD), fixed),
                  pl.BlockSpec((d, ROUTER_PAD), fixed)],
        out_specs=(pl.BlockSpec((ROW_TILE, d), row), pl.BlockSpec((ROW_TILE, d // 2), row),
                   pl.BlockSpec((ROW_TILE, ROUTER_PAD), row)),
        compiler_params=_params("parallel"),
        name="post_mixer",
    )(z, w_out, h, modtab, ln_g, ln_b, rw_hi, rw_lo)


def _combine_body(h_ref, ya_ref, yb_ref, w_ref, mod_ref, lng_ref, lnb_ref, o_ref):
    w = w_ref[...]
    y2 = w[:, 0:1] * ya_ref[...] + w[:, 1:2] * yb_ref[...]
    gt2 = mod_ref[MOD_GT2:MOD_GT2 + 1, :]
    o_ref[...] = _layer_norm_rows(ALPHA * h_ref[...] + gt2 * y2, lng_ref[...], lnb_ref[...])


def moe_combine(h, ya, yb, wts, modtab, ln_g, ln_b):
    t, d = h.shape
    row = lambda i: (i, 0)
    fixed = lambda i: (0, 0)
    return pl.pallas_call(
        _combine_body,
        out_shape=jax.ShapeDtypeStruct((t, d), F32),
        grid=(t // ROW_TILE,),
        in_specs=[pl.BlockSpec((ROW_TILE, d), row), pl.BlockSpec((ROW_TILE, d), row),
                  pl.BlockSpec((ROW_TILE, d), row), pl.BlockSpec((ROW_TILE, TOP_K), row),
                  pl.BlockSpec((None, 6, d), _mod_index),
                  pl.BlockSpec((1, d), fixed), pl.BlockSpec((1, d), fixed)],
        out_specs=pl.BlockSpec((ROW_TILE, d), row),
        compiler_params=_params("parallel"),
        name="moe_combine",
    )(h, ya, yb, wts, modtab, ln_g, ln_b)


def _rg_in_body(h_ref, mod_ref, w_ref, gate_ref, rec_ref):
    tt = h_ref.shape[1]
    sc = mod_ref[MOD_SC][:, None, :]
    sh = mod_ref[MOD_SH][:, None, :]
    inp = (h_ref[...] * (1.0 + sc) + sh).reshape(BATCH * tt, D_MODEL)
    p = jnp.dot(inp.astype(BF16), w_ref[...], preferred_element_type=F32)
    gate_ref[...] = jax.nn.gelu(p[:, :D_RNN]).astype(gate_ref.dtype).reshape(BATCH, tt, D_RNN)
    for b in range(BATCH):
        rec_b = p[b * tt:(b + 1) * tt, D_RNN:]
        for j in range(RG_LANES):
            rec_ref[j, pl.ds(b, tt, stride=BATCH), :] = rec_b[:, j * 128:(j + 1) * 128]


def rg_in(h3, modtab_t, w_in):
    nblk = L_JOINT // RG_TBLK
    return pl.pallas_call(
        _rg_in_body,
        out_shape=(jax.ShapeDtypeStruct((BATCH, L_JOINT, D_RNN), BF16),
                   jax.ShapeDtypeStruct((RG_LANES, L_JOINT * BATCH, 128), F32)),
        grid=(nblk,),
        in_specs=[pl.BlockSpec((BATCH, RG_TBLK, D_MODEL), lambda i: (0, i, 0)),
                  pl.BlockSpec((None, 6, BATCH, D_MODEL), lambda i: (i // (SEQ // RG_TBLK), 0, 0, 0)),
                  pl.BlockSpec((D_MODEL, 2 * D_RNN), lambda i: (0, 0))],
        out_specs=(pl.BlockSpec((BATCH, RG_TBLK, D_RNN), lambda i: (0, i, 0)),
                   pl.BlockSpec((RG_LANES, RG_TBLK * BATCH, 128), lambda i: (0, i, 0))),
        compiler_params=_params("parallel"),
        name="rg_in",
    )(h3, modtab_t, w_in)


def _rg_block(z, i):
    nblk = L_JOINT // RG_TBLK
    nlat = SEQ // RG_TBLK
    return jnp.where(z == 0, (i + nlat) % nblk, nblk - 1 - i)


def _rg_scan_body(prev_ref, main_ref, next_ref, cw_ref, cb_ref, wg_ref, ba_ref, bx_ref, lam_ref, ho_ref,
                  a_s, u_s, st_ref):
    z = pl.program_id(0)
    i = pl.program_id(1)
    nblk = L_JOINT // RG_TBLK
    nlat = SEQ // RG_TBLK
    blk = _rg_block(z, i)
    rows = RG_TBLK * BATCH

    @pl.when(i == 0)
    def _():
        st_ref[...] = jnp.zeros_like(st_ref)

    has_prev = jnp.where((blk == 0) | (blk == nlat), 0.0, 1.0)
    has_next = jnp.where((blk == nlat - 1) | (blk == nblk - 1), 0.0, 1.0)
    cols = []
    for j in range(RG_LANES):
        ext = jnp.concatenate([prev_ref[j] * has_prev, main_ref[j], next_ref[j] * has_next], axis=0)
        lane = slice(j * 128, (j + 1) * 128)
        acc = cb_ref[:, lane] + cw_ref[0:1, lane] * ext[0:rows]
        for k in range(1, RG_CONV):
            acc = acc + cw_ref[k:k + 1, lane] * ext[k * BATCH:k * BATCH + rows]
        cols.append(acc)
    xc = jnp.concatenate(cols, axis=1)

    pre = jnp.dot(xc.astype(BF16), wg_ref[...], preferred_element_type=F32)
    r = 0.5 * (jnp.tanh(0.5 * (pre[:, :D_RNN] + ba_ref[...])) + 1.0)
    g = 0.5 * (jnp.tanh(0.5 * (pre[:, D_RNN:] + bx_ref[...])) + 1.0)
    a = jnp.exp((-RG_C * jax.nn.softplus(-lam_ref[...])) * r)
    a_s[...] = a
    u_s[...] = jnp.sqrt(1.0 - a * a) * (g * xc)

    def step(t, carry):
        te = jnp.where(z == 0, t, RG_TBLK - 1 - t)
        r0 = pl.multiple_of(te * BATCH, BATCH)
        out = []
        for j in range(RG_LANES):
            lane = slice(j * 128, (j + 1) * 128)
            h = a_s[pl.ds(r0, BATCH), lane] * carry[j] + u_s[pl.ds(r0, BATCH), lane]
            ho_ref[j, pl.ds(r0, BATCH), :] = h
            out.append(h)
        return tuple(out)

    final = lax.fori_loop(0, RG_TBLK, step, tuple(st_ref[j] for j in range(RG_LANES)), unroll=4)
    for j in range(RG_LANES):
        st_ref[j] = final[j]


def rg_scan(rec_tm, conv_w, conv_b, w_gates, ga_b, gx_b, lam):
    nblk = L_JOINT // RG_TBLK
    rows = RG_TBLK * BATCH
    halo_prev = (RG_CONV // 2) * BATCH
    per_prev = rows // halo_prev
    fixed2 = lambda z, i: (0, 0)
    per_dir = lambda z, i: (z, 0, 0)
    return pl.pallas_call(
        _rg_scan_body,
        out_shape=jax.ShapeDtypeStruct((2, RG_LANES, L_JOINT * BATCH, 128), F32),
        grid=(2, nblk),
        in_specs=[pl.BlockSpec((RG_LANES, halo_prev, 128),
                               lambda z, i: (0, jnp.maximum(_rg_block(z, i) * per_prev - 1, 0), 0)),
                  pl.BlockSpec((RG_LANES, rows, 128), lambda z, i: (0, _rg_block(z, i), 0)),
                  pl.BlockSpec((RG_LANES, BATCH, 128),
                               lambda z, i: (0, jnp.minimum((_rg_block(z, i) + 1) * RG_TBLK, L_JOINT - 1), 0)),
                  pl.BlockSpec((RG_CONV, D_RNN), fixed2),
                  pl.BlockSpec((1, D_RNN), fixed2),
                  pl.BlockSpec((None, D_RNN, 2 * D_RNN), per_dir),
                  pl.BlockSpec((None, 1, D_RNN), per_dir),
                  pl.BlockSpec((None, 1, D_RNN), per_dir),
                  pl.BlockSpec((None, 1, D_RNN), per_dir)],
        out_specs=pl.BlockSpec((None, RG_LANES, rows, 128), lambda z, i: (z, 0, _rg_block(z, i), 0)),
        scratch_shapes=[pltpu.VMEM((rows, D_RNN), F32), pltpu.VMEM((rows, D_RNN), F32),
                        pltpu.VMEM((RG_LANES, BATCH, 128), F32)],
        compiler_params=_params("arbitrary", "arbitrary"),
        name="rg_scan",
    )(rec_tm, rec_tm, rec_tm, conv_w, conv_b, w_gates, ga_b, gx_b, lam)


def _rg_out_body(gate_ref, hf_ref, hb_ref, w_ref, h_ref, mod_ref, lng_ref, lnb_ref, rwh_ref, rwl_ref,
                 ho_ref, f_ref, lg_ref):
    tt = h_ref.shape[1]
    parts = []
    for b in range(BATCH):
        hsum = jnp.concatenate([hf_ref[j, pl.ds(b, tt, stride=BATCH), :] + hb_ref[j, pl.ds(b, tt, stride=BATCH), :]
                                for j in range(RG_LANES)], axis=1)
        parts.append((gate_ref[b].astype(F32) * hsum).astype(BF16))
    zz = jnp.concatenate(parts, axis=0)
    y = jnp.dot(zz, w_ref[...], preferred_element_type=F32)
    h = h_ref[...].reshape(BATCH * tt, D_MODEL)
    rep = lambda m: jnp.broadcast_to(mod_ref[m][:, None, :], (BATCH, tt, D_MODEL)).reshape(BATCH * tt, D_MODEL)
    hn, fpk, lg = _post_epilogue(y, h, rep(MOD_GT), rep(MOD_SC2), rep(MOD_SH2), lng_ref[...], lnb_ref[...],
                                 rwh_ref[...], rwl_ref[...])
    ho_ref[...] = hn.reshape(BATCH, tt, D_MODEL)
    f_ref[...] = fpk.reshape(BATCH, tt, D_MODEL // 2)
    lg_ref[...] = lg.reshape(BATCH, tt, ROUTER_PAD)


def rg_out(gate, h_tm, w_out, h3, modtab_t, ln_g, ln_b, rw_hi, rw_lo):
    nblk = L_JOINT // RG_TBLK
    rows = RG_TBLK * BATCH
    blk3 = lambda i: (0, i, 0)
    fixed = lambda i: (0, 0)
    return pl.pallas_call(
        _rg_out_body,
        out_shape=(jax.ShapeDtypeStruct((BATCH, L_JOINT, D_MODEL), F32),
                   jax.ShapeDtypeStruct((BATCH, L_JOINT, D_MODEL // 2), jnp.uint32),
                   jax.ShapeDtypeStruct((BATCH, L_JOINT, ROUTER_PAD), F32)),
        grid=(nblk,),
        in_specs=[pl.BlockSpec((BATCH, RG_TBLK, D_RNN), blk3),
                  pl.BlockSpec((None, RG_LANES, rows, 128), lambda i: (0, 0, i, 0)),
                  pl.BlockSpec((None, RG_LANES, rows, 128), lambda i: (1, 0, i, 0)),
                  pl.BlockSpec((D_RNN, D_MODEL), fixed),
                  pl.BlockSpec((BATCH, RG_TBLK, D_MODEL), blk3),
                  pl.BlockSpec((None, 6, BATCH, D_MODEL), lambda i: (i // (SEQ // RG_TBLK), 0, 0, 0)),
                  pl.BlockSpec((1, D_MODEL), fixed), pl.BlockSpec((1, D_MODEL), fixed),
                  pl.BlockSpec((D_MODEL, ROUTER_PAD), fixed), pl.BlockSpec((D_MODEL, ROUTER_PAD), fixed)],
        out_specs=(pl.BlockSpec((BATCH, RG_TBLK, D_MODEL), blk3),
                   pl.BlockSpec((BATCH, RG_TBLK, D_MODEL // 2), blk3),
                   pl.BlockSpec((BATCH, RG_TBLK, ROUTER_PAD), blk3)),
        compiler_params=_params("parallel"),
        name="rg_out",
    )(gate, h_tm, h_tm, w_out, h3, modtab_t, ln_g, ln_b, rw_hi, rw_lo)


def _split_dot(tri, x, lhs_tri):
    hi = x.astype(BF16)
    lo = (x - hi.astype(F32)).astype(BF16)
    if lhs_tri:
        return jnp.dot(tri, hi, preferred_element_type=F32) + jnp.dot(tri, lo, preferred_element_type=F32)
    return jnp.dot(hi, tri, preferred_element_type=F32) + jnp.dot(lo, tri, preferred_element_type=F32)


def _mlstm_body(q_ref, k_ref, v_ref, gc_ref, gr_ref, ng_ref, o_ref, hf_s, hb_s, ct_ref, n_ref, m_ref):
    row = lax.broadcasted_iota(jnp.int32, (CHUNK, CHUNK), 0)
    col = lax.broadcasted_iota(jnp.int32, (CHUNK, CHUNK), 1)
    lower = col <= row
    upper = col >= row
    tri_lower = jnp.where(lower, 1.0, 0.0).astype(BF16)
    tri_upper = jnp.where(upper, 1.0, 0.0).astype(BF16)

    def chunk(c, z, hh):
        reverse = z == 1
        c0 = pl.multiple_of(c * CHUNK, CHUNK)
        q = q_ref[pl.ds(c0, CHUNK), hh * M_DK:(hh + 1) * M_DK]
        k = k_ref[pl.ds(c0, CHUNK), hh * M_DK:(hh + 1) * M_DK]
        v = v_ref[pl.ds(c0, CHUNK), hh * M_DV:(hh + 1) * M_DV]
        gc = gc_ref[hh, pl.ds(c0, CHUNK), :]
        gr = gr_ref[hh, :, pl.ds(c0, CHUNK)]
        ig_c = gc[:, 2 * z:2 * z + 1]
        lf_c = gc[:, 2 * z + 1:2 * z + 2]
        ig_r = gr[2 * z:2 * z + 1, :]
        if reverse:
            bc_c = _split_dot(tri_upper, gc, True)[:, 2 * z + 1:2 * z + 2]
            bc_r = _split_dot(tri_lower, gr, False)[2 * z + 1:2 * z + 2, :]
            mask = upper
        else:
            bc_c = _split_dot(tri_lower, gc, True)[:, 2 * z + 1:2 * z + 2]
            bc_r = _split_dot(tri_upper, gr, False)[2 * z + 1:2 * z + 2, :]
            mask = lower
        btot = jnp.sum(lf_c, axis=0, keepdims=True)
        m_prev = m_ref[z, hh]
        n_prev = n_ref[z, hh]
        ct_prev = ct_ref[z, hh]
        dlog = jnp.where(mask, bc_c - bc_r + ig_r, -jnp.inf)
        m_inter = bc_c + m_prev
        m_comb = jnp.maximum(m_inter, jnp.max(dlog, axis=1, keepdims=True))
        qk = lax.dot_general(q, k, (((1,), (1,)), ((), ())), preferred_element_type=F32)
        s = qk * jnp.exp(dlog - m_comb)
        inter = jnp.exp(m_inter - m_comb)
        num = (jnp.dot(s.astype(BF16), v, preferred_element_type=F32)
               + inter * jnp.dot(q, ct_prev.astype(BF16), preferred_element_type=F32))
        den = (jnp.sum(s, axis=1, keepdims=True)
               + inter * jnp.sum(q.astype(F32) * n_prev, axis=1, keepdims=True))
        h = num / jnp.maximum(jnp.abs(den), jnp.exp(-m_comb))
        wlog = btot - bc_c + ig_c
        mloc = jnp.max(wlog, axis=0, keepdims=True)
        wgt = jnp.exp(wlog - mloc)
        m_new = jnp.maximum(btot + m_prev, mloc)
        sp = jnp.exp(btot + m_prev - m_new)
        sl = jnp.exp(mloc - m_new)
        kf = k.astype(F32)
        vw = (v.astype(F32) * wgt).astype(BF16)
        ct_loc = jnp.dot(kf.T.astype(BF16), vw, preferred_element_type=F32)
        ct_ref[z, hh] = sp * ct_prev + sl * ct_loc
        n_ref[z, hh] = sp * n_prev + sl * jnp.sum(kf * wgt, axis=0, keepdims=True)
        m_ref[z, hh] = m_new
        return c0, h

    ct_ref[...] = jnp.zeros_like(ct_ref)
    n_ref[...] = jnp.zeros_like(n_ref)
    m_ref[...] = jnp.zeros_like(m_ref)

    n_chunks = o_ref.shape[0] // CHUNK
    n_lat = SEQ // CHUNK

    def both_directions(i, carry):
        for hh in range(ML_HEADS_PER_STEP):
            c0, h = chunk((i + n_lat) % n_chunks, 0, hh)
            hf_s[hh, pl.ds(c0, CHUNK), :] = h
            c0, h = chunk(n_chunks - 1 - i, 1, hh)
            hb_s[hh, pl.ds(c0, CHUNK), :] = h
        return carry

    lax.fori_loop(0, n_chunks, both_directions, 0)

    def head_norm(c, carry):
        c0 = pl.multiple_of(c * CHUNK, CHUNK)
        for hh in range(ML_HEADS_PER_STEP):
            tot = hf_s[hh, pl.ds(c0, CHUNK), :] + hb_s[hh, pl.ds(c0, CHUNK), :]
            mu = jnp.mean(tot, axis=1, keepdims=True)
            var = jnp.mean(jnp.square(tot - mu), axis=1, keepdims=True)
            lanes = slice(hh * M_DV, (hh + 1) * M_DV)
            o_ref[pl.ds(c0, CHUNK), lanes] = ((tot - mu) * lax.rsqrt(var + LN_EPS) * ng_ref[:, lanes]
                                             ).astype(o_ref.dtype)
        return carry

    lax.fori_loop(0, n_chunks, head_norm, 0)


def mlstm_cell(q, k, v, g_col, g_row, norm_g):
    b, lj, _ = q.shape
    hp = ML_HEADS_PER_STEP
    return pl.pallas_call(
        _mlstm_body,
        out_shape=jax.ShapeDtypeStruct((b, lj, M_HEADS * M_DV), BF16),
        grid=(b, M_HEADS // hp),
        in_specs=[pl.BlockSpec((None, lj, hp * M_DK), lambda i, h: (i, 0, h)),
                  pl.BlockSpec((None, lj, hp * M_DK), lambda i, h: (i, 0, h)),
                  pl.BlockSpec((None, lj, hp * M_DV), lambda i, h: (i, 0, h)),
                  pl.BlockSpec((None, hp, lj, 4), lambda i, h: (i, h, 0, 0)),
                  pl.BlockSpec((None, hp, 4, lj), lambda i, h: (i, h, 0, 0)),
                  pl.BlockSpec((1, hp * M_DV), lambda i, h: (0, h))],
        out_specs=pl.BlockSpec((None, lj, hp * M_DV), lambda i, h: (i, 0, h)),
        scratch_shapes=[pltpu.VMEM((hp, lj, M_DV), F32), pltpu.VMEM((hp, lj, M_DV), F32),
                        pltpu.VMEM((2, hp, M_DK, M_DV), F32), pltpu.VMEM((2, hp, 1, M_DK), F32),
                        pltpu.VMEM((2, hp, 1, 1), F32)],
        compiler_params=_params("parallel", "parallel"),
        name="mlstm_cell",
    )(q, k, v, g_col, g_row, norm_g)


def _tile_neighbours(i):
    r = i % TILES_PER_BATCH
    has_prev = jnp.where((r == 0) | (r == LATENT_TILES), 0.0, 1.0)
    has_next = jnp.where((r == LATENT_TILES - 1) | (r == TILES_PER_BATCH - 1), 0.0, 1.0)
    return has_prev, has_next


def _ml_up_body(h_ref, mod_ref, w_ref, o_ref):
    inp = h_ref[...] * (1.0 + mod_ref[MOD_SC:MOD_SC + 1, :]) + mod_ref[MOD_SH:MOD_SH + 1, :]
    o_ref[...] = jnp.dot(inp.astype(BF16), w_ref[...], preferred_element_type=F32).astype(o_ref.dtype)


def mod_proj(h, modtab, w, bias=None):
    t, d = h.shape
    n = w.shape[1]
    body = _ml_up_body
    args = [h, modtab, w]
    in_specs = [pl.BlockSpec((ROW_TILE, d), lambda i: (i, 0)),
                pl.BlockSpec((None, 6, d), _mod_index),
                pl.BlockSpec((d, n), lambda i: (0, 0))]
    if bias is not None:
        def body(h_ref, mod_ref, w_ref, b_ref, o_ref):
            inp = h_ref[...] * (1.0 + mod_ref[MOD_SC:MOD_SC + 1, :]) + mod_ref[MOD_SH:MOD_SH + 1, :]
            y = jnp.dot(inp.astype(BF16), w_ref[...], preferred_element_type=F32) + b_ref[...]
            o_ref[...] = y.astype(o_ref.dtype)
        args.append(bias)
        in_specs.append(pl.BlockSpec((1, n), lambda i: (0, 0)))
    return pl.pallas_call(
        body,
        out_shape=jax.ShapeDtypeStruct((t, n), BF16),
        grid=(t // ROW_TILE,),
        in_specs=in_specs,
        out_specs=pl.BlockSpec((ROW_TILE, n), lambda i: (i, 0)),
        compiler_params=_params("parallel"),
        name="mod_proj",
    )(*args)


def _ml_qk_body(prev_ref, main_ref, next_ref, cw_ref, cb_ref, w_ref, xc_ref, q_ref, k_ref, ext_s):
    has_prev, has_next = _tile_neighbours(pl.program_id(0))
    ext_s[0:CONV_HALO, :] = prev_ref[...].astype(F32) * has_prev
    ext_s[CONV_HALO:CONV_HALO + ROW_TILE, :] = main_ref[...].astype(F32)
    ext_s[CONV_HALO + ROW_TILE:, :] = next_ref[...].astype(F32) * has_next
    kk = cw_ref.shape[0]
    acc = cb_ref[...] + cw_ref[0:1, :] * ext_s[pl.ds(CONV_HALO - kk // 2, ROW_TILE), :]
    for j in range(1, kk):
        acc = acc + cw_ref[j:j + 1, :] * ext_s[pl.ds(CONV_HALO - kk // 2 + j, ROW_TILE), :]
    xc = (acc * jax.nn.sigmoid(acc)).astype(BF16)
    xc_ref[...] = xc
    qk = jnp.dot(xc, w_ref[...], preferred_element_type=F32)
    nq = q_ref.shape[1]
    q_ref[...] = qk[:, :nq].astype(BF16)
    k_ref[...] = (qk[:, nq:] * (M_DK ** -0.5)).astype(BF16)


def ml_qk(xm, conv_w, conv_b, w_qk):
    t, dm = xm.shape
    nq = M_HEADS * M_DK
    per = ROW_TILE // CONV_HALO
    last = t // CONV_HALO - 1
    row = lambda i: (i, 0)
    fixed = lambda i: (0, 0)
    return pl.pallas_call(
        _ml_qk_body,
        out_shape=(jax.ShapeDtypeStruct((t, dm), BF16), jax.ShapeDtypeStruct((t, nq), BF16),
                   jax.ShapeDtypeStruct((t, nq), BF16)),
        grid=(t // ROW_TILE,),
        in_specs=[pl.BlockSpec((CONV_HALO, dm), lambda i: (jnp.maximum(i * per - 1, 0), 0)),
                  pl.BlockSpec((ROW_TILE, dm), row),
                  pl.BlockSpec((CONV_HALO, dm), lambda i: (jnp.minimum((i + 1) * per, last), 0)),
                  pl.BlockSpec(conv_w.shape, fixed), pl.BlockSpec((1, dm), fixed),
                  pl.BlockSpec((dm, 2 * nq), fixed)],
        out_specs=(pl.BlockSpec((ROW_TILE, dm), row), pl.BlockSpec((ROW_TILE, nq), row),
                   pl.BlockSpec((ROW_TILE, nq), row)),
        scratch_shapes=[pltpu.VMEM((ROW_TILE + 2 * CONV_HALO, dm), F32)],
        compiler_params=_params("parallel"),
        name="ml_qk",
    )(xm, xm, xm, conv_w, conv_b, w_qk)


def _mm_act_body(x_ref, w_ref, o_ref, *, act):
    y = jnp.dot(x_ref[...], w_ref[...], preferred_element_type=F32)
    if act == "sigmoid":
        y = jax.nn.sigmoid(y)
    o_ref[...] = y.astype(o_ref.dtype)


def mm_act(x, w, act=None, *, tm=512, tn=1024):
    m, k = x.shape
    n = w.shape[1]
    return pl.pallas_call(
        functools.partial(_mm_act_body, act=act),
        out_shape=jax.ShapeDtypeStruct((m, n), BF16),
        grid=(n // tn, m // tm),
        in_specs=[pl.BlockSpec((tm, k), lambda j, i: (i, 0)),
                  pl.BlockSpec((k, tn), lambda j, i: (0, j))],
        out_specs=pl.BlockSpec((tm, tn), lambda j, i: (i, j)),
        compiler_params=_params("parallel", "parallel"),
        name="mm_act",
    )(x, w)


def _ml_gates_body(x_ref, w_ref, b_ref, o_ref):
    g = jnp.dot(x_ref[...], w_ref[...], preferred_element_type=F32) + b_ref[...]
    lane = lax.broadcasted_iota(jnp.int32, g.shape, 1)
    is_forget = (lane % (2 * M_HEADS)) >= M_HEADS
    o_ref[...] = jnp.where(is_forget, jax.nn.log_sigmoid(g), g)


def ml_gates(xm, w_g, b_g):
    t, dm = xm.shape
    return pl.pallas_call(
        _ml_gates_body,
        out_shape=jax.ShapeDtypeStruct((t, 128), F32),
        grid=(t // 512,),
        in_specs=[pl.BlockSpec((512, dm), lambda i: (i, 0)), pl.BlockSpec((dm, 128), lambda i: (0, 0)),
                  pl.BlockSpec((1, 128), lambda i: (0, 0))],
        out_specs=pl.BlockSpec((512, 128), lambda i: (i, 0)),
        compiler_params=_params("parallel"),
        name="ml_gates",
    )(xm, w_g, b_g)


def _post_ml_body(o_ref, hn_ref, xc_ref, skip_ref, w_ref, h_ref, mod_ref, lng_ref, lnb_ref, rwh_ref, rwl_ref,
                  ho_ref, f_ref, lg_ref):
    z = o_ref[...].astype(F32) * hn_ref[...].astype(F32) + skip_ref[...] * xc_ref[...].astype(F32)
    y = jnp.dot(z.astype(BF16), w_ref[...], preferred_element_type=F32)
    hn, fpk, lg = _post_epilogue(y, h_ref[...], mod_ref[MOD_GT:MOD_GT + 1, :], mod_ref[MOD_SC2:MOD_SC2 + 1, :],
                                 mod_ref[MOD_SH2:MOD_SH2 + 1, :], lng_ref[...], lnb_ref[...], rwh_ref[...],
                                 rwl_ref[...])
    ho_ref[...] = hn
    f_ref[...] = fpk
    lg_ref[...] = lg


def post_mixer_ml(o, hn, xc, skip, w_out, h, modtab, ln_g, ln_b, rw_hi, rw_lo):
    t, k = o.shape
    d = w_out.shape[1]
    row = lambda i: (i, 0)
    fixed = lambda i: (0, 0)
    return pl.pallas_call(
        _post_ml_body,
        out_shape=(jax.ShapeDtypeStruct((t, d), F32), jax.ShapeDtypeStruct((t, d // 2), jnp.uint32),
                   jax.ShapeDtypeStruct((t, ROUTER_PAD), F32)),
        grid=(t // ROW_TILE,),
        in_specs=[pl.BlockSpec((ROW_TILE, k), row), pl.BlockSpec((ROW_TILE, k), row),
                  pl.BlockSpec((ROW_TILE, k), row), pl.BlockSpec((1, k), fixed),
                  pl.BlockSpec((k, d), fixed),
                  pl.BlockSpec((ROW_TILE, d), row),
                  pl.BlockSpec((None, 6, d), _mod_index),
                  pl.BlockSpec((1, d), fixed), pl.BlockSpec((1, d), fixed),
                  pl.BlockSpec((d, ROUTER_PAD), fixed), pl.BlockSpec((d, ROUTER_PAD), fixed)],
        out_specs=(pl.BlockSpec((ROW_TILE, d), row), pl.BlockSpec((ROW_TILE, d // 2), row),
                   pl.BlockSpec((ROW_TILE, ROUTER_PAD), row)),
        compiler_params=_params("parallel"),
        name="post_mixer_ml",
    )(o, hn, xc, skip, w_out, h, modtab, ln_g, ln_b, rw_hi, rw_lo)


def _hy_conv(u_ref, cw_ref, cb_ref, ext_s):
    n_time = u_ref.shape[0]
    ext_s[0:8, :] = jnp.zeros((8, ext_s.shape[1]), F32)
    ext_s[8:8 + n_time, :] = u_ref[...].astype(F32)
    ext_s[8 + n_time:, :] = jnp.zeros((8, ext_s.shape[1]), F32)
    kk = cw_ref.shape[0]
    acc = cb_ref[...] + cw_ref[0:1, :] * ext_s[pl.ds(8 - kk // 2, n_time), :]
    for j in range(1, kk):
        acc = acc + cw_ref[j:j + 1, :] * ext_s[pl.ds(8 - kk // 2 + j, n_time), :]
    return acc


def _hy_spectrum(fm_ref, sig, hf_ref, y_ref):
    x = jnp.dot(fm_ref[...], sig, preferred_element_type=F32)
    half = x.shape[0] // 2
    xt, xb = x[:half], x[half:]
    ht, hb = hf_ref[0:half, :], hf_ref[half:, :]
    first = lax.broadcasted_iota(jnp.int32, xt.shape, 0) == 0
    y_ref[0:half, :] = jnp.where(first, xt * ht, xt * ht - xb * hb).astype(y_ref.dtype)
    y_ref[half:, :] = jnp.where(first, xb * hb, xt * hb + xb * ht).astype(y_ref.dtype)


def _hy_fwd_conv_body(u_ref, cw_ref, cb_ref, fm_ref, hf_ref, v_ref, y_ref, ext_s):
    v = _hy_conv(u_ref, cw_ref, cb_ref, ext_s).astype(BF16)
    v_ref[...] = v
    _hy_spectrum(fm_ref, v, hf_ref, y_ref)


def _hy_fwd_body(s_ref, fm_ref, hf_ref, y_ref):
    _hy_spectrum(fm_ref, s_ref[...], hf_ref, y_ref)


def _hy_inv_body(y_ref, gm_ref, u_ref, cw_ref, cb_ref, s_ref, skip_ref, z_ref, ext_s):
    y = jnp.dot(gm_ref[...], y_ref[...], preferred_element_type=F32)
    g = _hy_conv(u_ref, cw_ref, cb_ref, ext_s)
    z_ref[...] = (g * (y + s_ref[...].astype(F32) * skip_ref[...])).astype(z_ref.dtype)


def hyena_segment(u3, seg_block, n_time, conv_w, conv_b, hf, skip):
    b = u3.shape[0]
    tn = HY_TN
    nj = D_MODEL // tn
    fm, gm = _dft_mats(n_time)
    n = fm.shape[0]
    grid = (nj, b)
    u_spec = lambda part: pl.BlockSpec((None, n_time, tn), lambda j, i: (i, seg_block, part * nj + j))
    cw_spec = lambda part: pl.BlockSpec((conv_w.shape[0], tn), lambda j, i: (0, part * nj + j))
    cb_spec = lambda part: pl.BlockSpec((1, tn), lambda j, i: (0, part * nj + j))
    fixed = lambda j, i: (0, 0)
    sig_spec = pl.BlockSpec((None, n_time, tn), lambda j, i: (i, 0, j))
    spec_spec = pl.BlockSpec((None, n, tn), lambda j, i: (i, 0, j))
    hf_spec = lambda c: pl.BlockSpec((n, tn), lambda j, i: (0, c * nj + j))
    skip_spec = lambda c: pl.BlockSpec((None, 1, tn), lambda j, i: (c, 0, j))
    ext = pltpu.VMEM((n_time + 16, tn), F32)
    sig_shape = jax.ShapeDtypeStruct((b, n_time, D_MODEL), BF16)
    spec_shape = jax.ShapeDtypeStruct((b, n, D_MODEL), BF16)
    par = _params("parallel", "parallel")
    skip3 = skip[:, None, :]

    v, y1 = pl.pallas_call(
        _hy_fwd_conv_body, out_shape=(sig_shape, spec_shape), grid=grid,
        in_specs=[u_spec(0), cw_spec(0), cb_spec(0), pl.BlockSpec(fm.shape, fixed), hf_spec(0)],
        out_specs=(sig_spec, spec_spec), scratch_shapes=[ext], compiler_params=par, name="hy_fwd_conv",
    )(u3, conv_w, conv_b, fm, hf)
    z1 = pl.pallas_call(
        _hy_inv_body, out_shape=sig_shape, grid=grid,
        in_specs=[spec_spec, pl.BlockSpec(gm.shape, fixed), u_spec(1), cw_spec(1), cb_spec(1), sig_spec,
                  skip_spec(0)],
        out_specs=sig_spec, scratch_shapes=[ext], compiler_params=par, name="hy_inv",
    )(y1, gm, u3, conv_w, conv_b, v, skip3)
    y2 = pl.pallas_call(
        _hy_fwd_body, out_shape=spec_shape, grid=grid,
        in_specs=[sig_spec, pl.BlockSpec(fm.shape, fixed), hf_spec(1)],
        out_specs=spec_spec, compiler_params=par, name="hy_fwd",
    )(z1, fm, hf)
    return pl.pallas_call(
        _hy_inv_body, out_shape=sig_shape, grid=grid,
        in_specs=[spec_spec, pl.BlockSpec(gm.shape, fixed), u_spec(2), cw_spec(2), cb_spec(2), sig_spec,
                  skip_spec(1)],
        out_specs=sig_spec, scratch_shapes=[ext], compiler_params=par, name="hy_inv",
    )(y2, gm, u3, conv_w, conv_b, z1, skip3)


def _moe_body(te_ref, tf_ref, nu_ref, x_ref, wg_ref, wu_ref, wd_ref, o_ref, wg_s, wu_s, wd_s):
    i = pl.program_id(0)

    @pl.when(i < nu_ref[0])
    def _():
        @pl.when(tf_ref[i] == 1)
        def _():
            wg_s[...] = wg_ref[...].astype(BF16)
            wu_s[...] = wu_ref[...].astype(BF16)
            wd_s[...] = wd_ref[...].astype(BF16)

        x = _unpack_bf16_pairs(x_ref[...])
        a = jnp.dot(x, wg_s[...], preferred_element_type=F32)
        u = jnp.dot(x, wu_s[...], preferred_element_type=F32)
        hid = (a * jax.nn.sigmoid(a)) * u
        o_ref[...] = jnp.dot(hid.astype(BF16), wd_s[...], preferred_element_type=F32)

    @pl.when(i >= nu_ref[0])
    def _():
        o_ref[...] = jnp.zeros_like(o_ref)


def moe_experts(layer, tile_expert, tile_first, n_used, xs, w_gate, w_up, w_down):
    n_rows = xs.shape[0]
    d = 2 * xs.shape[1]
    n_tiles = n_rows // MOE_TILE
    wmap_in = lambda i, te, tf, nu: (layer, te[i], 0, 0)
    grid_spec = pltpu.PrefetchScalarGridSpec(
        num_scalar_prefetch=3,
        grid=(n_tiles,),
        in_specs=[pl.BlockSpec((MOE_TILE, d // 2), lambda i, te, tf, nu: (i, 0)),
                  pl.BlockSpec((None, None, d, D_EXPERT), wmap_in),
                  pl.BlockSpec((None, None, d, D_EXPERT), wmap_in),
                  pl.BlockSpec((None, None, D_EXPERT, d), wmap_in)],
        out_specs=pl.BlockSpec((MOE_TILE, d), lambda i, te, tf, nu: (i, 0)),
        scratch_shapes=[pltpu.VMEM((d, D_EXPERT), BF16), pltpu.VMEM((d, D_EXPERT), BF16),
                        pltpu.VMEM((D_EXPERT, d), BF16)],
    )
    return pl.pallas_call(
        _moe_body,
        out_shape=jax.ShapeDtypeStruct((n_rows, d), F32),
        grid_spec=grid_spec,
        compiler_params=_params("arbitrary"),
        name="moe_experts",
    )(tile_expert, tile_first, n_used, xs, w_gate, w_up, w_down)


def moe(layer, f, logits, router_b, w_gate, w_up, w_down):
    t = f.shape[0]
    scores = jax.nn.sigmoid(logits[:, :N_EXPERTS])
    sel = (scores + router_b).reshape(t, N_GROUPS, EXP_PER_GROUP)
    lane = jnp.arange(EXP_PER_GROUP, dtype=jnp.int32)
    i1 = jnp.argmax(sel, -1).astype(jnp.int32)
    m1 = jnp.max(sel, -1)
    rest = jnp.where(lane == i1[..., None], -jnp.inf, sel)
    i2 = jnp.argmax(rest, -1).astype(jnp.int32)
    m2 = jnp.max(rest, -1)
    g_best = jnp.argmax(m1 + m2, -1).astype(jnp.int32)
    in_best = jnp.arange(N_GROUPS, dtype=jnp.int32) == g_best[:, None]
    pick = lambda a: jnp.sum(jnp.where(in_best, a, 0), axis=1)
    idx = g_best[:, None] * EXP_PER_GROUP + jnp.stack([pick(i1), pick(i2)], axis=-1)
    chosen = jnp.arange(N_EXPERTS, dtype=jnp.int32) == idx[..., None]
    wts = jnp.sum(jnp.where(chosen, scores[:, None, :], 0.0), axis=-1)
    wts = wts / jnp.sum(wts, -1, keepdims=True)

    n_pick = t * TOP_K
    n_tiles = n_pick // MOE_TILE + N_EXPERTS
    n_rows = n_tiles * MOE_TILE
    flat_e = idx.reshape(-1).astype(jnp.int32)
    onehot = (flat_e[:, None] == jnp.arange(N_EXPERTS, dtype=jnp.int32)[None, :]).astype(jnp.int32)
    csum = jnp.cumsum(onehot, axis=0)
    counts = csum[-1]
    rank = jnp.sum(csum * onehot, axis=1) - 1
    tiles_per = (counts + MOE_TILE - 1) // MOE_TILE
    tile_end = jnp.cumsum(tiles_per)
    tile_start = tile_end - tiles_per
    dest = tile_start[flat_e] * MOE_TILE + rank
    flat_t = jnp.arange(n_pick, dtype=jnp.int32) // TOP_K
    row_token = jnp.zeros((n_rows,), jnp.int32).at[dest].set(flat_t)
    n_used = tile_end[-1]
    tile_ids = jnp.arange(n_tiles, dtype=jnp.int32)
    tile_expert = jnp.searchsorted(tile_end, jnp.minimum(tile_ids, n_used - 1), side="right").astype(jnp.int32)
    tile_expert = jnp.minimum(tile_expert, N_EXPERTS - 1)
    tile_first = ((tile_ids == tile_start[tile_expert]) & (tile_ids < n_used)).astype(jnp.int32)

    xs = jnp.take(f, row_token, axis=0)
    ys = moe_experts(layer, tile_expert, tile_first, n_used.reshape(1).astype(jnp.int32), xs, w_gate, w_up, w_down)
    pos = dest.reshape(t, TOP_K)
    return jnp.take(ys, pos[:, 0], axis=0), jnp.take(ys, pos[:, 1], axis=0), wts


def sincos_2d(n_tok):
    rows = n_tok // GRID_W
    quarter = D_MODEL // 4
    omega = 1.0 / (10000.0 ** (jnp.arange(quarter, dtype=F32) / quarter))
    ar = jnp.arange(rows, dtype=F32)[:, None] * omega
    ac = jnp.arange(GRID_W, dtype=F32)[:, None] * omega
    er = jnp.concatenate([jnp.sin(ar), jnp.cos(ar)], -1)
    ec = jnp.concatenate([jnp.sin(ac), jnp.cos(ac)], -1)
    half = D_MODEL // 2
    pos = jnp.concatenate([jnp.broadcast_to(er[:, None], (rows, GRID_W, half)),
                           jnp.broadcast_to(ec[None], (rows, GRID_W, half))], -1)
    return pos.reshape(rows * GRID_W, D_MODEL)


def rglru_layer(h3, modtab_t, w_in, conv_w, conv_b, ga_w, ga_b, gx_w, gx_b, lam, w_out, ln_g, ln_b, rw_hi, rw_lo):
    def block_diag(w):
        eye = jnp.eye(RG_BLOCKS, dtype=w.dtype)
        return jnp.einsum("nkj,nm->nkmj", w, eye).reshape(D_RNN, D_RNN)

    w_gates = jnp.stack([jnp.concatenate([block_diag(ga_w[z]), block_diag(gx_w[z])], axis=1)
                         for z in range(2)]).astype(BF16)
    gate, rec_tm = rg_in(h3, modtab_t, w_in.astype(BF16))
    h_tm = rg_scan(rec_tm, conv_w, conv_b[None], w_gates, ga_b[:, None], gx_b[:, None], lam[:, None])
    hn, fpk, lg = rg_out(gate, h_tm, w_out.astype(BF16), h3, modtab_t, ln_g, ln_b, rw_hi, rw_lo)
    rows = BATCH * L_JOINT
    return hn.reshape(rows, D_MODEL), fpk.reshape(rows, D_MODEL // 2), lg.reshape(rows, ROUTER_PAD)


def mlstm_layer(hj, modtab, w_up, conv_w, conv_b, w_q, w_k, w_v, w_o, w_if, b_if, norm_g, skip, w_down,
                ln_g, ln_b, rw_hi, rw_lo):
    rows = hj.shape[0]
    xm = mod_proj(hj, modtab, w_up.astype(BF16))
    xc, q, k = ml_qk(xm, conv_w, conv_b[None], jnp.concatenate([w_q, w_k], axis=1).astype(BF16))
    v = mm_act(xm, w_v.astype(BF16))
    o = mm_act(xm, w_o.astype(BF16), "sigmoid")
    n_gate = 4 * M_HEADS
    w_g = jnp.concatenate([w_if[0], w_if[1], jnp.zeros((D_M, 128 - n_gate), F32)], axis=1).astype(BF16)
    b_g = jnp.concatenate([b_if[0], b_if[1], jnp.zeros((128 - n_gate,), F32)])[None]
    g = ml_gates(xm, w_g, b_g)[:, :n_gate].reshape(BATCH, L_JOINT, 2, 2, M_HEADS)
    g4 = jnp.transpose(g, (0, 1, 4, 2, 3)).reshape(BATCH, L_JOINT, M_HEADS, 4)
    g_col = jnp.transpose(g4, (0, 2, 1, 3))
    g_row = jnp.transpose(g4, (0, 2, 3, 1))
    r3 = lambda a: a.reshape(BATCH, L_JOINT, a.shape[-1])
    hn = mlstm_cell(r3(q), r3(k), r3(v), g_col, g_row, norm_g[None]).reshape(rows, D_M)
    return post_mixer_ml(o, hn, xc, skip[None], w_down.astype(BF16), hj, modtab, ln_g, ln_b, rw_hi, rw_lo)


def _dft_mats(n_time):
    n = 3 * n_time // 2
    half = n // 2
    kk = np.arange(half, dtype=np.int64)[:, None]
    tt = np.arange(n_time, dtype=np.int64)[None, :]
    ang = 2.0 * np.pi * ((kk * tt) % n).astype(np.float64) / n
    top = np.cos(ang)
    bot = -np.sin(ang)
    bot[0] = np.cos(np.pi * tt[0])
    fwd = np.concatenate([top, bot], axis=0)
    mm_ = (np.arange(n_time, dtype=np.int64) + n_time // 2)[:, None]
    ang2 = 2.0 * np.pi * ((mm_ * kk.T) % n).astype(np.float64) / n
    wk = np.full((1, half), 2.0)
    wk[0, 0] = 1.0
    itop = wk * np.cos(ang2) / n
    ibot = -2.0 * np.sin(ang2) / n
    ibot[:, 0] = np.cos(np.pi * mm_[:, 0]) / n
    inv = np.concatenate([itop, ibot], axis=1)
    return jnp.asarray(fwd, dtype=BF16), jnp.asarray(inv, dtype=BF16)


def hyena_filters(n_time, w1, b1, fq1, w2, b2, fq2, w3):
    hp = lax.Precision.HIGHEST
    t01 = jnp.linspace(0.0, 1.0, n_time, dtype=F32)
    bands = jnp.linspace(1e-4, H_BANDS - 1, H_BANDS, dtype=F32)
    ang = (2.0 * math.pi / n_time) * jnp.arange(n_time, dtype=F32)[:, None] * bands[None, :]
    z = jnp.concatenate([t01[:, None], jnp.cos(ang), -jnp.sin(ang)], -1)
    hdn = jnp.sin(fq1 * (jnp.dot(z, w1, precision=hp) + b1))
    hdn = jnp.sin(fq2 * (jnp.dot(hdn, w2, precision=hp) + b2))
    filt = jnp.dot(hdn, w3, precision=hp).reshape(n_time, 2, D_MODEL)
    dist = jnp.abs(jnp.arange(n_time) - n_time // 2).astype(F32) * (2.0 / n_time)
    d_max = math.log(H_DECAY_TARGET) / H_FAST
    d_min = math.log(H_DECAY_TARGET) / H_SLOW
    deltas = jnp.abs(jnp.linspace(d_min, d_max, D_MODEL, dtype=F32))
    window = jnp.exp(-dist[:, None] * deltas[None, :])
    return filt * window[:, None, :]


def hyena_layer(hj, modtab, w_in, b_in, conv_w, conv_b, fparams, skip, w_out, ln_g, ln_b, rw_hi, rw_lo):
    rows = hj.shape[0]
    u3 = mod_proj(hj, modtab, w_in.astype(BF16), b_in[None]).reshape(BATCH, L_JOINT, 3 * D_MODEL)
    parts = []
    for seg_block, n_time in ((0, SEQ), (SEQ // CTX_LEN, CTX_LEN)):
        filt = hyena_filters(n_time, *fparams)
        fm, _ = _dft_mats(n_time)
        hf = bmm_left(fm, filt.reshape(1, n_time, 2 * D_MODEL))[0]
        parts.append(hyena_segment(u3, seg_block, n_time, conv_w, conv_b[None], hf, skip))
    z = jnp.concatenate(parts, axis=1).reshape(rows, D_MODEL)
    return post_mixer(z, w_out.astype(BF16), hj, modtab, ln_g, ln_b, rw_hi, rw_lo)


def kernel(x, c, ctx, c_ctx, router_w, router_b, ada_w, ada_b, ln_g, ln_b, moe_w_gate, moe_w_up, moe_w_down, rg_w_in, rg_conv_w, rg_conv_b, rg_gate_a_w, rg_gate_a_b, rg_gate_x_w, rg_gate_x_b, rg_lambda, rg_w_out, ml_w_up, ml_conv_w, ml_conv_b, ml_w_q, ml_w_k, ml_w_v, ml_w_o, ml_w_if, ml_b_if, ml_norm_g, ml_skip, ml_w_down, hy_w_in, hy_b_in, hy_conv_w, hy_conv_b, hy_f_w1, hy_f_b1, hy_f_freq1, hy_f_w2, hy_f_b2, hy_f_freq2, hy_f_w3, hy_skip, hy_w_out):
    bsz = x.shape[0]
    rows = bsz * L_JOINT
    hx = x + sincos_2d(SEQ)[None]
    hj = jnp.concatenate([hx, ctx], axis=1).reshape(rows, D_MODEL)
    is_ctx = (jnp.arange(L_JOINT) >= SEQ)[None, :, None]

    cond = jnp.concatenate([jax.nn.silu(c), jax.nn.silu(c_ctx)[None],
                            jnp.zeros((16 - bsz - 1, D_MODEL), F32)], axis=0)
    rw_pad = jnp.concatenate([router_w, jnp.zeros((D_MODEL, ROUTER_PAD - N_EXPERTS), F32)], axis=1)
    rw_hi = rw_pad.astype(BF16)
    rw_lo = (rw_pad - rw_hi.astype(F32)).astype(BF16)

    for i in range(DEPTH):
        kind, j = i % N_MIXERS, i // N_MIXERS
        mod = (mm(cond, ada_w[i].astype(BF16), tm=16, tn=D_MODEL) + ada_b[i]).reshape(16, 6, D_MODEL)
        mod_x = mod[:bsz]
        mod_c = jnp.broadcast_to(mod[bsz][None], (bsz, 6, D_MODEL))
        modtab = jnp.stack([mod_x, mod_c], axis=1).reshape(2 * bsz, 6, D_MODEL)
        lng, lnb = ln_g[i, 0][None], ln_b[i, 0][None]
        if kind == 0:
            modtab_t = jnp.stack([jnp.transpose(mod_x, (1, 0, 2)), jnp.transpose(mod_c, (1, 0, 2))])
            hj, f, logits = rglru_layer(hj.reshape(bsz, L_JOINT, D_MODEL), modtab_t, rg_w_in[j], rg_conv_w[j],
                                        rg_conv_b[j], rg_gate_a_w[j], rg_gate_a_b[j], rg_gate_x_w[j],
                                        rg_gate_x_b[j], rg_lambda[j], rg_w_out[j], lng, lnb, rw_hi, rw_lo)
        elif kind == 1:
            hj, f, logits = mlstm_layer(hj, modtab, ml_w_up[j], ml_conv_w[j], ml_conv_b[j], ml_w_q[j], ml_w_k[j],
                                        ml_w_v[j], ml_w_o[j], ml_w_if[j], ml_b_if[j], ml_norm_g[j], ml_skip[j],
                                        ml_w_down[j], lng, lnb, rw_hi, rw_lo)
        else:
            fparams = (hy_f_w1[j], hy_f_b1[j], hy_f_freq1[j], hy_f_w2[j], hy_f_b2[j], hy_f_freq2[j], hy_f_w3[j])
            hj, f, logits = hyena_layer(hj, modtab, hy_w_in[j], hy_b_in[j], hy_conv_w[j], hy_conv_b[j], fparams,
                                        hy_skip[j], hy_w_out[j], lng, lnb, rw_hi, rw_lo)
        ya, yb, wts = moe(i, f, logits, router_b, moe_w_gate, moe_w_up, moe_w_down)
        hj = moe_combine(hj, ya, yb, wts, modtab, ln_g[i, 1][None], ln_b[i, 1][None])
    return hj.reshape(bsz, L_JOINT, D_MODEL)[:, :SEQ]
```

```python
import functools
import math

import numpy as np
import jax
import jax.numpy as jnp
from jax import lax
from jax.experimental import pallas as pl
from jax.experimental.pallas import tpu as pltpu

F32 = jnp.float32
BF16 = jnp.bfloat16

D_MODEL = 1024
BATCH = 8
SEQ = 2048
DEPTH = 4
GRID_W = 64
CTX_LEN = 256
L_JOINT = SEQ + CTX_LEN
N_MIXERS = 3
ALPHA = (2.0 * DEPTH) ** 0.25
LN_EPS = 1e-6

D_RNN = 1408
RG_BLOCKS = 16
RG_BS = D_RNN // RG_BLOCKS
RG_C = 8.0

D_M = 2 * D_MODEL
M_HEADS = 8
M_DK = 128
M_DV = D_M // M_HEADS
CHUNK = 128

H_EMB = 33
H_BANDS = (H_EMB - 1) // 2
H_DECAY_TARGET = 1e-2
H_FAST = 0.3
H_SLOW = 1.5

N_EXPERTS = 16
N_GROUPS = 4
EXP_PER_GROUP = N_EXPERTS // N_GROUPS
TOP_K = 2
D_EXPERT = 512
ROUTER_PAD = 128

VMEM_LIMIT_BYTES = 56 * 1024 * 1024
ROW_TILE = 256
MOE_TILE = 256
RG_TBLK = 32
RG_LANES = D_RNN // 128
RG_CONV = 4
RG_WIN = 3
assert RG_BS <= 128 and RG_WIN * 128 >= 128 + 2 * (RG_BS - 1)
CONV_HALO = 16
ML_HEADS_PER_STEP = 2
HY_TN = 256
TILES_PER_BATCH = L_JOINT // ROW_TILE
LATENT_TILES = SEQ // ROW_TILE

MOD_SH, MOD_SC, MOD_GT, MOD_SH2, MOD_SC2, MOD_GT2 = range(6)


def _params(*sem):
    return pltpu.CompilerParams(dimension_semantics=sem, vmem_limit_bytes=VMEM_LIMIT_BYTES)


def _mod_index(i):
    return (2 * (i // TILES_PER_BATCH) + (i % TILES_PER_BATCH) // LATENT_TILES, 0, 0)


def _layer_norm_rows(r, g, b):
    mu = jnp.mean(r, axis=-1, keepdims=True)
    var = jnp.mean(jnp.square(r - mu), axis=-1, keepdims=True)
    return (r - mu) * lax.rsqrt(var + LN_EPS) * g + b


def _mm_body(x_ref, w_ref, o_ref):
    o_ref[...] = jnp.dot(x_ref[...].astype(BF16), w_ref[...], preferred_element_type=F32).astype(o_ref.dtype)


def mm(x, w, *, tm=512, tn=None, out_dtype=F32):
    m, k = x.shape
    n = w.shape[1]
    tn = n if tn is None else tn
    assert m % tm == 0 and n % tn == 0
    return pl.pallas_call(
        _mm_body,
        out_shape=jax.ShapeDtypeStruct((m, n), out_dtype),
        grid=(n // tn, m // tm),
        in_specs=[pl.BlockSpec((tm, k), lambda j, i: (i, 0)),
                  pl.BlockSpec((k, tn), lambda j, i: (0, j))],
        out_specs=pl.BlockSpec((tm, tn), lambda j, i: (i, j)),
        compiler_params=_params("parallel", "parallel"),
        name="mm",
    )(x, w)


def _bmm_body(a_ref, x_ref, o_ref):
    o_ref[...] = jnp.dot(a_ref[...], x_ref[...].astype(BF16), preferred_element_type=F32).astype(o_ref.dtype)


def bmm_left(a, x, *, tn=256, out_dtype=F32):
    mo, k = a.shape
    b, _, d = x.shape
    assert d % tn == 0
    return pl.pallas_call(
        _bmm_body,
        out_shape=jax.ShapeDtypeStruct((b, mo, d), out_dtype),
        grid=(b, d // tn),
        in_specs=[pl.BlockSpec((mo, k), lambda i, j: (0, 0)),
                  pl.BlockSpec((None, k, tn), lambda i, j: (i, 0, j))],
        out_specs=pl.BlockSpec((None, mo, tn), lambda i, j: (i, 0, j)),
        compiler_params=_params("parallel", "parallel"),
        name="dft_mm",
    )(a, x)


def _pack_bf16_pairs(f):
    half = f.shape[1] // 2
    hi = lax.bitcast_convert_type(f[:, :half].astype(BF16).astype(F32), jnp.uint32)
    lo = lax.bitcast_convert_type(f[:, half:].astype(BF16).astype(F32), jnp.uint32)
    return hi | (lo >> 16)


def _unpack_bf16_pairs(u):
    hi = lax.bitcast_convert_type(u & jnp.uint32(0xFFFF0000), F32).astype(BF16)
    lo = lax.bitcast_convert_type(u << 16, F32).astype(BF16)
    return jnp.concatenate([hi, lo], axis=1)


def _post_epilogue(y, h, gt, sc2, sh2, ln_g, ln_b, rw_hi, rw_lo):
    hn = _layer_norm_rows(ALPHA * h + gt * y, ln_g, ln_b)
    f = hn * (1.0 + sc2) + sh2
    f_hi = f.astype(BF16)
    f_lo = (f - f_hi.astype(F32)).astype(BF16)
    lg = (jnp.dot(f_hi, rw_hi, preferred_element_type=F32) + jnp.dot(f_lo, rw_hi, preferred_element_type=F32)
          + jnp.dot(f_hi, rw_lo, preferred_element_type=F32))
    return hn, _pack_bf16_pairs(f), lg


def _post_body(z_ref, w_ref, h_ref, mod_ref, lng_ref, lnb_ref, rwh_ref, rwl_ref, ho_ref, f_ref, lg_ref):
    y = jnp.dot(z_ref[...].astype(BF16), w_ref[...], preferred_element_type=F32)
    hn, fpk, lg = _post_epilogue(y, h_ref[...], mod_ref[MOD_GT:MOD_GT + 1, :], mod_ref[MOD_SC2:MOD_SC2 + 1, :],
                                 mod_ref[MOD_SH2:MOD_SH2 + 1, :], lng_ref[...], lnb_ref[...], rwh_ref[...],
                                 rwl_ref[...])
    ho_ref[...] = hn
    f_ref[...] = fpk
    lg_ref[...] = lg


def post_mixer(z, w_out, h, modtab, ln_g, ln_b, rw_hi, rw_lo):
    t, k = z.shape
    d = w_out.shape[1]
    row = lambda i: (i, 0)
    fixed = lambda i: (0, 0)
    return pl.pallas_call(
        _post_body,
        out_shape=(jax.ShapeDtypeStruct((t, d), F32), jax.ShapeDtypeStruct((t, d // 2), jnp.uint32),
                   jax.ShapeDtypeStruct((t, ROUTER_PAD), F32)),
        grid=(t // ROW_TILE,),
        in_specs=[pl.BlockSpec((ROW_TILE, k), row),
                  pl.BlockSpec((k, d), fixed),
                  pl.BlockSpec((ROW_TILE, d), row),
                  pl.BlockSpec((None, 6, d), _mod_index),
                  pl.BlockSpec((1, d), fixed),
                  pl.BlockSpec((1, d), fixed),
                  pl.BlockSpec((d, ROUTER_PAD), fixed),
                  pl.BlockSpec((d, ROUTER_PAD), fixed)],
        out_specs=(pl.BlockSpec((ROW_TILE, d), row), pl.BlockSpec((ROW_TILE, d // 2), row),
                   pl.BlockSpec((ROW_TILE, ROUTER_PAD), row)),
        compiler_params=_params("parallel"),
        name="post_mixer",
    )(z, w_out, h, modtab, ln_g, ln_b, rw_hi, rw_lo)


def _combine_body(h_ref, ya_ref, yb_ref, w_ref, mod_ref, lng_ref, lnb_ref, o_ref):
    w = w_ref[...]
    ya = _unpack_bf16_pairs(ya_ref[...]).astype(F32)
    yb = _unpack_bf16_pairs(yb_ref[...]).astype(F32)
    y2 = w[:, 0:1] * ya + w[:, 1:2] * yb
    gt2 = mod_ref[MOD_GT2:MOD_GT2 + 1, :]
    o_ref[...] = _layer_norm_rows(ALPHA * h_ref[...] + gt2 * y2, lng_ref[...], lnb_ref[...])


def moe_combine(h, ya, yb, wts, modtab, ln_g, ln_b):
    t, d = h.shape
    row = lambda i: (i, 0)
    fixed = lambda i: (0, 0)
    return pl.pallas_call(
        _combine_body,
        out_shape=jax.ShapeDtypeStruct((t, d), F32),
        grid=(t // ROW_TILE,),
        in_specs=[pl.BlockSpec((ROW_TILE, d), row), pl.BlockSpec((ROW_TILE, d // 2), row),
                  pl.BlockSpec((ROW_TILE, d // 2), row), pl.BlockSpec((ROW_TILE, TOP_K), row),
                  pl.BlockSpec((None, 6, d), _mod_index),
                  pl.BlockSpec((1, d), fixed), pl.BlockSpec((1, d), fixed)],
        out_specs=pl.BlockSpec((ROW_TILE, d), row),
        compiler_params=_params("parallel"),
        name="moe_combine",
    )(h, ya, yb, wts, modtab, ln_g, ln_b)


def _rg_in_body(h_ref, mod_ref, w_ref, gate_ref, rec_ref):
    tt = h_ref.shape[1]
    sc = mod_ref[MOD_SC][:, None, :]
    sh = mod_ref[MOD_SH][:, None, :]
    inp = (h_ref[...] * (1.0 + sc) + sh).reshape(BATCH * tt, D_MODEL)
    p = jnp.dot(inp.astype(BF16), w_ref[...], preferred_element_type=F32)
    gate_ref[...] = jax.nn.gelu(p[:, :D_RNN]).astype(gate_ref.dtype).reshape(BATCH, tt, D_RNN)
    for b in range(BATCH):
        rec_b = p[b * tt:(b + 1) * tt, D_RNN:]
        for j in range(RG_LANES):
            rec_ref[j, pl.ds(b, tt, stride=BATCH), :] = rec_b[:, j * 128:(j + 1) * 128]


def rg_in(h3, modtab_t, w_in):
    nblk = L_JOINT // RG_TBLK
    return pl.pallas_call(
        _rg_in_body,
        out_shape=(jax.ShapeDtypeStruct((BATCH, L_JOINT, D_RNN), BF16),
                   jax.ShapeDtypeStruct((RG_LANES, L_JOINT * BATCH, 128), F32)),
        grid=(nblk,),
        in_specs=[pl.BlockSpec((BATCH, RG_TBLK, D_MODEL), lambda i: (0, i, 0)),
                  pl.BlockSpec((None, 6, BATCH, D_MODEL), lambda i: (i // (SEQ // RG_TBLK), 0, 0, 0)),
                  pl.BlockSpec((D_MODEL, 2 * D_RNN), lambda i: (0, 0))],
        out_specs=(pl.BlockSpec((BATCH, RG_TBLK, D_RNN), lambda i: (0, i, 0)),
                   pl.BlockSpec((RG_LANES, RG_TBLK * BATCH, 128), lambda i: (0, i, 0))),
        compiler_params=_params("parallel"),
        name="rg_in",
    )(h3, modtab_t, w_in)


def _rg_block(z, i):
    nblk = L_JOINT // RG_TBLK
    nlat = SEQ // RG_TBLK
    return jnp.where(z == 0, (i + nlat) % nblk, nblk - 1 - i)


def _rg_window_start(j):
    return min(max(j - 1, 0), RG_LANES - RG_WIN)


def _rg_scan_body(prev_ref, main_ref, next_ref, cw_ref, cb_ref, wg_ref, ba_ref, bx_ref, lam_ref, ho_ref,
                  a_s, u_s, st_ref):
    z = pl.program_id(0)
    i = pl.program_id(1)
    nblk = L_JOINT // RG_TBLK
    nlat = SEQ // RG_TBLK
    blk = _rg_block(z, i)
    rows = RG_TBLK * BATCH

    @pl.when(i == 0)
    def _():
        st_ref[...] = jnp.zeros_like(st_ref)

    has_prev = jnp.where((blk == 0) | (blk == nlat), 0.0, 1.0)
    has_next = jnp.where((blk == nlat - 1) | (blk == nblk - 1), 0.0, 1.0)
    cols = []
    for j in range(RG_LANES):
        ext = jnp.concatenate([prev_ref[j] * has_prev, main_ref[j], next_ref[j] * has_next], axis=0)
        lane = slice(j * 128, (j + 1) * 128)
        acc = cb_ref[:, lane] + cw_ref[0:1, lane] * ext[0:rows]
        for k in range(1, RG_CONV):
            acc = acc + cw_ref[k:k + 1, lane] * ext[k * BATCH:k * BATCH + rows]
        cols.append(acc)
    xcb = jnp.concatenate(cols, axis=1).astype(BF16)

    log_a_unit = -RG_C * jax.nn.softplus(-lam_ref[...])
    for j in range(RG_LANES):
        lane = slice(j * 128, (j + 1) * 128)
        w0 = _rg_window_start(j) * 128
        pre = jnp.dot(xcb[:, w0:w0 + RG_WIN * 128], wg_ref[j], preferred_element_type=F32)
        r = 0.5 * (jnp.tanh(0.5 * (pre[:, :128] + ba_ref[:, lane])) + 1.0)
        g = 0.5 * (jnp.tanh(0.5 * (pre[:, 128:] + bx_ref[:, lane])) + 1.0)
        a = jnp.exp(log_a_unit[:, lane] * r)
        a_s[:, lane] = a
        u_s[:, lane] = jnp.sqrt(1.0 - a * a) * (g * cols[j])

    def step(t, carry):
        te = jnp.where(z == 0, t, RG_TBLK - 1 - t)
        r0 = pl.multiple_of(te * BATCH, BATCH)
        out = []
        for j in range(RG_LANES):
            lane = slice(j * 128, (j + 1) * 128)
            h = a_s[pl.ds(r0, BATCH), lane] * carry[j] + u_s[pl.ds(r0, BATCH), lane]
            ho_ref[j, pl.ds(r0, BATCH), :] = h
            out.append(h)
        return tuple(out)

    final = lax.fori_loop(0, RG_TBLK, step, tuple(st_ref[j] for j in range(RG_LANES)), unroll=4)
    for j in range(RG_LANES):
        st_ref[j] = final[j]


def rg_scan(rec_tm, conv_w, conv_b, w_gates, ga_b, gx_b, lam):
    nblk = L_JOINT // RG_TBLK
    rows = RG_TBLK * BATCH
    halo_prev = (RG_CONV // 2) * BATCH
    per_prev = rows // halo_prev
    fixed2 = lambda z, i: (0, 0)
    per_dir = lambda z, i: (z, 0, 0)
    return pl.pallas_call(
        _rg_scan_body,
        out_shape=jax.ShapeDtypeStruct((2, RG_LANES, L_JOINT * BATCH, 128), F32),
        grid=(2, nblk),
        in_specs=[pl.BlockSpec((RG_LANES, halo_prev, 128),
                               lambda z, i: (0, jnp.maximum(_rg_block(z, i) * per_prev - 1, 0), 0)),
                  pl.BlockSpec((RG_LANES, rows, 128), lambda z, i: (0, _rg_block(z, i), 0)),
                  pl.BlockSpec((RG_LANES, BATCH, 128),
                               lambda z, i: (0, jnp.minimum((_rg_block(z, i) + 1) * RG_TBLK, L_JOINT - 1), 0)),
                  pl.BlockSpec((RG_CONV, D_RNN), fixed2),
                  pl.BlockSpec((1, D_RNN), fixed2),
                  pl.BlockSpec((None, RG_LANES, RG_WIN * 128, 256), lambda z, i: (z, 0, 0, 0)),
                  pl.BlockSpec((None, 1, D_RNN), per_dir),
                  pl.BlockSpec((None, 1, D_RNN), per_dir),
                  pl.BlockSpec((None, 1, D_RNN), per_dir)],
        out_specs=pl.BlockSpec((None, RG_LANES, rows, 128), lambda z, i: (z, 0, _rg_block(z, i), 0)),
        scratch_shapes=[pltpu.VMEM((rows, D_RNN), F32), pltpu.VMEM((rows, D_RNN), F32),
                        pltpu.VMEM((RG_LANES, BATCH, 128), F32)],
        compiler_params=_params("arbitrary", "arbitrary"),
        name="rg_scan",
    )(rec_tm, rec_tm, rec_tm, conv_w, conv_b, w_gates, ga_b, gx_b, lam)


def _rg_out_body(gate_ref, hf_ref, hb_ref, w_ref, h_ref, mod_ref, lng_ref, lnb_ref, rwh_ref, rwl_ref,
                 ho_ref, f_ref, lg_ref):
    tt = h_ref.shape[1]
    parts = []
    for b in range(BATCH):
        hsum = jnp.concatenate([hf_ref[j, pl.ds(b, tt, stride=BATCH), :] + hb_ref[j, pl.ds(b, tt, stride=BATCH), :]
                                for j in range(RG_LANES)], axis=1)
        parts.append((gate_ref[b].astype(F32) * hsum).astype(BF16))
    zz = jnp.concatenate(parts, axis=0)
    y = jnp.dot(zz, w_ref[...], preferred_element_type=F32)
    h = h_ref[...].reshape(BATCH * tt, D_MODEL)
    rep = lambda m: jnp.broadcast_to(mod_ref[m][:, None, :], (BATCH, tt, D_MODEL)).reshape(BATCH * tt, D_MODEL)
    hn, fpk, lg = _post_epilogue(y, h, rep(MOD_GT), rep(MOD_SC2), rep(MOD_SH2), lng_ref[...], lnb_ref[...],
                                 rwh_ref[...], rwl_ref[...])
    ho_ref[...] = hn.reshape(BATCH, tt, D_MODEL)
    f_ref[...] = fpk.reshape(BATCH, tt, D_MODEL // 2)
    lg_ref[...] = lg.reshape(BATCH, tt, ROUTER_PAD)


def rg_out(gate, h_tm, w_out, h3, modtab_t, ln_g, ln_b, rw_hi, rw_lo):
    nblk = L_JOINT // RG_TBLK
    rows = RG_TBLK * BATCH
    blk3 = lambda i: (0, i, 0)
    fixed = lambda i: (0, 0)
    return pl.pallas_call(
        _rg_out_body,
        out_shape=(jax.ShapeDtypeStruct((BATCH, L_JOINT, D_MODEL), F32),
                   jax.ShapeDtypeStruct((BATCH, L_JOINT, D_MODEL // 2), jnp.uint32),
                   jax.ShapeDtypeStruct((BATCH, L_JOINT, ROUTER_PAD), F32)),
        grid=(nblk,),
        in_specs=[pl.BlockSpec((BATCH, RG_TBLK, D_RNN), blk3),
                  pl.BlockSpec((None, RG_LANES, rows, 128), lambda i: (0, 0, i, 0)),
                  pl.BlockSpec((None, RG_LANES, rows, 128), lambda i: (1, 0, i, 0)),
                  pl.BlockSpec((D_RNN, D_MODEL), fixed),
                  pl.BlockSpec((BATCH, RG_TBLK, D_MODEL), blk3),
                  pl.BlockSpec((None, 6, BATCH, D_MODEL), lambda i: (i // (SEQ // RG_TBLK), 0, 0, 0)),
                  pl.BlockSpec((1, D_MODEL), fixed), pl.BlockSpec((1, D_MODEL), fixed),
                  pl.BlockSpec((D_MODEL, ROUTER_PAD), fixed), pl.BlockSpec((D_MODEL, ROUTER_PAD), fixed)],
        out_specs=(pl.BlockSpec((BATCH, RG_TBLK, D_MODEL), blk3),
                   pl.BlockSpec((BATCH, RG_TBLK, D_MODEL // 2), blk3),
                   pl.BlockSpec((BATCH, RG_TBLK, ROUTER_PAD), blk3)),
        compiler_params=_params("parallel"),
        name="rg_out",
    )(gate, h_tm, h_tm, w_out, h3, modtab_t, ln_g, ln_b, rw_hi, rw_lo)


def _split_dot(tri, x, lhs_tri):
    hi = x.astype(BF16)
    lo = (x - hi.astype(F32)).astype(BF16)
    if lhs_tri:
        return jnp.dot(tri, hi, preferred_element_type=F32) + jnp.dot(tri, lo, preferred_element_type=F32)
    return jnp.dot(hi, tri, preferred_element_type=F32) + jnp.dot(lo, tri, preferred_element_type=F32)


def _mlstm_body(q_ref, k_ref, v_ref, gc_ref, gr_ref, ng_ref, o_ref, hf_s, hb_s, ct_ref, n_ref, m_ref):
    row = lax.broadcasted_iota(jnp.int32, (CHUNK, CHUNK), 0)
    col = lax.broadcasted_iota(jnp.int32, (CHUNK, CHUNK), 1)
    lower = col <= row
    upper = col >= row
    tri_lower = jnp.where(lower, 1.0, 0.0).astype(BF16)
    tri_upper = jnp.where(upper, 1.0, 0.0).astype(BF16)
    ones_sq = jnp.ones((CHUNK, CHUNK), BF16)
    gate_row = lax.broadcasted_iota(jnp.int32, (4, CHUNK), 0)

    def _lane_bcast(x, j):
        e = jnp.where(gate_row == j, 1.0, 0.0).astype(BF16)
        hi = x.astype(BF16)
        lo = (x - hi.astype(F32)).astype(BF16)
        return jnp.dot(hi, e, preferred_element_type=F32) + jnp.dot(lo, e, preferred_element_type=F32)

    def chunk(c, z, hh):
        reverse = z == 1
        c0 = pl.multiple_of(c * CHUNK, CHUNK)
        q = q_ref[pl.ds(c0, CHUNK), hh * M_DK:(hh + 1) * M_DK]
        k = k_ref[pl.ds(c0, CHUNK), hh * M_DK:(hh + 1) * M_DK]
        v = v_ref[pl.ds(c0, CHUNK), hh * M_DV:(hh + 1) * M_DV]
        gc = gc_ref[hh, pl.ds(c0, CHUNK), :]
        gr = gr_ref[hh, :, pl.ds(c0, CHUNK)]
        tri_c, tri_r, mask = (tri_upper, tri_lower, upper) if reverse else (tri_lower, tri_upper, lower)
        cum_c = _split_dot(tri_c, gc, True)
        cum_r = _split_dot(tri_r, gr, False)
        ig_b = _lane_bcast(gc, 2 * z)
        bc_b = _lane_bcast(cum_c, 2 * z + 1)
        ig_r = gr[2 * z:2 * z + 1, :]
        bc_r = cum_r[2 * z + 1:2 * z + 2, :]
        btot = bc_b[0:1, :] if reverse else bc_b[CHUNK - 1:CHUNK, :]
        m_prev = m_ref[z, hh]
        n_prev = n_ref[z, hh]
        ct_prev = ct_ref[z, hh]
        two = lambda a: jnp.concatenate([a, a], axis=1)
        dlog = jnp.where(mask, bc_b - bc_r + ig_r, -jnp.inf)
        m_inter = bc_b + m_prev
        m_comb = jnp.maximum(m_inter, jnp.max(dlog, axis=1, keepdims=True))
        qk = lax.dot_general(q, k, (((1,), (1,)), ((), ())), preferred_element_type=F32)
        s = qk * jnp.exp(dlog - m_comb)
        inter = jnp.exp(m_inter - m_comb)
        sb = s.astype(BF16)
        num = (jnp.dot(sb, v, preferred_element_type=F32)
               + two(inter) * jnp.dot(q, ct_prev.astype(BF16), preferred_element_type=F32))
        s_sum = jnp.dot(sb, ones_sq, preferred_element_type=F32)
        n_rows = jnp.broadcast_to(n_prev, (CHUNK, M_DK)).astype(BF16)
        qn = lax.dot_general(q, n_rows, (((1,), (1,)), ((), ())), preferred_element_type=F32)
        den = s_sum + inter * qn
        h = num * two(1.0 / jnp.maximum(jnp.abs(den), jnp.exp(-m_comb)))
        wlog = btot - bc_b + ig_b
        mloc = jnp.max(wlog, axis=0, keepdims=True)
        wgt = jnp.exp(wlog - mloc)
        m_new = jnp.maximum(btot + m_prev, mloc)
        sp = jnp.exp(btot + m_prev - m_new)
        sl = jnp.exp(mloc - m_new)
        kf = k.astype(F32)
        vw = (v.astype(F32) * two(wgt)).astype(BF16)
        ct_loc = jnp.dot(kf.T.astype(BF16), vw, preferred_element_type=F32)
        ct_ref[z, hh] = two(sp) * ct_prev + two(sl) * ct_loc
        n_ref[z, hh] = sp * n_prev + sl * jnp.sum(kf * wgt, axis=0, keepdims=True)
        m_ref[z, hh] = m_new
        return c0, h

    ct_ref[...] = jnp.zeros_like(ct_ref)
    n_ref[...] = jnp.zeros_like(n_ref)
    m_ref[...] = jnp.zeros_like(m_ref)

    n_chunks = o_ref.shape[0] // CHUNK
    n_lat = SEQ // CHUNK

    def both_directions(i, carry):
        for hh in range(ML_HEADS_PER_STEP):
            c0, h = chunk((i + n_lat) % n_chunks, 0, hh)
            hf_s[hh, pl.ds(c0, CHUNK), :] = h
            c0, h = chunk(n_chunks - 1 - i, 1, hh)
            hb_s[hh, pl.ds(c0, CHUNK), :] = h
        return carry

    lax.fori_loop(0, n_chunks, both_directions, 0)

    def head_norm(c, carry):
        c0 = pl.multiple_of(c * CHUNK, CHUNK)
        for hh in range(ML_HEADS_PER_STEP):
            tot = hf_s[hh, pl.ds(c0, CHUNK), :] + hb_s[hh, pl.ds(c0, CHUNK), :]
            mu = jnp.mean(tot, axis=1, keepdims=True)
            var = jnp.mean(jnp.square(tot - mu), axis=1, keepdims=True)
            lanes = slice(hh * M_DV, (hh + 1) * M_DV)
            o_ref[pl.ds(c0, CHUNK), lanes] = ((tot - mu) * lax.rsqrt(var + LN_EPS) * ng_ref[:, lanes]
                                             ).astype(o_ref.dtype)
        return carry

    lax.fori_loop(0, n_chunks, head_norm, 0)


def mlstm_cell(q, k, v, g_col, g_row, norm_g):
    b, lj, _ = q.shape
    hp = ML_HEADS_PER_STEP
    return pl.pallas_call(
        _mlstm_body,
        out_shape=jax.ShapeDtypeStruct((b, lj, M_HEADS * M_DV), BF16),
        grid=(b, M_HEADS // hp),
        in_specs=[pl.BlockSpec((None, lj, hp * M_DK), lambda i, h: (i, 0, h)),
                  pl.BlockSpec((None, lj, hp * M_DK), lambda i, h: (i, 0, h)),
                  pl.BlockSpec((None, lj, hp * M_DV), lambda i, h: (i, 0, h)),
                  pl.BlockSpec((None, hp, lj, 4), lambda i, h: (i, h, 0, 0)),
                  pl.BlockSpec((None, hp, 4, lj), lambda i, h: (i, h, 0, 0)),
                  pl.BlockSpec((1, hp * M_DV), lambda i, h: (0, h))],
        out_specs=pl.BlockSpec((None, lj, hp * M_DV), lambda i, h: (i, 0, h)),
        scratch_shapes=[pltpu.VMEM((hp, lj, M_DV), F32), pltpu.VMEM((hp, lj, M_DV), F32),
                        pltpu.VMEM((2, hp, M_DK, M_DV), F32), pltpu.VMEM((2, hp, 1, M_DK), F32),
                        pltpu.VMEM((2, hp, 1, CHUNK), F32)],
        compiler_params=_params("parallel", "parallel"),
        name="mlstm_cell",
    )(q, k, v, g_col, g_row, norm_g)


def _tile_neighbours(i):
    r = i % TILES_PER_BATCH
    has_prev = jnp.where((r == 0) | (r == LATENT_TILES), 0.0, 1.0)
    has_next = jnp.where((r == LATENT_TILES - 1) | (r == TILES_PER_BATCH - 1), 0.0, 1.0)
    return has_prev, has_next


def _ml_up_body(h_ref, mod_ref, w_ref, o_ref):
    inp = h_ref[...] * (1.0 + mod_ref[MOD_SC:MOD_SC + 1, :]) + mod_ref[MOD_SH:MOD_SH + 1, :]
    o_ref[...] = jnp.dot(inp.astype(BF16), w_ref[...], preferred_element_type=F32).astype(o_ref.dtype)


def mod_proj(h, modtab, w, bias=None):
    t, d = h.shape
    n = w.shape[1]
    body = _ml_up_body
    args = [h, modtab, w]
    in_specs = [pl.BlockSpec((ROW_TILE, d), lambda i: (i, 0)),
                pl.BlockSpec((None, 6, d), _mod_index),
                pl.BlockSpec((d, n), lambda i: (0, 0))]
    if bias is not None:
        def body(h_ref, mod_ref, w_ref, b_ref, o_ref):
            inp = h_ref[...] * (1.0 + mod_ref[MOD_SC:MOD_SC + 1, :]) + mod_ref[MOD_SH:MOD_SH + 1, :]
            y = jnp.dot(inp.astype(BF16), w_ref[...], preferred_element_type=F32) + b_ref[...]
            o_ref[...] = y.astype(o_ref.dtype)
        args.append(bias)
        in_specs.append(pl.BlockSpec((1, n), lambda i: (0, 0)))
    return pl.pallas_call(
        body,
        out_shape=jax.ShapeDtypeStruct((t, n), BF16),
        grid=(t // ROW_TILE,),
        in_specs=in_specs,
        out_specs=pl.BlockSpec((ROW_TILE, n), lambda i: (i, 0)),
        compiler_params=_params("parallel"),
        name="mod_proj",
    )(*args)


def _ml_qk_body(prev_ref, main_ref, next_ref, cw_ref, cb_ref, w_ref, xc_ref, q_ref, k_ref, ext_s):
    has_prev, has_next = _tile_neighbours(pl.program_id(0))
    ext_s[0:CONV_HALO, :] = prev_ref[...].astype(F32) * has_prev
    ext_s[CONV_HALO:CONV_HALO + ROW_TILE, :] = main_ref[...].astype(F32)
    ext_s[CONV_HALO + ROW_TILE:, :] = next_ref[...].astype(F32) * has_next
    kk = cw_ref.shape[0]
    acc = cb_ref[...] + cw_ref[0:1, :] * ext_s[pl.ds(CONV_HALO - kk // 2, ROW_TILE), :]
    for j in range(1, kk):
        acc = acc + cw_ref[j:j + 1, :] * ext_s[pl.ds(CONV_HALO - kk // 2 + j, ROW_TILE), :]
    xc = (acc * jax.nn.sigmoid(acc)).astype(BF16)
    xc_ref[...] = xc
    qk = jnp.dot(xc, w_ref[...], preferred_element_type=F32)
    nq = q_ref.shape[1]
    q_ref[...] = qk[:, :nq].astype(BF16)
    k_ref[...] = (qk[:, nq:] * (M_DK ** -0.5)).astype(BF16)


def ml_qk(xm, conv_w, conv_b, w_qk):
    t, dm = xm.shape
    nq = M_HEADS * M_DK
    per = ROW_TILE // CONV_HALO
    last = t // CONV_HALO - 1
    row = lambda i: (i, 0)
    fixed = lambda i: (0, 0)
    return pl.pallas_call(
        _ml_qk_body,
        out_shape=(jax.ShapeDtypeStruct((t, dm), BF16), jax.ShapeDtypeStruct((t, nq), BF16),
                   jax.ShapeDtypeStruct((t, nq), BF16)),
        grid=(t // ROW_TILE,),
        in_specs=[pl.BlockSpec((CONV_HALO, dm), lambda i: (jnp.maximum(i * per - 1, 0), 0)),
                  pl.BlockSpec((ROW_TILE, dm), row),
                  pl.BlockSpec((CONV_HALO, dm), lambda i: (jnp.minimum((i + 1) * per, last), 0)),
                  pl.BlockSpec(conv_w.shape, fixed), pl.BlockSpec((1, dm), fixed),
                  pl.BlockSpec((dm, 2 * nq), fixed)],
        out_specs=(pl.BlockSpec((ROW_TILE, dm), row), pl.BlockSpec((ROW_TILE, nq), row),
                   pl.BlockSpec((ROW_TILE, nq), row)),
        scratch_shapes=[pltpu.VMEM((ROW_TILE + 2 * CONV_HALO, dm), F32)],
        compiler_params=_params("parallel"),
        name="ml_qk",
    )(xm, xm, xm, conv_w, conv_b, w_qk)


def _mm_act_body(x_ref, w_ref, o_ref, *, act):
    y = jnp.dot(x_ref[...], w_ref[...], preferred_element_type=F32)
    if act == "sigmoid":
        y = jax.nn.sigmoid(y)
    o_ref[...] = y.astype(o_ref.dtype)


def mm_act(x, w, act=None, *, tm=512, tn=1024):
    m, k = x.shape
    n = w.shape[1]
    return pl.pallas_call(
        functools.partial(_mm_act_body, act=act),
        out_shape=jax.ShapeDtypeStruct((m, n), BF16),
        grid=(n // tn, m // tm),
        in_specs=[pl.BlockSpec((tm, k), lambda j, i: (i, 0)),
                  pl.BlockSpec((k, tn), lambda j, i: (0, j))],
        out_specs=pl.BlockSpec((tm, tn), lambda j, i: (i, j)),
        compiler_params=_params("parallel", "parallel"),
        name="mm_act",
    )(x, w)


def _ml_gates_body(x_ref, w_ref, b_ref, o_ref):
    g = jnp.dot(x_ref[...], w_ref[...], preferred_element_type=F32) + b_ref[...]
    lane = lax.broadcasted_iota(jnp.int32, g.shape, 1)
    is_forget = (lane % (2 * M_HEADS)) >= M_HEADS
    o_ref[...] = jnp.where(is_forget, jax.nn.log_sigmoid(g), g)


def ml_gates(xm, w_g, b_g):
    t, dm = xm.shape
    return pl.pallas_call(
        _ml_gates_body,
        out_shape=jax.ShapeDtypeStruct((t, 128), F32),
        grid=(t // 512,),
        in_specs=[pl.BlockSpec((512, dm), lambda i: (i, 0)), pl.BlockSpec((dm, 128), lambda i: (0, 0)),
                  pl.BlockSpec((1, 128), lambda i: (0, 0))],
        out_specs=pl.BlockSpec((512, 128), lambda i: (i, 0)),
        compiler_params=_params("parallel"),
        name="ml_gates",
    )(xm, w_g, b_g)


def _post_ml_body(o_ref, hn_ref, xc_ref, skip_ref, w_ref, h_ref, mod_ref, lng_ref, lnb_ref, rwh_ref, rwl_ref,
                  ho_ref, f_ref, lg_ref):
    z = o_ref[...].astype(F32) * hn_ref[...].astype(F32) + skip_ref[...] * xc_ref[...].astype(F32)
    y = jnp.dot(z.astype(BF16), w_ref[...], preferred_element_type=F32)
    hn, fpk, lg = _post_epilogue(y, h_ref[...], mod_ref[MOD_GT:MOD_GT + 1, :], mod_ref[MOD_SC2:MOD_SC2 + 1, :],
                                 mod_ref[MOD_SH2:MOD_SH2 + 1, :], lng_ref[...], lnb_ref[...], rwh_ref[...],
                                 rwl_ref[...])
    ho_ref[...] = hn
    f_ref[...] = fpk
    lg_ref[...] = lg


def post_mixer_ml(o, hn, xc, skip, w_out, h, modtab, ln_g, ln_b, rw_hi, rw_lo):
    t, k = o.shape
    d = w_out.shape[1]
    row = lambda i: (i, 0)
    fixed = lambda i: (0, 0)
    return pl.pallas_call(
        _post_ml_body,
        out_shape=(jax.ShapeDtypeStruct((t, d), F32), jax.ShapeDtypeStruct((t, d // 2), jnp.uint32),
                   jax.ShapeDtypeStruct((t, ROUTER_PAD), F32)),
        grid=(t // ROW_TILE,),
        in_specs=[pl.BlockSpec((ROW_TILE, k), row), pl.BlockSpec((ROW_TILE, k), row),
                  pl.BlockSpec((ROW_TILE, k), row), pl.BlockSpec((1, k), fixed),
                  pl.BlockSpec((k, d), fixed),
                  pl.BlockSpec((ROW_TILE, d), row),
                  pl.BlockSpec((None, 6, d), _mod_index),
                  pl.BlockSpec((1, d), fixed), pl.BlockSpec((1, d), fixed),
                  pl.BlockSpec((d, ROUTER_PAD), fixed), pl.BlockSpec((d, ROUTER_PAD), fixed)],
        out_specs=(pl.BlockSpec((ROW_TILE, d), row), pl.BlockSpec((ROW_TILE, d // 2), row),
                   pl.BlockSpec((ROW_TILE, ROUTER_PAD), row)),
        compiler_params=_params("parallel"),
        name="post_mixer_ml",
    )(o, hn, xc, skip, w_out, h, modtab, ln_g, ln_b, rw_hi, rw_lo)


def _hy_conv(u_ref, cw_ref, cb_ref, ext_s):
    n_time = u_ref.shape[0]
    ext_s[0:8, :] = jnp.zeros((8, ext_s.shape[1]), F32)
    ext_s[8:8 + n_time, :] = u_ref[...].astype(F32)
    ext_s[8 + n_time:, :] = jnp.zeros((8, ext_s.shape[1]), F32)
    kk = cw_ref.shape[0]
    acc = cb_ref[...] + cw_ref[0:1, :] * ext_s[pl.ds(8 - kk // 2, n_time), :]
    for j in range(1, kk):
        acc = acc + cw_ref[j:j + 1, :] * ext_s[pl.ds(8 - kk // 2 + j, n_time), :]
    return acc


def _hy_spectrum(fm_ref, sig, hf_ref, y_ref):
    x = jnp.dot(fm_ref[...], sig, preferred_element_type=F32)
    half = x.shape[0] // 2
    xt, xb = x[:half], x[half:]
    ht, hb = hf_ref[0:half, :], hf_ref[half:, :]
    first = lax.broadcasted_iota(jnp.int32, xt.shape, 0) == 0
    y_ref[0:half, :] = jnp.where(first, xt * ht, xt * ht - xb * hb).astype(y_ref.dtype)
    y_ref[half:, :] = jnp.where(first, xb * hb, xt * hb + xb * ht).astype(y_ref.dtype)


def _hy_fwd_conv_body(u_ref, cw_ref, cb_ref, fm_ref, hf_ref, v_ref, y_ref, ext_s):
    v = _hy_conv(u_ref, cw_ref, cb_ref, ext_s).astype(BF16)
    v_ref[...] = v
    _hy_spectrum(fm_ref, v, hf_ref, y_ref)


def _hy_fwd_body(s_ref, fm_ref, hf_ref, y_ref):
    _hy_spectrum(fm_ref, s_ref[...], hf_ref, y_ref)


def _hy_inv_body(y_ref, gm_ref, u_ref, cw_ref, cb_ref, s_ref, skip_ref, z_ref, ext_s):
    y = jnp.dot(gm_ref[...], y_ref[...], preferred_element_type=F32)
    g = _hy_conv(u_ref, cw_ref, cb_ref, ext_s)
    z_ref[...] = (g * (y + s_ref[...].astype(F32) * skip_ref[...])).astype(z_ref.dtype)


def hyena_segment(u3, seg_block, n_time, conv_w, conv_b, hf, skip):
    b = u3.shape[0]
    tn = HY_TN
    nj = D_MODEL // tn
    fm, gm = _dft_mats(n_time)
    n = fm.shape[0]
    grid = (nj, b)
    u_spec = lambda part: pl.BlockSpec((None, n_time, tn), lambda j, i: (i, seg_block, part * nj + j))
    cw_spec = lambda part: pl.BlockSpec((conv_w.shape[0], tn), lambda j, i: (0, part * nj + j))
    cb_spec = lambda part: pl.BlockSpec((1, tn), lambda j, i: (0, part * nj + j))
    fixed = lambda j, i: (0, 0)
    sig_spec = pl.BlockSpec((None, n_time, tn), lambda j, i: (i, 0, j))
    spec_spec = pl.BlockSpec((None, n, tn), lambda j, i: (i, 0, j))
    hf_spec = lambda c: pl.BlockSpec((n, tn), lambda j, i: (0, c * nj + j))
    skip_spec = lambda c: pl.BlockSpec((None, 1, tn), lambda j, i: (c, 0, j))
    ext = pltpu.VMEM((n_time + 16, tn), F32)
    sig_shape = jax.ShapeDtypeStruct((b, n_time, D_MODEL), BF16)
    spec_shape = jax.ShapeDtypeStruct((b, n, D_MODEL), BF16)
    par = _params("parallel", "parallel")
    skip3 = skip[:, None, :]

    v, y1 = pl.pallas_call(
        _hy_fwd_conv_body, out_shape=(sig_shape, spec_shape), grid=grid,
        in_specs=[u_spec(0), cw_spec(0), cb_spec(0), pl.BlockSpec(fm.shape, fixed), hf_spec(0)],
        out_specs=(sig_spec, spec_spec), scratch_shapes=[ext], compiler_params=par, name="hy_fwd_conv",
    )(u3, conv_w, conv_b, fm, hf)
    z1 = pl.pallas_call(
        _hy_inv_body, out_shape=sig_shape, grid=grid,
        in_specs=[spec_spec, pl.BlockSpec(gm.shape, fixed), u_spec(1), cw_spec(1), cb_spec(1), sig_spec,
                  skip_spec(0)],
        out_specs=sig_spec, scratch_shapes=[ext], compiler_params=par, name="hy_inv",
    )(y1, gm, u3, conv_w, conv_b, v, skip3)
    y2 = pl.pallas_call(
        _hy_fwd_body, out_shape=spec_shape, grid=grid,
        in_specs=[sig_spec, pl.BlockSpec(fm.shape, fixed), hf_spec(1)],
        out_specs=spec_spec, compiler_params=par, name="hy_fwd",
    )(z1, fm, hf)
    return pl.pallas_call(
        _hy_inv_body, out_shape=sig_shape, grid=grid,
        in_specs=[spec_spec, pl.BlockSpec(gm.shape, fixed), u_spec(2), cw_spec(2), cb_spec(2), sig_spec,
                  skip_spec(1)],
        out_specs=sig_spec, scratch_shapes=[ext], compiler_params=par, name="hy_inv",
    )(y2, gm, u3, conv_w, conv_b, z1, skip3)


def _moe_body(te_ref, tf_ref, nu_ref, x_ref, wg_ref, wu_ref, wd_ref, o_ref, wg_s, wu_s, wd_s):
    i = pl.program_id(0)

    @pl.when(i < nu_ref[0])
    def _():
        @pl.when(tf_ref[i] == 1)
        def _():
            wg_s[...] = wg_ref[...].astype(BF16)
            wu_s[...] = wu_ref[...].astype(BF16)
            wd_s[...] = wd_ref[...].astype(BF16)

        x = _unpack_bf16_pairs(x_ref[...])
        a = jnp.dot(x, wg_s[...], preferred_element_type=F32)
        u = jnp.dot(x, wu_s[...], preferred_element_type=F32)
        hid = (a * jax.nn.sigmoid(a)) * u
        o_ref[...] = _pack_bf16_pairs(jnp.dot(hid.astype(BF16), wd_s[...], preferred_element_type=F32))

    @pl.when(i >= nu_ref[0])
    def _():
        o_ref[...] = jnp.zeros_like(o_ref)


def moe_experts(layer, tile_expert, tile_first, n_used, xs, w_gate, w_up, w_down):
    n_rows = xs.shape[0]
    d = 2 * xs.shape[1]
    n_tiles = n_rows // MOE_TILE
    wmap_in = lambda i, te, tf, nu: (layer, te[i], 0, 0)
    grid_spec = pltpu.PrefetchScalarGridSpec(
        num_scalar_prefetch=3,
        grid=(n_tiles,),
        in_specs=[pl.BlockSpec((MOE_TILE, d // 2), lambda i, te, tf, nu: (i, 0)),
                  pl.BlockSpec((None, None, d, D_EXPERT), wmap_in),
                  pl.BlockSpec((None, None, d, D_EXPERT), wmap_in),
                  pl.BlockSpec((None, None, D_EXPERT, d), wmap_in)],
        out_specs=pl.BlockSpec((MOE_TILE, d // 2), lambda i, te, tf, nu: (i, 0)),
        scratch_shapes=[pltpu.VMEM((d, D_EXPERT), BF16), pltpu.VMEM((d, D_EXPERT), BF16),
                        pltpu.VMEM((D_EXPERT, d), BF16)],
    )
    return pl.pallas_call(
        _moe_body,
        out_shape=jax.ShapeDtypeStruct((n_rows, d // 2), jnp.uint32),
        grid_spec=grid_spec,
        compiler_params=_params("arbitrary"),
        name="moe_experts",
    )(tile_expert, tile_first, n_used, xs, w_gate, w_up, w_down)


def _first_max4(a):
    m = jnp.maximum(jnp.maximum(a[0], a[1]), jnp.maximum(a[2], a[3]))
    idx = jnp.where(a[0] == m, 0, jnp.where(a[1] == m, 1, jnp.where(a[2] == m, 2, 3))).astype(jnp.int32)
    return m, idx


def _router_body(lg_ref, rb_ref, dest_ref, wts_ref, te_ref, tf_ref, nu_ref, lgt_s, cnt_s, pre_s):
    t = lg_ref.shape[0]
    n_chunk = t // 128

    def transpose(c, carry):
        c0 = pl.multiple_of(c * 128, 128)
        lgt_s[:, pl.ds(c0, 128)] = lg_ref[pl.ds(c0, 128), :].T[0:N_EXPERTS, :]
        return carry

    lax.fori_loop(0, n_chunk, transpose, 0)

    s = jax.nn.sigmoid(lgt_s[...])
    sel = s + rb_ref[...]
    gs, i1s, i2s = [], [], []
    for g in range(N_GROUPS):
        a = [sel[g * EXP_PER_GROUP + j:g * EXP_PER_GROUP + j + 1, :] for j in range(EXP_PER_GROUP)]
        m1, i1 = _first_max4(a)
        m2, i2 = _first_max4([jnp.where(i1 == j, -jnp.inf, a[j]) for j in range(EXP_PER_GROUP)])
        gs.append(m1 + m2)
        i1s.append(i1)
        i2s.append(i2)
    _, gb = _first_max4(gs)
    pick = lambda v: jnp.where(gb == 0, v[0], jnp.where(gb == 1, v[1], jnp.where(gb == 2, v[2], v[3])))
    e1 = gb * EXP_PER_GROUP + pick(i1s)
    e2 = gb * EXP_PER_GROUP + pick(i2s)
    eid = lax.broadcasted_iota(jnp.int32, (N_EXPERTS, t), 0)
    is1 = eid == e1
    is2 = eid == e2
    w1 = jnp.sum(jnp.where(is1, s, 0.0), axis=0, keepdims=True)
    w2 = jnp.sum(jnp.where(is2, s, 0.0), axis=0, keepdims=True)
    tot = w1 + w2
    wts_ref[...] = jnp.concatenate([w1 / tot, w2 / tot, jnp.zeros((6, t), F32)], axis=0)

    cnt_s[...] = jnp.where(is1 | is2, 1.0, 0.0).astype(BF16)
    r = lax.broadcasted_iota(jnp.int32, (128, 128), 0)
    c = lax.broadcasted_iota(jnp.int32, (128, 128), 1)
    before = jnp.where(r < c, 1.0, 0.0).astype(BF16)
    ones = jnp.ones((128, 128), BF16)

    def prefix(ci, carry):
        c0 = pl.multiple_of(ci * 128, 128)
        blk = cnt_s[:, pl.ds(c0, 128)]
        pre_s[:, pl.ds(c0, 128)] = carry + jnp.dot(blk, before, preferred_element_type=F32)
        return carry + jnp.dot(blk, ones, preferred_element_type=F32)

    counts = lax.fori_loop(0, n_chunk, prefix, jnp.zeros((N_EXPERTS, 128), F32))
    tiles_per = jnp.floor((counts + (MOE_TILE - 1)) * (1.0 / MOE_TILE))
    er = lax.broadcasted_iota(jnp.int32, (N_EXPERTS, N_EXPERTS), 0)
    ec = lax.broadcasted_iota(jnp.int32, (N_EXPERTS, N_EXPERTS), 1)
    earlier = jnp.where(ec < er, 1.0, 0.0).astype(BF16)
    tile_start = jnp.dot(earlier, tiles_per.astype(BF16), preferred_element_type=F32)
    tile_end = tile_start + tiles_per
    slot = tile_start[:, 0:1] * MOE_TILE + pre_s[...]
    d1 = jnp.sum(jnp.where(is1, slot, 0.0), axis=0, keepdims=True)
    d2 = jnp.sum(jnp.where(is2, slot, 0.0), axis=0, keepdims=True)
    dest_ref[...] = jnp.concatenate([d1, d2, jnp.zeros((6, t), F32)], axis=0).astype(jnp.int32)

    n_used = tile_end[N_EXPERTS - 1:N_EXPERTS, 0:1]
    tid = lax.broadcasted_iota(jnp.int32, (N_EXPERTS, te_ref.shape[1]), 1).astype(F32)
    tid_c = jnp.minimum(tid, n_used - 1.0)
    t_exp = jnp.sum(jnp.where(tile_end[:, 0:1] <= tid_c, 1.0, 0.0), axis=0, keepdims=True)
    t_exp = jnp.minimum(t_exp, N_EXPERTS - 1.0)
    t_first = jnp.sum(jnp.where((tile_start[:, 0:1] == tid) & (tiles_per[:, 0:1] > 0.0), 1.0, 0.0),
                      axis=0, keepdims=True)
    te_ref[...] = jnp.broadcast_to(t_exp, te_ref.shape).astype(jnp.int32)
    tf_ref[...] = jnp.broadcast_to(t_first, tf_ref.shape).astype(jnp.int32)
    nu_ref[...] = jnp.broadcast_to(n_used, nu_ref.shape).astype(jnp.int32)


def moe_router(logits, router_b):
    t = logits.shape[0]
    n_tab = 256
    assert t * TOP_K // MOE_TILE + N_EXPERTS <= n_tab
    full = lambda shape: pl.BlockSpec(shape, lambda i: (0, 0))
    dest, wts, te, tf, nu = pl.pallas_call(
        _router_body,
        out_shape=(jax.ShapeDtypeStruct((8, t), jnp.int32), jax.ShapeDtypeStruct((8, t), F32),
                   jax.ShapeDtypeStruct((8, n_tab), jnp.int32), jax.ShapeDtypeStruct((8, n_tab), jnp.int32),
                   jax.ShapeDtypeStruct((8, 128), jnp.int32)),
        grid=(1,),
        in_specs=[full((t, ROUTER_PAD)), full((N_EXPERTS, 1))],
        out_specs=(full((8, t)), full((8, t)), full((8, n_tab)), full((8, n_tab)), full((8, 128))),
        scratch_shapes=[pltpu.VMEM((N_EXPERTS, t), F32), pltpu.VMEM((N_EXPERTS, t), BF16),
                        pltpu.VMEM((N_EXPERTS, t), F32)],
        compiler_params=_params("arbitrary"),
        name="moe_router",
    )(logits, router_b[:, None])
    return dest[:TOP_K], wts[:TOP_K], te[0], tf[0], nu[0, :1]


def moe(layer, f, logits, router_b, w_gate, w_up, w_down):
    t = f.shape[0]
    n_tiles = t * TOP_K // MOE_TILE + N_EXPERTS
    n_rows = n_tiles * MOE_TILE
    dest, wts, tile_expert, tile_first, n_used = moe_router(logits, router_b)
    token = jnp.arange(t, dtype=jnp.int32)
    row_token = jnp.zeros((n_rows,), jnp.int32).at[dest.reshape(-1)].set(
        jnp.concatenate([token, token]), unique_indices=True)
    xs = jnp.take(f, row_token, axis=0)
    ys = moe_experts(layer, tile_expert[:n_tiles], tile_first[:n_tiles], n_used, xs, w_gate, w_up, w_down)
    return jnp.take(ys, dest[0], axis=0), jnp.take(ys, dest[1], axis=0), wts.T


def sincos_2d(n_tok):
    rows = n_tok // GRID_W
    quarter = D_MODEL // 4
    omega = 1.0 / (10000.0 ** (jnp.arange(quarter, dtype=F32) / quarter))
    ar = jnp.arange(rows, dtype=F32)[:, None] * omega
    ac = jnp.arange(GRID_W, dtype=F32)[:, None] * omega
    er = jnp.concatenate([jnp.sin(ar), jnp.cos(ar)], -1)
    ec = jnp.concatenate([jnp.sin(ac), jnp.cos(ac)], -1)
    half = D_MODEL // 2
    pos = jnp.concatenate([jnp.broadcast_to(er[:, None], (rows, GRID_W, half)),
                           jnp.broadcast_to(ec[None], (rows, GRID_W, half))], -1)
    return pos.reshape(rows * GRID_W, D_MODEL)


def rglru_layer(h3, modtab_t, w_in, conv_w, conv_b, ga_w, ga_b, gx_w, gx_b, lam, w_out, ln_g, ln_b, rw_hi, rw_lo):
    def block_diag(w):
        eye = jnp.eye(RG_BLOCKS, dtype=w.dtype)
        return jnp.einsum("nkj,nm->nkmj", w, eye).reshape(D_RNN, D_RNN)

    def banded(wa, wx):
        out = []
        for jj in range(RG_LANES):
            r0 = _rg_window_start(jj) * 128
            cs = slice(jj * 128, (jj + 1) * 128)
            out.append(jnp.concatenate([wa[r0:r0 + RG_WIN * 128, cs], wx[r0:r0 + RG_WIN * 128, cs]], axis=1))
        return jnp.stack(out)

    w_gates = jnp.stack([banded(block_diag(ga_w[z]), block_diag(gx_w[z]))
                         for z in range(2)]).astype(BF16)
    gate, rec_tm = rg_in(h3, modtab_t, w_in.astype(BF16))
    h_tm = rg_scan(rec_tm, conv_w, conv_b[None], w_gates, ga_b[:, None], gx_b[:, None], lam[:, None])
    hn, fpk, lg = rg_out(gate, h_tm, w_out.astype(BF16), h3, modtab_t, ln_g, ln_b, rw_hi, rw_lo)
    rows = BATCH * L_JOINT
    return hn.reshape(rows, D_MODEL), fpk.reshape(rows, D_MODEL // 2), lg.reshape(rows, ROUTER_PAD)


def mlstm_layer(hj, modtab, w_up, conv_w, conv_b, w_q, w_k, w_v, w_o, w_if, b_if, norm_g, skip, w_down,
                ln_g, ln_b, rw_hi, rw_lo):
    rows = hj.shape[0]
    xm = mod_proj(hj, modtab, w_up.astype(BF16))
    xc, q, k = ml_qk(xm, conv_w, conv_b[None], jnp.concatenate([w_q, w_k], axis=1).astype(BF16))
    v = mm_act(xm, w_v.astype(BF16))
    o = mm_act(xm, w_o.astype(BF16), "sigmoid")
    n_gate = 4 * M_HEADS
    w_g = jnp.concatenate([w_if[0], w_if[1], jnp.zeros((D_M, 128 - n_gate), F32)], axis=1).astype(BF16)
    b_g = jnp.concatenate([b_if[0], b_if[1], jnp.zeros((128 - n_gate,), F32)])[None]
    g = ml_gates(xm, w_g, b_g)[:, :n_gate].reshape(BATCH, L_JOINT, 2, 2, M_HEADS)
    g4 = jnp.transpose(g, (0, 1, 4, 2, 3)).reshape(BATCH, L_JOINT, M_HEADS, 4)
    g_col = jnp.transpose(g4, (0, 2, 1, 3))
    g_row = jnp.transpose(g4, (0, 2, 3, 1))
    r3 = lambda a: a.reshape(BATCH, L_JOINT, a.shape[-1])
    hn = mlstm_cell(r3(q), r3(k), r3(v), g_col, g_row, norm_g[None]).reshape(rows, D_M)
    return post_mixer_ml(o, hn, xc, skip[None], w_down.astype(BF16), hj, modtab, ln_g, ln_b, rw_hi, rw_lo)


def _dft_mats(n_time):
    n = 3 * n_time // 2
    half = n // 2
    kk = np.arange(half, dtype=np.int64)[:, None]
    tt = np.arange(n_time, dtype=np.int64)[None, :]
    ang = 2.0 * np.pi * ((kk * tt) % n).astype(np.float64) / n
    top = np.cos(ang)
    bot = -np.sin(ang)
    bot[0] = np.cos(np.pi * tt[0])
    fwd = np.concatenate([top, bot], axis=0)
    mm_ = (np.arange(n_time, dtype=np.int64) + n_time // 2)[:, None]
    ang2 = 2.0 * np.pi * ((mm_ * kk.T) % n).astype(np.float64) / n
    wk = np.full((1, half), 2.0)
    wk[0, 0] = 1.0
    itop = wk * np.cos(ang2) / n
    ibot = -2.0 * np.sin(ang2) / n
    ibot[:, 0] = np.cos(np.pi * mm_[:, 0]) / n
    inv = np.concatenate([itop, ibot], axis=1)
    return jnp.asarray(fwd, dtype=BF16), jnp.asarray(inv, dtype=BF16)


def hyena_filters(n_time, w1, b1, fq1, w2, b2, fq2, w3):
    hp = lax.Precision.HIGHEST
    t01 = jnp.linspace(0.0, 1.0, n_time, dtype=F32)
    bands = jnp.linspace(1e-4, H_BANDS - 1, H_BANDS, dtype=F32)
    ang = (2.0 * math.pi / n_time) * jnp.arange(n_time, dtype=F32)[:, None] * bands[None, :]
    z = jnp.concatenate([t01[:, None], jnp.cos(ang), -jnp.sin(ang)], -1)
    hdn = jnp.sin(fq1 * (jnp.dot(z, w1, precision=hp) + b1))
    hdn = jnp.sin(fq2 * (jnp.dot(hdn, w2, precision=hp) + b2))
    filt = jnp.dot(hdn, w3, precision=hp).reshape(n_time, 2, D_MODEL)
    dist = jnp.abs(jnp.arange(n_time) - n_time // 2).astype(F32) * (2.0 / n_time)
    d_max = math.log(H_DECAY_TARGET) / H_FAST
    d_min = math.log(H_DECAY_TARGET) / H_SLOW
    deltas = jnp.abs(jnp.linspace(d_min, d_max, D_MODEL, dtype=F32))
    window = jnp.exp(-dist[:, None] * deltas[None, :])
    return filt * window[:, None, :]


def hyena_layer(hj, modtab, w_in, b_in, conv_w, conv_b, fparams, skip, w_out, ln_g, ln_b, rw_hi, rw_lo):
    rows = hj.shape[0]
    u3 = mod_proj(hj, modtab, w_in.astype(BF16), b_in[None]).reshape(BATCH, L_JOINT, 3 * D_MODEL)
    parts = []
    for seg_block, n_time in ((0, SEQ), (SEQ // CTX_LEN, CTX_LEN)):
        filt = hyena_filters(n_time, *fparams)
        fm, _ = _dft_mats(n_time)
        hf = bmm_left(fm, filt.reshape(1, n_time, 2 * D_MODEL))[0]
        parts.append(hyena_segment(u3, seg_block, n_time, conv_w, conv_b[None], hf, skip))
    z = jnp.concatenate(parts, axis=1).reshape(rows, D_MODEL)
    return post_mixer(z, w_out.astype(BF16), hj, modtab, ln_g, ln_b, rw_hi, rw_lo)


def kernel(x, c, ctx, c_ctx, router_w, router_b, ada_w, ada_b, ln_g, ln_b, moe_w_gate, moe_w_up, moe_w_down, rg_w_in, rg_conv_w, rg_conv_b, rg_gate_a_w, rg_gate_a_b, rg_gate_x_w, rg_gate_x_b, rg_lambda, rg_w_out, ml_w_up, ml_conv_w, ml_conv_b, ml_w_q, ml_w_k, ml_w_v, ml_w_o, ml_w_if, ml_b_if, ml_norm_g, ml_skip, ml_w_down, hy_w_in, hy_b_in, hy_conv_w, hy_conv_b, hy_f_w1, hy_f_b1, hy_f_freq1, hy_f_w2, hy_f_b2, hy_f_freq2, hy_f_w3, hy_skip, hy_w_out):
    bsz = x.shape[0]
    rows = bsz * L_JOINT
    hx = x + sincos_2d(SEQ)[None]
    hj = jnp.concatenate([hx, ctx], axis=1).reshape(rows, D_MODEL)
    is_ctx = (jnp.arange(L_JOINT) >= SEQ)[None, :, None]

    cond = jnp.concatenate([jax.nn.silu(c), jax.nn.silu(c_ctx)[None],
                            jnp.zeros((16 - bsz - 1, D_MODEL), F32)], axis=0)
    rw_pad = jnp.concatenate([router_w, jnp.zeros((D_MODEL, ROUTER_PAD - N_EXPERTS), F32)], axis=1)
    rw_hi = rw_pad.astype(BF16)
    rw_lo = (rw_pad - rw_hi.astype(F32)).astype(BF16)

    for i in range(DEPTH):
        kind, j = i % N_MIXERS, i // N_MIXERS
        mod = (mm(cond, ada_w[i].astype(BF16), tm=16, tn=D_MODEL) + ada_b[i]).reshape(16, 6, D_MODEL)
        mod_x = mod[:bsz]
        mod_c = jnp.broadcast_to(mod[bsz][None], (bsz, 6, D_MODEL))
        modtab = jnp.stack([mod_x, mod_c], axis=1).reshape(2 * bsz, 6, D_MODEL)
        lng, lnb = ln_g[i, 0][None], ln_b[i, 0][None]
        if kind == 0:
            modtab_t = jnp.stack([jnp.transpose(mod_x, (1, 0, 2)), jnp.transpose(mod_c, (1, 0, 2))])
            hj, f, logits = rglru_layer(hj.reshape(bsz, L_JOINT, D_MODEL), modtab_t, rg_w_in[j], rg_conv_w[j],
                                        rg_conv_b[j], rg_gate_a_w[j], rg_gate_a_b[j], rg_gate_x_w[j],
                                        rg_gate_x_b[j], rg_lambda[j], rg_w_out[j], lng, lnb, rw_hi, rw_lo)
        elif kind == 1:
            hj, f, logits = mlstm_layer(hj, modtab, ml_w_up[j], ml_conv_w[j], ml_conv_b[j], ml_w_q[j], ml_w_k[j],
                                        ml_w_v[j], ml_w_o[j], ml_w_if[j], ml_b_if[j], ml_norm_g[j], ml_skip[j],
                                        ml_w_down[j], lng, lnb, rw_hi, rw_lo)
        else:
            fparams = (hy_f_w1[j], hy_f_b1[j], hy_f_freq1[j], hy_f_w2[j], hy_f_b2[j], hy_f_freq2[j], hy_f_w3[j])
            hj, f, logits = hyena_layer(hj, modtab, hy_w_in[j], hy_b_in[j], hy_conv_w[j], hy_conv_b[j], fparams,
                                        hy_skip[j], hy_w_out[j], lng, lnb, rw_hi, rw_lo)
        ya, yb, wts = moe(i, f, logits, router_b, moe_w_gate, moe_w_up, moe_w_down)
        hj = moe_combine(hj, ya, yb, wts, modtab, ln_g[i, 1][None], ln_b[i, 1][None])
    return hj.reshape(bsz, L_JOINT, D_MODEL)[:, :SEQ]
```

```python
import functools
import math

import numpy as np
import jax
import jax.numpy as jnp
from jax import lax
from jax.experimental import pallas as pl
from jax.experimental.pallas import tpu as pltpu

F32 = jnp.float32
BF16 = jnp.bfloat16

D_MODEL = 1024
BATCH = 8
SEQ = 2048
DEPTH = 4
GRID_W = 64
CTX_LEN = 256
L_JOINT = SEQ + CTX_LEN
N_MIXERS = 3
ALPHA = (2.0 * DEPTH) ** 0.25
LN_EPS = 1e-6

D_RNN = 1408
RG_BLOCKS = 16
RG_BS = D_RNN // RG_BLOCKS
RG_C = 8.0

D_M = 2 * D_MODEL
M_HEADS = 8
M_DK = 128
M_DV = D_M // M_HEADS
CHUNK = 128

H_EMB = 33
H_BANDS = (H_EMB - 1) // 2
H_DECAY_TARGET = 1e-2
H_FAST = 0.3
H_SLOW = 1.5

N_EXPERTS = 16
N_GROUPS = 4
EXP_PER_GROUP = N_EXPERTS // N_GROUPS
TOP_K = 2
D_EXPERT = 512
ROUTER_PAD = 128

VMEM_LIMIT_BYTES = 56 * 1024 * 1024
ROW_TILE = 256
MOE_TILE = 256
RG_TBLK = 32
RG_LANES = D_RNN // 128
RG_CONV = 4
RG_WIN = 3
assert RG_BS <= 128 and RG_WIN * 128 >= 128 + 2 * (RG_BS - 1)
CONV_HALO = 16
ML_HEADS_PER_STEP = 2
HY_TN = 256
TILES_PER_BATCH = L_JOINT // ROW_TILE
LATENT_TILES = SEQ // ROW_TILE

MOD_SH, MOD_SC, MOD_GT, MOD_SH2, MOD_SC2, MOD_GT2 = range(6)


def _params(*sem):
    return pltpu.CompilerParams(dimension_semantics=sem, vmem_limit_bytes=VMEM_LIMIT_BYTES)


def _mod_index(i):
    return (2 * (i // TILES_PER_BATCH) + (i % TILES_PER_BATCH) // LATENT_TILES, 0, 0)


def _layer_norm_rows(r, g, b):
    mu = jnp.mean(r, axis=-1, keepdims=True)
    var = jnp.mean(jnp.square(r - mu), axis=-1, keepdims=True)
    return (r - mu) * lax.rsqrt(var + LN_EPS) * g + b


def _mm_body(x_ref, w_ref, o_ref):
    o_ref[...] = jnp.dot(x_ref[...].astype(BF16), w_ref[...], preferred_element_type=F32).astype(o_ref.dtype)


def mm(x, w, *, tm=512, tn=None, out_dtype=F32):
    m, k = x.shape
    n = w.shape[1]
    tn = n if tn is None else tn
    assert m % tm == 0 and n % tn == 0
    return pl.pallas_call(
        _mm_body,
        out_shape=jax.ShapeDtypeStruct((m, n), out_dtype),
        grid=(n // tn, m // tm),
        in_specs=[pl.BlockSpec((tm, k), lambda j, i: (i, 0)),
                  pl.BlockSpec((k, tn), lambda j, i: (0, j))],
        out_specs=pl.BlockSpec((tm, tn), lambda j, i: (i, j)),
        compiler_params=_params("parallel", "parallel"),
        name="mm",
    )(x, w)


def _bmm_body(a_ref, x_ref, o_ref):
    o_ref[...] = jnp.dot(a_ref[...], x_ref[...].astype(BF16), preferred_element_type=F32).astype(o_ref.dtype)


def bmm_left(a, x, *, tn=256, out_dtype=F32):
    mo, k = a.shape
    b, _, d = x.shape
    assert d % tn == 0
    return pl.pallas_call(
        _bmm_body,
        out_shape=jax.ShapeDtypeStruct((b, mo, d), out_dtype),
        grid=(b, d // tn),
        in_specs=[pl.BlockSpec((mo, k), lambda i, j: (0, 0)),
                  pl.BlockSpec((None, k, tn), lambda i, j: (i, 0, j))],
        out_specs=pl.BlockSpec((None, mo, tn), lambda i, j: (i, 0, j)),
        compiler_params=_params("parallel", "parallel"),
        name="dft_mm",
    )(a, x)


def _pack_bf16_pairs(f):
    half = f.shape[1] // 2
    hi = lax.bitcast_convert_type(f[:, :half].astype(BF16).astype(F32), jnp.uint32)
    lo = lax.bitcast_convert_type(f[:, half:].astype(BF16).astype(F32), jnp.uint32)
    return hi | (lo >> 16)


def _unpack_bf16_pairs(u):
    hi = lax.bitcast_convert_type(u & jnp.uint32(0xFFFF0000), F32).astype(BF16)
    lo = lax.bitcast_convert_type(u << 16, F32).astype(BF16)
    return jnp.concatenate([hi, lo], axis=1)


def _post_epilogue(y, h, gt, sc2, sh2, ln_g, ln_b, rw_hi, rw_lo):
    hn = _layer_norm_rows(ALPHA * h + gt * y, ln_g, ln_b)
    f = hn * (1.0 + sc2) + sh2
    f_hi = f.astype(BF16)
    f_lo = (f - f_hi.astype(F32)).astype(BF16)
    lg = (jnp.dot(f_hi, rw_hi, preferred_element_type=F32) + jnp.dot(f_lo, rw_hi, preferred_element_type=F32)
          + jnp.dot(f_hi, rw_lo, preferred_element_type=F32))
    return hn, _pack_bf16_pairs(f), lg


def _post_body(z_ref, w_ref, h_ref, mod_ref, lng_ref, lnb_ref, rwh_ref, rwl_ref, ho_ref, f_ref, lg_ref):
    y = jnp.dot(z_ref[...].astype(BF16), w_ref[...], preferred_element_type=F32)
    hn, fpk, lg = _post_epilogue(y, h_ref[...], mod_ref[MOD_GT:MOD_GT + 1, :], mod_ref[MOD_SC2:MOD_SC2 + 1, :],
                                 mod_ref[MOD_SH2:MOD_SH2 + 1, :], lng_ref[...], lnb_ref[...], rwh_ref[...],
                                 rwl_ref[...])
    ho_ref[...] = hn
    f_ref[...] = fpk
    lg_ref[...] = lg


def post_mixer(z, w_out, h, modtab, ln_g, ln_b, rw_hi, rw_lo):
    t, k = z.shape
    d = w_out.shape[1]
    row = lambda i: (i, 0)
    fixed = lambda i: (0, 0)
    return pl.pallas_call(
        _post_body,
        out_shape=(jax.ShapeDtypeStruct((t, d), F32), jax.ShapeDtypeStruct((t, d // 2), jnp.uint32),
                   jax.ShapeDtypeStruct((t, ROUTER_PAD), F32)),
        grid=(t // ROW_TILE,),
        in_specs=[pl.BlockSpec((ROW_TILE, k), row),
                  pl.BlockSpec((k, d), fixed),
                  pl.BlockSpec((ROW_TILE, d), row),
                  pl.BlockSpec((None, 6, d), _mod_index),
                  pl.BlockSpec((1, d), fixed),
                  pl.BlockSpec((1, d), fixed),
                  pl.BlockSpec((d, ROUTER_PAD), fixed),
                  pl.BlockSpec((d, ROUTER_PAD), fixed)],
        out_specs=(pl.BlockSpec((ROW_TILE, d), row), pl.BlockSpec((ROW_TILE, d // 2), row),
                   pl.BlockSpec((ROW_TILE, ROUTER_PAD), row)),
        compiler_params=_params("parallel"),
        name="post_mixer",
    )(z, w_out, h, modtab, ln_g, ln_b, rw_hi, rw_lo)


def _combine_body(h_ref, ya_ref, yb_ref, w_ref, mod_ref, lng_ref, lnb_ref, o_ref):
    w = w_ref[...]
    ya = _unpack_bf16_pairs(ya_ref[...]).astype(F32)
    yb = _unpack_bf16_pairs(yb_ref[...]).astype(F32)
    y2 = w[:, 0:1] * ya + w[:, 1:2] * yb
    gt2 = mod_ref[MOD_GT2:MOD_GT2 + 1, :]
    o_ref[...] = _layer_norm_rows(ALPHA * h_ref[...] + gt2 * y2, lng_ref[...], lnb_ref[...])


def moe_combine(h, ya, yb, wts, modtab, ln_g, ln_b):
    t, d = h.shape
    row = lambda i: (i, 0)
    fixed = lambda i: (0, 0)
    return pl.pallas_call(
        _combine_body,
        out_shape=jax.ShapeDtypeStruct((t, d), F32),
        grid=(t // ROW_TILE,),
        in_specs=[pl.BlockSpec((ROW_TILE, d), row), pl.BlockSpec((ROW_TILE, d // 2), row),
                  pl.BlockSpec((ROW_TILE, d // 2), row), pl.BlockSpec((ROW_TILE, TOP_K), row),
                  pl.BlockSpec((None, 6, d), _mod_index),
                  pl.BlockSpec((1, d), fixed), pl.BlockSpec((1, d), fixed)],
        out_specs=pl.BlockSpec((ROW_TILE, d), row),
        compiler_params=_params("parallel"),
        name="moe_combine",
    )(h, ya, yb, wts, modtab, ln_g, ln_b)


def _rg_in_body(h_ref, mod_ref, w_ref, gate_ref, rec_ref):
    tt = h_ref.shape[1]
    sc = mod_ref[MOD_SC][:, None, :]
    sh = mod_ref[MOD_SH][:, None, :]
    inp = (h_ref[...] * (1.0 + sc) + sh).reshape(BATCH * tt, D_MODEL)
    p = jnp.dot(inp.astype(BF16), w_ref[...], preferred_element_type=F32)
    gate_ref[...] = jax.nn.gelu(p[:, :D_RNN]).astype(gate_ref.dtype).reshape(BATCH, tt, D_RNN)
    for b in range(BATCH):
        rec_b = p[b * tt:(b + 1) * tt, D_RNN:]
        for j in range(RG_LANES):
            rec_ref[j, pl.ds(b, tt, stride=BATCH), :] = rec_b[:, j * 128:(j + 1) * 128]


def rg_in(h3, modtab_t, w_in):
    nblk = L_JOINT // RG_TBLK
    return pl.pallas_call(
        _rg_in_body,
        out_shape=(jax.ShapeDtypeStruct((BATCH, L_JOINT, D_RNN), BF16),
                   jax.ShapeDtypeStruct((RG_LANES, L_JOINT * BATCH, 128), F32)),
        grid=(nblk,),
        in_specs=[pl.BlockSpec((BATCH, RG_TBLK, D_MODEL), lambda i: (0, i, 0)),
                  pl.BlockSpec((None, 6, BATCH, D_MODEL), lambda i: (i // (SEQ // RG_TBLK), 0, 0, 0)),
                  pl.BlockSpec((D_MODEL, 2 * D_RNN), lambda i: (0, 0))],
        out_specs=(pl.BlockSpec((BATCH, RG_TBLK, D_RNN), lambda i: (0, i, 0)),
                   pl.BlockSpec((RG_LANES, RG_TBLK * BATCH, 128), lambda i: (0, i, 0))),
        compiler_params=_params("parallel"),
        name="rg_in",
    )(h3, modtab_t, w_in)


def _rg_block(z, i):
    nblk = L_JOINT // RG_TBLK
    nlat = SEQ // RG_TBLK
    return jnp.where(z == 0, (i + nlat) % nblk, nblk - 1 - i)


def _rg_window_start(j):
    return min(max(j - 1, 0), RG_LANES - RG_WIN)


def _rg_scan_body(prev_ref, main_ref, next_ref, cw_ref, cb_ref, wg_ref, ba_ref, bx_ref, lam_ref, ho_ref,
                  a_s, u_s, st_ref):
    z = pl.program_id(0)
    i = pl.program_id(1)
    nblk = L_JOINT // RG_TBLK
    nlat = SEQ // RG_TBLK
    blk = _rg_block(z, i)
    rows = RG_TBLK * BATCH

    @pl.when(i == 0)
    def _():
        st_ref[...] = jnp.zeros_like(st_ref)

    has_prev = jnp.where((blk == 0) | (blk == nlat), 0.0, 1.0)
    has_next = jnp.where((blk == nlat - 1) | (blk == nblk - 1), 0.0, 1.0)
    cols = []
    for j in range(RG_LANES):
        ext = jnp.concatenate([prev_ref[j] * has_prev, main_ref[j], next_ref[j] * has_next], axis=0)
        lane = slice(j * 128, (j + 1) * 128)
        acc = cb_ref[:, lane] + cw_ref[0:1, lane] * ext[0:rows]
        for k in range(1, RG_CONV):
            acc = acc + cw_ref[k:k + 1, lane] * ext[k * BATCH:k * BATCH + rows]
        cols.append(acc)
    xcb = jnp.concatenate(cols, axis=1).astype(BF16)

    log_a_unit = -RG_C * jax.nn.softplus(-lam_ref[...])
    for j in range(RG_LANES):
        lane = slice(j * 128, (j + 1) * 128)
        w0 = _rg_window_start(j) * 128
        pre = jnp.dot(xcb[:, w0:w0 + RG_WIN * 128], wg_ref[j], preferred_element_type=F32)
        r = 0.5 * (jnp.tanh(0.5 * (pre[:, :128] + ba_ref[:, lane])) + 1.0)
        g = 0.5 * (jnp.tanh(0.5 * (pre[:, 128:] + bx_ref[:, lane])) + 1.0)
        a = jnp.exp(log_a_unit[:, lane] * r)
        a_s[:, lane] = a
        u_s[:, lane] = jnp.sqrt(1.0 - a * a) * (g * cols[j])

    def step(t, carry):
        te = jnp.where(z == 0, t, RG_TBLK - 1 - t)
        r0 = pl.multiple_of(te * BATCH, BATCH)
        out = []
        for j in range(RG_LANES):
            lane = slice(j * 128, (j + 1) * 128)
            h = a_s[pl.ds(r0, BATCH), lane] * carry[j] + u_s[pl.ds(r0, BATCH), lane]
            ho_ref[j, pl.ds(r0, BATCH), :] = h
            out.append(h)
        return tuple(out)

    final = lax.fori_loop(0, RG_TBLK, step, tuple(st_ref[j] for j in range(RG_LANES)), unroll=4)
    for j in range(RG_LANES):
        st_ref[j] = final[j]


def rg_scan(rec_tm, conv_w, conv_b, w_gates, ga_b, gx_b, lam):
    nblk = L_JOINT // RG_TBLK
    rows = RG_TBLK * BATCH
    halo_prev = (RG_CONV // 2) * BATCH
    per_prev = rows // halo_prev
    fixed2 = lambda z, i: (0, 0)
    per_dir = lambda z, i: (z, 0, 0)
    return pl.pallas_call(
        _rg_scan_body,
        out_shape=jax.ShapeDtypeStruct((2, RG_LANES, L_JOINT * BATCH, 128), F32),
        grid=(2, nblk),
        in_specs=[pl.BlockSpec((RG_LANES, halo_prev, 128),
                               lambda z, i: (0, jnp.maximum(_rg_block(z, i) * per_prev - 1, 0), 0)),
                  pl.BlockSpec((RG_LANES, rows, 128), lambda z, i: (0, _rg_block(z, i), 0)),
                  pl.BlockSpec((RG_LANES, BATCH, 128),
                               lambda z, i: (0, jnp.minimum((_rg_block(z, i) + 1) * RG_TBLK, L_JOINT - 1), 0)),
                  pl.BlockSpec((RG_CONV, D_RNN), fixed2),
                  pl.BlockSpec((1, D_RNN), fixed2),
                  pl.BlockSpec((None, RG_LANES, RG_WIN * 128, 256), lambda z, i: (z, 0, 0, 0)),
                  pl.BlockSpec((None, 1, D_RNN), per_dir),
                  pl.BlockSpec((None, 1, D_RNN), per_dir),
                  pl.BlockSpec((None, 1, D_RNN), per_dir)],
        out_specs=pl.BlockSpec((None, RG_LANES, rows, 128), lambda z, i: (z, 0, _rg_block(z, i), 0)),
        scratch_shapes=[pltpu.VMEM((rows, D_RNN), F32), pltpu.VMEM((rows, D_RNN), F32),
                        pltpu.VMEM((RG_LANES, BATCH, 128), F32)],
        compiler_params=_params("arbitrary", "arbitrary"),
        name="rg_scan",
    )(rec_tm, rec_tm, rec_tm, conv_w, conv_b, w_gates, ga_b, gx_b, lam)


def _rg_out_body(gate_ref, hf_ref, hb_ref, w_ref, h_ref, mod_ref, lng_ref, lnb_ref, rwh_ref, rwl_ref,
                 ho_ref, f_ref, lg_ref):
    tt = h_ref.shape[1]
    parts = []
    for b in range(BATCH):
        hsum = jnp.concatenate([hf_ref[j, pl.ds(b, tt, stride=BATCH), :] + hb_ref[j, pl.ds(b, tt, stride=BATCH), :]
                                for j in range(RG_LANES)], axis=1)
        parts.append((gate_ref[b].astype(F32) * hsum).astype(BF16))
    zz = jnp.concatenate(parts, axis=0)
    y = jnp.dot(zz, w_ref[...], preferred_element_type=F32)
    h = h_ref[...].reshape(BATCH * tt, D_MODEL)
    rep = lambda m: jnp.broadcast_to(mod_ref[m][:, None, :], (BATCH, tt, D_MODEL)).reshape(BATCH * tt, D_MODEL)
    hn, fpk, lg = _post_epilogue(y, h, rep(MOD_GT), rep(MOD_SC2), rep(MOD_SH2), lng_ref[...], lnb_ref[...],
                                 rwh_ref[...], rwl_ref[...])
    ho_ref[...] = hn.reshape(BATCH, tt, D_MODEL)
    f_ref[...] = fpk.reshape(BATCH, tt, D_MODEL // 2)
    lg_ref[...] = lg.reshape(BATCH, tt, ROUTER_PAD)


def rg_out(gate, h_tm, w_out, h3, modtab_t, ln_g, ln_b, rw_hi, rw_lo):
    nblk = L_JOINT // RG_TBLK
    rows = RG_TBLK * BATCH
    blk3 = lambda i: (0, i, 0)
    fixed = lambda i: (0, 0)
    return pl.pallas_call(
        _rg_out_body,
        out_shape=(jax.ShapeDtypeStruct((BATCH, L_JOINT, D_MODEL), F32),
                   jax.ShapeDtypeStruct((BATCH, L_JOINT, D_MODEL // 2), jnp.uint32),
                   jax.ShapeDtypeStruct((BATCH, L_JOINT, ROUTER_PAD), F32)),
        grid=(nblk,),
        in_specs=[pl.BlockSpec((BATCH, RG_TBLK, D_RNN), blk3),
                  pl.BlockSpec((None, RG_LANES, rows, 128), lambda i: (0, 0, i, 0)),
                  pl.BlockSpec((None, RG_LANES, rows, 128), lambda i: (1, 0, i, 0)),
                  pl.BlockSpec((D_RNN, D_MODEL), fixed),
                  pl.BlockSpec((BATCH, RG_TBLK, D_MODEL), blk3),
                  pl.BlockSpec((None, 6, BATCH, D_MODEL), lambda i: (i // (SEQ // RG_TBLK), 0, 0, 0)),
                  pl.BlockSpec((1, D_MODEL), fixed), pl.BlockSpec((1, D_MODEL), fixed),
                  pl.BlockSpec((D_MODEL, ROUTER_PAD), fixed), pl.BlockSpec((D_MODEL, ROUTER_PAD), fixed)],
        out_specs=(pl.BlockSpec((BATCH, RG_TBLK, D_MODEL), blk3),
                   pl.BlockSpec((BATCH, RG_TBLK, D_MODEL // 2), blk3),
                   pl.BlockSpec((BATCH, RG_TBLK, ROUTER_PAD), blk3)),
        compiler_params=_params("parallel"),
        name="rg_out",
    )(gate, h_tm, h_tm, w_out, h3, modtab_t, ln_g, ln_b, rw_hi, rw_lo)


def _mlstm_body(q_ref, k_ref, v_ref, gc_ref, gr_ref, ng_ref, o_ref, hf_s, hb_s, gb_s, gr_s, ct_ref, n_ref, m_ref):
    row = lax.broadcasted_iota(jnp.int32, (CHUNK, CHUNK), 0)
    col = lax.broadcasted_iota(jnp.int32, (CHUNK, CHUNK), 1)
    lower = col <= row
    upper = col >= row
    tri_lower = jnp.where(lower, 1.0, 0.0).astype(BF16)
    tri_upper = jnp.where(upper, 1.0, 0.0).astype(BF16)
    ones_sq = jnp.ones((CHUNK, CHUNK), BF16)
    hp = ML_HEADS_PER_STEP
    tile_order = [(z, hh, kind) for z in range(2) for hh in range(hp) for kind in range(2)]
    n_src = 12 * hp
    src_row = lax.broadcasted_iota(jnp.int32, (2 * n_src, CHUNK), 0) % n_src
    spread = jnp.concatenate(
        [jnp.where(src_row == (4 * hh + 2 * z + (1 - kind) * (1 + 4 * hp * (1 + z))), 1.0, 0.0)
         for z, hh, kind in tile_order], axis=1).astype(BF16)

    def _hi_lo(x):
        hi = x.astype(BF16)
        return hi, (x - hi.astype(F32)).astype(BF16)

    def chunk(c, z, hh):
        reverse = z == 1
        c0 = pl.multiple_of(c * CHUNK, CHUNK)
        q = q_ref[pl.ds(c0, CHUNK), hh * M_DK:(hh + 1) * M_DK]
        k = k_ref[pl.ds(c0, CHUNK), hh * M_DK:(hh + 1) * M_DK]
        v = v_ref[pl.ds(c0, CHUNK), hh * M_DV:(hh + 1) * M_DV]
        mask = upper if reverse else lower
        bc_b = gb_s[z, hh, 0, pl.ds(c0, CHUNK), :]
        ig_b = gb_s[z, hh, 1, pl.ds(c0, CHUNK), :]
        r0 = 2 * (z * ML_HEADS_PER_STEP + hh)
        bc_r = gr_s[r0:r0 + 1, pl.ds(c0, CHUNK)]
        ig_r = gr_s[r0 + 1:r0 + 2, pl.ds(c0, CHUNK)]
        btot = bc_b[0:1, :] if reverse else bc_b[CHUNK - 1:CHUNK, :]
        m_prev = m_ref[z, hh]
        n_prev = n_ref[z, hh]
        ct_prev = ct_ref[z, hh]
        two = lambda a: jnp.concatenate([a, a], axis=1)
        dlog = jnp.where(mask, bc_b - bc_r + ig_r, -jnp.inf)
        m_inter = bc_b + m_prev
        m_comb = jnp.maximum(m_inter, jnp.max(dlog, axis=1, keepdims=True))
        qk = lax.dot_general(q, k, (((1,), (1,)), ((), ())), preferred_element_type=F32)
        s = qk * jnp.exp(dlog - m_comb)
        inter = jnp.exp(m_inter - m_comb)
        sb = s.astype(BF16)
        num = (jnp.dot(sb, v, preferred_element_type=F32)
               + two(inter) * jnp.dot(q, ct_prev.astype(BF16), preferred_element_type=F32))
        s_sum = jnp.dot(sb, ones_sq, preferred_element_type=F32)
        n_rows = jnp.broadcast_to(n_prev, (CHUNK, M_DK)).astype(BF16)
        qn = lax.dot_general(q, n_rows, (((1,), (1,)), ((), ())), preferred_element_type=F32)
        den = s_sum + inter * qn
        h = num * two(1.0 / jnp.maximum(jnp.abs(den), jnp.exp(-m_comb)))
        wlog = btot - bc_b + ig_b
        mloc = jnp.max(wlog, axis=0, keepdims=True)
        wgt = jnp.exp(wlog - mloc)
        m_new = jnp.maximum(btot + m_prev, mloc)
        sp = jnp.exp(btot + m_prev - m_new)
        sl = jnp.exp(mloc - m_new)
        kf = k.astype(F32)
        vw = (v.astype(F32) * two(wgt)).astype(BF16)
        ct_loc = jnp.dot(kf.T.astype(BF16), vw, preferred_element_type=F32)
        ct_ref[z, hh] = two(sp) * ct_prev + two(sl) * ct_loc
        n_ref[z, hh] = sp * n_prev + sl * jnp.sum(kf * wgt, axis=0, keepdims=True)
        m_ref[z, hh] = m_new
        return c0, h

    ct_ref[...] = jnp.zeros_like(ct_ref)
    n_ref[...] = jnp.zeros_like(n_ref)
    m_ref[...] = jnp.zeros_like(m_ref)

    n_chunks = o_ref.shape[0] // CHUNK
    n_lat = SEQ // CHUNK

    def gate_sums(c, carry):
        c0 = pl.multiple_of(c * CHUNK, CHUNK)
        gc = jnp.concatenate([gc_ref[hh, pl.ds(c0, CHUNK), :] for hh in range(hp)], axis=1)
        gr = jnp.concatenate([gr_ref[hh, :, pl.ds(c0, CHUNK)] for hh in range(hp)], axis=0)
        gc_parts = jnp.concatenate(_hi_lo(gc), axis=0)
        cum_c = [jnp.dot(jnp.concatenate([t, t], axis=1), gc_parts, preferred_element_type=F32)
                 for t in (tri_lower, tri_upper)]
        gr_parts = jnp.concatenate(_hi_lo(gr), axis=1)
        cum_r = [jnp.dot(gr_parts, jnp.concatenate([t, t], axis=0), preferred_element_type=F32)
                 for t in (tri_upper, tri_lower)]
        src = jnp.concatenate([gc, cum_c[0], cum_c[1]], axis=1)
        tiles = jnp.dot(jnp.concatenate(_hi_lo(src), axis=1), spread, preferred_element_type=F32)
        for n, (z, hh, kind) in enumerate(tile_order):
            gb_s[z, hh, kind, pl.ds(c0, CHUNK), :] = tiles[:, n * CHUNK:(n + 1) * CHUNK]
            r0 = 2 * (z * hp + hh) + kind
            col = 4 * hh + 2 * z + 1 - kind
            row = cum_r[z][col:col + 1, :] if kind == 0 else gr[col:col + 1, :]
            gr_s[r0:r0 + 1, pl.ds(c0, CHUNK)] = row
        return carry

    lax.fori_loop(0, n_chunks, gate_sums, 0, unroll=3)

    def both_directions(i, carry):
        for hh in range(ML_HEADS_PER_STEP):
            c0, h = chunk((i + n_lat) % n_chunks, 0, hh)
            hf_s[hh, pl.ds(c0, CHUNK), :] = h
            c0, h = chunk(n_chunks - 1 - i, 1, hh)
            hb_s[hh, pl.ds(c0, CHUNK), :] = h
        return carry

    lax.fori_loop(0, n_chunks, both_directions, 0)

    def head_norm(c, carry):
        c0 = pl.multiple_of(c * CHUNK, CHUNK)
        for hh in range(ML_HEADS_PER_STEP):
            tot = hf_s[hh, pl.ds(c0, CHUNK), :] + hb_s[hh, pl.ds(c0, CHUNK), :]
            mu = jnp.mean(tot, axis=1, keepdims=True)
            var = jnp.mean(jnp.square(tot - mu), axis=1, keepdims=True)
            lanes = slice(hh * M_DV, (hh + 1) * M_DV)
            o_ref[pl.ds(c0, CHUNK), lanes] = ((tot - mu) * lax.rsqrt(var + LN_EPS) * ng_ref[:, lanes]
                                             ).astype(o_ref.dtype)
        return carry

    lax.fori_loop(0, n_chunks, head_norm, 0)


def mlstm_cell(q, k, v, g_col, g_row, norm_g):
    b, lj, _ = q.shape
    hp = ML_HEADS_PER_STEP
    return pl.pallas_call(
        _mlstm_body,
        out_shape=jax.ShapeDtypeStruct((b, lj, M_HEADS * M_DV), BF16),
        grid=(b, M_HEADS // hp),
        in_specs=[pl.BlockSpec((None, lj, hp * M_DK), lambda i, h: (i, 0, h)),
                  pl.BlockSpec((None, lj, hp * M_DK), lambda i, h: (i, 0, h)),
                  pl.BlockSpec((None, lj, hp * M_DV), lambda i, h: (i, 0, h)),
                  pl.BlockSpec((None, hp, lj, 4), lambda i, h: (i, h, 0, 0)),
                  pl.BlockSpec((None, hp, 4, lj), lambda i, h: (i, h, 0, 0)),
                  pl.BlockSpec((1, hp * M_DV), lambda i, h: (0, h))],
        out_specs=pl.BlockSpec((None, lj, hp * M_DV), lambda i, h: (i, 0, h)),
        scratch_shapes=[pltpu.VMEM((hp, lj, M_DV), F32), pltpu.VMEM((hp, lj, M_DV), F32),
                        pltpu.VMEM((2, hp, 2, lj, CHUNK), F32), pltpu.VMEM((4 * hp, lj), F32),
                        pltpu.VMEM((2, hp, M_DK, M_DV), F32), pltpu.VMEM((2, hp, 1, M_DK), F32),
                        pltpu.VMEM((2, hp, 1, CHUNK), F32)],
        compiler_params=_params("parallel", "parallel"),
        name="mlstm_cell",
    )(q, k, v, g_col, g_row, norm_g)


def _tile_neighbours(i):
    r = i % TILES_PER_BATCH
    has_prev = jnp.where((r == 0) | (r == LATENT_TILES), 0.0, 1.0)
    has_next = jnp.where((r == LATENT_TILES - 1) | (r == TILES_PER_BATCH - 1), 0.0, 1.0)
    return has_prev, has_next


def _ml_up_body(h_ref, mod_ref, w_ref, o_ref):
    inp = h_ref[...] * (1.0 + mod_ref[MOD_SC:MOD_SC + 1, :]) + mod_ref[MOD_SH:MOD_SH + 1, :]
    o_ref[...] = jnp.dot(inp.astype(BF16), w_ref[...], preferred_element_type=F32).astype(o_ref.dtype)


def mod_proj(h, modtab, w, bias=None):
    t, d = h.shape
    n = w.shape[1]
    body = _ml_up_body
    args = [h, modtab, w]
    in_specs = [pl.BlockSpec((ROW_TILE, d), lambda i: (i, 0)),
                pl.BlockSpec((None, 6, d), _mod_index),
                pl.BlockSpec((d, n), lambda i: (0, 0))]
    if bias is not None:
        def body(h_ref, mod_ref, w_ref, b_ref, o_ref):
            inp = h_ref[...] * (1.0 + mod_ref[MOD_SC:MOD_SC + 1, :]) + mod_ref[MOD_SH:MOD_SH + 1, :]
            y = jnp.dot(inp.astype(BF16), w_ref[...], preferred_element_type=F32) + b_ref[...]
            o_ref[...] = y.astype(o_ref.dtype)
        args.append(bias)
        in_specs.append(pl.BlockSpec((1, n), lambda i: (0, 0)))
    return pl.pallas_call(
        body,
        out_shape=jax.ShapeDtypeStruct((t, n), BF16),
        grid=(t // ROW_TILE,),
        in_specs=in_specs,
        out_specs=pl.BlockSpec((ROW_TILE, n), lambda i: (i, 0)),
        compiler_params=_params("parallel"),
        name="mod_proj",
    )(*args)


def _ml_qk_body(prev_ref, main_ref, next_ref, cw_ref, cb_ref, w_ref, xc_ref, q_ref, k_ref, ext_s):
    has_prev, has_next = _tile_neighbours(pl.program_id(0))
    ext_s[0:CONV_HALO, :] = prev_ref[...].astype(F32) * has_prev
    ext_s[CONV_HALO:CONV_HALO + ROW_TILE, :] = main_ref[...].astype(F32)
    ext_s[CONV_HALO + ROW_TILE:, :] = next_ref[...].astype(F32) * has_next
    kk = cw_ref.shape[0]
    acc = cb_ref[...] + cw_ref[0:1, :] * ext_s[pl.ds(CONV_HALO - kk // 2, ROW_TILE), :]
    for j in range(1, kk):
        acc = acc + cw_ref[j:j + 1, :] * ext_s[pl.ds(CONV_HALO - kk // 2 + j, ROW_TILE), :]
    xc = (acc * jax.nn.sigmoid(acc)).astype(BF16)
    xc_ref[...] = xc
    qk = jnp.dot(xc, w_ref[...], preferred_element_type=F32)
    nq = q_ref.shape[1]
    q_ref[...] = qk[:, :nq].astype(BF16)
    k_ref[...] = (qk[:, nq:] * (M_DK ** -0.5)).astype(BF16)


def ml_qk(xm, conv_w, conv_b, w_qk):
    t, dm = xm.shape
    nq = M_HEADS * M_DK
    per = ROW_TILE // CONV_HALO
    last = t // CONV_HALO - 1
    row = lambda i: (i, 0)
    fixed = lambda i: (0, 0)
    return pl.pallas_call(
        _ml_qk_body,
        out_shape=(jax.ShapeDtypeStruct((t, dm), BF16), jax.ShapeDtypeStruct((t, nq), BF16),
                   jax.ShapeDtypeStruct((t, nq), BF16)),
        grid=(t // ROW_TILE,),
        in_specs=[pl.BlockSpec((CONV_HALO, dm), lambda i: (jnp.maximum(i * per - 1, 0), 0)),
                  pl.BlockSpec((ROW_TILE, dm), row),
                  pl.BlockSpec((CONV_HALO, dm), lambda i: (jnp.minimum((i + 1) * per, last), 0)),
                  pl.BlockSpec(conv_w.shape, fixed), pl.BlockSpec((1, dm), fixed),
                  pl.BlockSpec((dm, 2 * nq), fixed)],
        out_specs=(pl.BlockSpec((ROW_TILE, dm), row), pl.BlockSpec((ROW_TILE, nq), row),
                   pl.BlockSpec((ROW_TILE, nq), row)),
        scratch_shapes=[pltpu.VMEM((ROW_TILE + 2 * CONV_HALO, dm), F32)],
        compiler_params=_params("parallel"),
        name="ml_qk",
    )(xm, xm, xm, conv_w, conv_b, w_qk)


def _mm_act_body(x_ref, w_ref, o_ref, *, act):
    y = jnp.dot(x_ref[...], w_ref[...], preferred_element_type=F32)
    if act == "sigmoid":
        y = jax.nn.sigmoid(y)
    o_ref[...] = y.astype(o_ref.dtype)


def mm_act(x, w, act=None, *, tm=512, tn=1024):
    m, k = x.shape
    n = w.shape[1]
    return pl.pallas_call(
        functools.partial(_mm_act_body, act=act),
        out_shape=jax.ShapeDtypeStruct((m, n), BF16),
        grid=(n // tn, m // tm),
        in_specs=[pl.BlockSpec((tm, k), lambda j, i: (i, 0)),
                  pl.BlockSpec((k, tn), lambda j, i: (0, j))],
        out_specs=pl.BlockSpec((tm, tn), lambda j, i: (i, j)),
        compiler_params=_params("parallel", "parallel"),
        name="mm_act",
    )(x, w)


def _ml_gates_body(x_ref, w_ref, b_ref, o_ref):
    g = jnp.dot(x_ref[...], w_ref[...], preferred_element_type=F32) + b_ref[...]
    lane = lax.broadcasted_iota(jnp.int32, g.shape, 1)
    is_forget = (lane % (2 * M_HEADS)) >= M_HEADS
    o_ref[...] = jnp.where(is_forget, jax.nn.log_sigmoid(g), g)


def ml_gates(xm, w_g, b_g):
    t, dm = xm.shape
    return pl.pallas_call(
        _ml_gates_body,
        out_shape=jax.ShapeDtypeStruct((t, 128), F32),
        grid=(t // 512,),
        in_specs=[pl.BlockSpec((512, dm), lambda i: (i, 0)), pl.BlockSpec((dm, 128), lambda i: (0, 0)),
                  pl.BlockSpec((1, 128), lambda i: (0, 0))],
        out_specs=pl.BlockSpec((512, 128), lambda i: (i, 0)),
        compiler_params=_params("parallel"),
        name="ml_gates",
    )(xm, w_g, b_g)


def _post_ml_body(o_ref, hn_ref, xc_ref, skip_ref, w_ref, h_ref, mod_ref, lng_ref, lnb_ref, rwh_ref, rwl_ref,
                  ho_ref, f_ref, lg_ref):
    z = o_ref[...].astype(F32) * hn_ref[...].astype(F32) + skip_ref[...] * xc_ref[...].astype(F32)
    y = jnp.dot(z.astype(BF16), w_ref[...], preferred_element_type=F32)
    hn, fpk, lg = _post_epilogue(y, h_ref[...], mod_ref[MOD_GT:MOD_GT + 1, :], mod_ref[MOD_SC2:MOD_SC2 + 1, :],
                                 mod_ref[MOD_SH2:MOD_SH2 + 1, :], lng_ref[...], lnb_ref[...], rwh_ref[...],
                                 rwl_ref[...])
    ho_ref[...] = hn
    f_ref[...] = fpk
    lg_ref[...] = lg


def post_mixer_ml(o, hn, xc, skip, w_out, h, modtab, ln_g, ln_b, rw_hi, rw_lo):
    t, k = o.shape
    d = w_out.shape[1]
    row = lambda i: (i, 0)
    fixed = lambda i: (0, 0)
    return pl.pallas_call(
        _post_ml_body,
        out_shape=(jax.ShapeDtypeStruct((t, d), F32), jax.ShapeDtypeStruct((t, d // 2), jnp.uint32),
                   jax.ShapeDtypeStruct((t, ROUTER_PAD), F32)),
        grid=(t // ROW_TILE,),
        in_specs=[pl.BlockSpec((ROW_TILE, k), row), pl.BlockSpec((ROW_TILE, k), row),
                  pl.BlockSpec((ROW_TILE, k), row), pl.BlockSpec((1, k), fixed),
                  pl.BlockSpec((k, d), fixed),
                  pl.BlockSpec((ROW_TILE, d), row),
                  pl.BlockSpec((None, 6, d), _mod_index),
                  pl.BlockSpec((1, d), fixed), pl.BlockSpec((1, d), fixed),
                  pl.BlockSpec((d, ROUTER_PAD), fixed), pl.BlockSpec((d, ROUTER_PAD), fixed)],
        out_specs=(pl.BlockSpec((ROW_TILE, d), row), pl.BlockSpec((ROW_TILE, d // 2), row),
                   pl.BlockSpec((ROW_TILE, ROUTER_PAD), row)),
        compiler_params=_params("parallel"),
        name="post_mixer_ml",
    )(o, hn, xc, skip, w_out, h, modtab, ln_g, ln_b, rw_hi, rw_lo)


def _hy_conv(u_ref, cw_ref, cb_ref, ext_s):
    n_time = u_ref.shape[0]
    ext_s[0:8, :] = jnp.zeros((8, ext_s.shape[1]), F32)
    ext_s[8:8 + n_time, :] = u_ref[...].astype(F32)
    ext_s[8 + n_time:, :] = jnp.zeros((8, ext_s.shape[1]), F32)
    kk = cw_ref.shape[0]
    acc = cb_ref[...] + cw_ref[0:1, :] * ext_s[pl.ds(8 - kk // 2, n_time), :]
    for j in range(1, kk):
        acc = acc + cw_ref[j:j + 1, :] * ext_s[pl.ds(8 - kk // 2 + j, n_time), :]
    return acc


def _hy_spectrum(fm_ref, sig, hf_ref, y_ref):
    x = jnp.dot(fm_ref[...], sig, preferred_element_type=F32)
    half = x.shape[0] // 2
    xt, xb = x[:half], x[half:]
    ht, hb = hf_ref[0:half, :], hf_ref[half:, :]
    first = lax.broadcasted_iota(jnp.int32, xt.shape, 0) == 0
    y_ref[0:half, :] = jnp.where(first, xt * ht, xt * ht - xb * hb).astype(y_ref.dtype)
    y_ref[half:, :] = jnp.where(first, xb * hb, xt * hb + xb * ht).astype(y_ref.dtype)


def _hy_fwd_conv_body(u_ref, cw_ref, cb_ref, fm_ref, hf_ref, v_ref, y_ref, ext_s):
    v = _hy_conv(u_ref, cw_ref, cb_ref, ext_s).astype(BF16)
    v_ref[...] = v
    _hy_spectrum(fm_ref, v, hf_ref, y_ref)


def _hy_fwd_body(s_ref, fm_ref, hf_ref, y_ref):
    _hy_spectrum(fm_ref, s_ref[...], hf_ref, y_ref)


def _hy_inv_body(y_ref, gm_ref, u_ref, cw_ref, cb_ref, s_ref, skip_ref, z_ref, ext_s):
    y = jnp.dot(gm_ref[...], y_ref[...], preferred_element_type=F32)
    g = _hy_conv(u_ref, cw_ref, cb_ref, ext_s)
    z_ref[...] = (g * (y + s_ref[...].astype(F32) * skip_ref[...])).astype(z_ref.dtype)


def hyena_segment(u3, seg_block, n_time, conv_w, conv_b, hf, skip):
    b = u3.shape[0]
    tn = HY_TN
    nj = D_MODEL // tn
    fm, gm = _dft_mats(n_time)
    n = fm.shape[0]
    grid = (nj, b)
    u_spec = lambda part: pl.BlockSpec((None, n_time, tn), lambda j, i: (i, seg_block, part * nj + j))
    cw_spec = lambda part: pl.BlockSpec((conv_w.shape[0], tn), lambda j, i: (0, part * nj + j))
    cb_spec = lambda part: pl.BlockSpec((1, tn), lambda j, i: (0, part * nj + j))
    fixed = lambda j, i: (0, 0)
    sig_spec = pl.BlockSpec((None, n_time, tn), lambda j, i: (i, 0, j))
    spec_spec = pl.BlockSpec((None, n, tn), lambda j, i: (i, 0, j))
    hf_spec = lambda c: pl.BlockSpec((n, tn), lambda j, i: (0, c * nj + j))
    skip_spec = lambda c: pl.BlockSpec((None, 1, tn), lambda j, i: (c, 0, j))
    ext = pltpu.VMEM((n_time + 16, tn), F32)
    sig_shape = jax.ShapeDtypeStruct((b, n_time, D_MODEL), BF16)
    spec_shape = jax.ShapeDtypeStruct((b, n, D_MODEL), BF16)
    par = _params("parallel", "parallel")
    skip3 = skip[:, None, :]

    v, y1 = pl.pallas_call(
        _hy_fwd_conv_body, out_shape=(sig_shape, spec_shape), grid=grid,
        in_specs=[u_spec(0), cw_spec(0), cb_spec(0), pl.BlockSpec(fm.shape, fixed), hf_spec(0)],
        out_specs=(sig_spec, spec_spec), scratch_shapes=[ext], compiler_params=par, name="hy_fwd_conv",
    )(u3, conv_w, conv_b, fm, hf)
    z1 = pl.pallas_call(
        _hy_inv_body, out_shape=sig_shape, grid=grid,
        in_specs=[spec_spec, pl.BlockSpec(gm.shape, fixed), u_spec(1), cw_spec(1), cb_spec(1), sig_spec,
                  skip_spec(0)],
        out_specs=sig_spec, scratch_shapes=[ext], compiler_params=par, name="hy_inv",
    )(y1, gm, u3, conv_w, conv_b, v, skip3)
    y2 = pl.pallas_call(
        _hy_fwd_body, out_shape=spec_shape, grid=grid,
        in_specs=[sig_spec, pl.BlockSpec(fm.shape, fixed), hf_spec(1)],
        out_specs=spec_spec, compiler_params=par, name="hy_fwd",
    )(z1, fm, hf)
    return pl.pallas_call(
        _hy_inv_body, out_shape=sig_shape, grid=grid,
        in_specs=[spec_spec, pl.BlockSpec(gm.shape, fixed), u_spec(2), cw_spec(2), cb_spec(2), sig_spec,
                  skip_spec(1)],
        out_specs=sig_spec, scratch_shapes=[ext], compiler_params=par, name="hy_inv",
    )(y2, gm, u3, conv_w, conv_b, z1, skip3)


def _moe_body(te_ref, tf_ref, nu_ref, x_ref, wg_ref, wu_ref, wd_ref, o_ref, wg_s, wu_s, wd_s):
    i = pl.program_id(0)

    @pl.when(i < nu_ref[0])
    def _():
        @pl.when(tf_ref[i] == 1)
        def _():
            wg_s[...] = wg_ref[...].astype(BF16)
            wu_s[...] = wu_ref[...].astype(BF16)
            wd_s[...] = wd_ref[...].astype(BF16)

        x = _unpack_bf16_pairs(x_ref[...])
        a = jnp.dot(x, wg_s[...], preferred_element_type=F32)
        u = jnp.dot(x, wu_s[...], preferred_element_type=F32)
        hid = (a * jax.nn.sigmoid(a)) * u
        o_ref[...] = _pack_bf16_pairs(jnp.dot(hid.astype(BF16), wd_s[...], preferred_element_type=F32))

    @pl.when(i >= nu_ref[0])
    def _():
        o_ref[...] = jnp.zeros_like(o_ref)


def moe_experts(layer, tile_expert, tile_first, n_used, xs, w_gate, w_up, w_down):
    n_rows = xs.shape[0]
    d = 2 * xs.shape[1]
    n_tiles = n_rows // MOE_TILE
    wmap_in = lambda i, te, tf, nu: (layer, te[i], 0, 0)
    grid_spec = pltpu.PrefetchScalarGridSpec(
        num_scalar_prefetch=3,
        grid=(n_tiles,),
        in_specs=[pl.BlockSpec((MOE_TILE, d // 2), lambda i, te, tf, nu: (i, 0)),
                  pl.BlockSpec((None, None, d, D_EXPERT), wmap_in),
                  pl.BlockSpec((None, None, d, D_EXPERT), wmap_in),
                  pl.BlockSpec((None, None, D_EXPERT, d), wmap_in)],
        out_specs=pl.BlockSpec((MOE_TILE, d // 2), lambda i, te, tf, nu: (i, 0)),
        scratch_shapes=[pltpu.VMEM((d, D_EXPERT), BF16), pltpu.VMEM((d, D_EXPERT), BF16),
                        pltpu.VMEM((D_EXPERT, d), BF16)],
    )
    return pl.pallas_call(
        _moe_body,
        out_shape=jax.ShapeDtypeStruct((n_rows, d // 2), jnp.uint32),
        grid_spec=grid_spec,
        compiler_params=_params("arbitrary"),
        name="moe_experts",
    )(tile_expert, tile_first, n_used, xs, w_gate, w_up, w_down)


def _first_max4(a):
    m = jnp.maximum(jnp.maximum(a[0], a[1]), jnp.maximum(a[2], a[3]))
    idx = jnp.where(a[0] == m, 0, jnp.where(a[1] == m, 1, jnp.where(a[2] == m, 2, 3))).astype(jnp.int32)
    return m, idx


def _router_body(lg_ref, rb_ref, dest_ref, wts_ref, te_ref, tf_ref, nu_ref, lgt_s, cnt_s, pre_s):
    t = lg_ref.shape[0]
    n_chunk = t // 128

    def transpose(c, carry):
        c0 = pl.multiple_of(c * 128, 128)
        lgt_s[:, pl.ds(c0, 128)] = lg_ref[pl.ds(c0, 128), :].T[0:N_EXPERTS, :]
        return carry

    lax.fori_loop(0, n_chunk, transpose, 0)

    s = jax.nn.sigmoid(lgt_s[...])
    sel = s + rb_ref[...]
    gs, i1s, i2s = [], [], []
    for g in range(N_GROUPS):
        a = [sel[g * EXP_PER_GROUP + j:g * EXP_PER_GROUP + j + 1, :] for j in range(EXP_PER_GROUP)]
        m1, i1 = _first_max4(a)
        m2, i2 = _first_max4([jnp.where(i1 == j, -jnp.inf, a[j]) for j in range(EXP_PER_GROUP)])
        gs.append(m1 + m2)
        i1s.append(i1)
        i2s.append(i2)
    _, gb = _first_max4(gs)
    pick = lambda v: jnp.where(gb == 0, v[0], jnp.where(gb == 1, v[1], jnp.where(gb == 2, v[2], v[3])))
    e1 = gb * EXP_PER_GROUP + pick(i1s)
    e2 = gb * EXP_PER_GROUP + pick(i2s)
    eid = lax.broadcasted_iota(jnp.int32, (N_EXPERTS, t), 0)
    is1 = eid == e1
    is2 = eid == e2
    w1 = jnp.sum(jnp.where(is1, s, 0.0), axis=0, keepdims=True)
    w2 = jnp.sum(jnp.where(is2, s, 0.0), axis=0, keepdims=True)
    tot = w1 + w2
    wts_ref[...] = jnp.concatenate([w1 / tot, w2 / tot, jnp.zeros((6, t), F32)], axis=0)

    cnt_s[...] = jnp.where(is1 | is2, 1.0, 0.0).astype(BF16)
    r = lax.broadcasted_iota(jnp.int32, (128, 128), 0)
    c = lax.broadcasted_iota(jnp.int32, (128, 128), 1)
    before = jnp.where(r < c, 1.0, 0.0).astype(BF16)
    ones = jnp.ones((128, 128), BF16)

    def prefix(ci, carry):
        c0 = pl.multiple_of(ci * 128, 128)
        blk = cnt_s[:, pl.ds(c0, 128)]
        pre_s[:, pl.ds(c0, 128)] = carry + jnp.dot(blk, before, preferred_element_type=F32)
        return carry + jnp.dot(blk, ones, preferred_element_type=F32)

    counts = lax.fori_loop(0, n_chunk, prefix, jnp.zeros((N_EXPERTS, 128), F32))
    tiles_per = jnp.floor((counts + (MOE_TILE - 1)) * (1.0 / MOE_TILE))
    er = lax.broadcasted_iota(jnp.int32, (N_EXPERTS, N_EXPERTS), 0)
    ec = lax.broadcasted_iota(jnp.int32, (N_EXPERTS, N_EXPERTS), 1)
    earlier = jnp.where(ec < er, 1.0, 0.0).astype(BF16)
    tile_start = jnp.dot(earlier, tiles_per.astype(BF16), preferred_element_type=F32)
    tile_end = tile_start + tiles_per
    slot = tile_start[:, 0:1] * MOE_TILE + pre_s[...]
    d1 = jnp.sum(jnp.where(is1, slot, 0.0), axis=0, keepdims=True)
    d2 = jnp.sum(jnp.where(is2, slot, 0.0), axis=0, keepdims=True)
    dest_ref[...] = jnp.concatenate([d1, d2, jnp.zeros((6, t), F32)], axis=0).astype(jnp.int32)

    n_used = tile_end[N_EXPERTS - 1:N_EXPERTS, 0:1]
    tid = lax.broadcasted_iota(jnp.int32, (N_EXPERTS, te_ref.shape[1]), 1).astype(F32)
    tid_c = jnp.minimum(tid, n_used - 1.0)
    t_exp = jnp.sum(jnp.where(tile_end[:, 0:1] <= tid_c, 1.0, 0.0), axis=0, keepdims=True)
    t_exp = jnp.minimum(t_exp, N_EXPERTS - 1.0)
    t_first = jnp.sum(jnp.where((tile_start[:, 0:1] == tid) & (tiles_per[:, 0:1] > 0.0), 1.0, 0.0),
                      axis=0, keepdims=True)
    te_ref[...] = jnp.broadcast_to(t_exp, te_ref.shape).astype(jnp.int32)
    tf_ref[...] = jnp.broadcast_to(t_first, tf_ref.shape).astype(jnp.int32)
    nu_ref[...] = jnp.broadcast_to(n_used, nu_ref.shape).astype(jnp.int32)


def moe_router(logits, router_b):
    t = logits.shape[0]
    n_tab = 256
    assert t * TOP_K // MOE_TILE + N_EXPERTS <= n_tab
    full = lambda shape: pl.BlockSpec(shape, lambda i: (0, 0))
    dest, wts, te, tf, nu = pl.pallas_call(
        _router_body,
        out_shape=(jax.ShapeDtypeStruct((8, t), jnp.int32), jax.ShapeDtypeStruct((8, t), F32),
                   jax.ShapeDtypeStruct((8, n_tab), jnp.int32), jax.ShapeDtypeStruct((8, n_tab), jnp.int32),
                   jax.ShapeDtypeStruct((8, 128), jnp.int32)),
        grid=(1,),
        in_specs=[full((t, ROUTER_PAD)), full((N_EXPERTS, 1))],
        out_specs=(full((8, t)), full((8, t)), full((8, n_tab)), full((8, n_tab)), full((8, 128))),
        scratch_shapes=[pltpu.VMEM((N_EXPERTS, t), F32), pltpu.VMEM((N_EXPERTS, t), BF16),
                        pltpu.VMEM((N_EXPERTS, t), F32)],
        compiler_params=_params("arbitrary"),
        name="moe_router",
    )(logits, router_b[:, None])
    return dest[:TOP_K], wts[:TOP_K], te[0], tf[0], nu[0, :1]


def moe(layer, f, logits, router_b, w_gate, w_up, w_down):
    t = f.shape[0]
    n_tiles = t * TOP_K // MOE_TILE + N_EXPERTS
    n_rows = n_tiles * MOE_TILE
    dest, wts, tile_expert, tile_first, n_used = moe_router(logits, router_b)
    token = jnp.arange(t, dtype=jnp.int32)
    row_token = jnp.zeros((n_rows,), jnp.int32).at[dest.reshape(-1)].set(
        jnp.concatenate([token, token]), unique_indices=True)
    xs = jnp.take(f, row_token, axis=0)
    ys = moe_experts(layer, tile_expert[:n_tiles], tile_first[:n_tiles], n_used, xs, w_gate, w_up, w_down)
    return jnp.take(ys, dest[0], axis=0), jnp.take(ys, dest[1], axis=0), wts.T


def sincos_2d(n_tok):
    rows = n_tok // GRID_W
    quarter = D_MODEL // 4
    omega = 1.0 / (10000.0 ** (jnp.arange(quarter, dtype=F32) / quarter))
    ar = jnp.arange(rows, dtype=F32)[:, None] * omega
    ac = jnp.arange(GRID_W, dtype=F32)[:, None] * omega
    er = jnp.concatenate([jnp.sin(ar), jnp.cos(ar)], -1)
    ec = jnp.concatenate([jnp.sin(ac), jnp.cos(ac)], -1)
    half = D_MODEL // 2
    pos = jnp.concatenate([jnp.broadcast_to(er[:, None], (rows, GRID_W, half)),
                           jnp.broadcast_to(ec[None], (rows, GRID_W, half))], -1)
    return pos.reshape(rows * GRID_W, D_MODEL)


def rglru_layer(h3, modtab_t, w_in, conv_w, conv_b, ga_w, ga_b, gx_w, gx_b, lam, w_out, ln_g, ln_b, rw_hi, rw_lo):
    def block_diag(w):
        eye = jnp.eye(RG_BLOCKS, dtype=w.dtype)
        return jnp.einsum("nkj,nm->nkmj", w, eye).reshape(D_RNN, D_RNN)

    def banded(wa, wx):
        out = []
        for jj in range(RG_LANES):
            r0 = _rg_window_start(jj) * 128
            cs = slice(jj * 128, (jj + 1) * 128)
            out.append(jnp.concatenate([wa[r0:r0 + RG_WIN * 128, cs], wx[r0:r0 + RG_WIN * 128, cs]], axis=1))
        return jnp.stack(out)

    w_gates = jnp.stack([banded(block_diag(ga_w[z]), block_diag(gx_w[z]))
                         for z in range(2)]).astype(BF16)
    gate, rec_tm = rg_in(h3, modtab_t, w_in.astype(BF16))
    h_tm = rg_scan(rec_tm, conv_w, conv_b[None], w_gates, ga_b[:, None], gx_b[:, None], lam[:, None])
    hn, fpk, lg = rg_out(gate, h_tm, w_out.astype(BF16), h3, modtab_t, ln_g, ln_b, rw_hi, rw_lo)
    rows = BATCH * L_JOINT
    return hn.reshape(rows, D_MODEL), fpk.reshape(rows, D_MODEL // 2), lg.reshape(rows, ROUTER_PAD)


def mlstm_layer(hj, modtab, w_up, conv_w, conv_b, w_q, w_k, w_v, w_o, w_if, b_if, norm_g, skip, w_down,
                ln_g, ln_b, rw_hi, rw_lo):
    rows = hj.shape[0]
    xm = mod_proj(hj, modtab, w_up.astype(BF16))
    xc, q, k = ml_qk(xm, conv_w, conv_b[None], jnp.concatenate([w_q, w_k], axis=1).astype(BF16))
    v = mm_act(xm, w_v.astype(BF16))
    o = mm_act(xm, w_o.astype(BF16), "sigmoid")
    n_gate = 4 * M_HEADS
    w_g = jnp.concatenate([w_if[0], w_if[1], jnp.zeros((D_M, 128 - n_gate), F32)], axis=1).astype(BF16)
    b_g = jnp.concatenate([b_if[0], b_if[1], jnp.zeros((128 - n_gate,), F32)])[None]
    g = ml_gates(xm, w_g, b_g)[:, :n_gate].reshape(BATCH, L_JOINT, 2, 2, M_HEADS)
    g4 = jnp.transpose(g, (0, 1, 4, 2, 3)).reshape(BATCH, L_JOINT, M_HEADS, 4)
    g_col = jnp.transpose(g4, (0, 2, 1, 3))
    g_row = jnp.transpose(g4, (0, 2, 3, 1))
    r3 = lambda a: a.reshape(BATCH, L_JOINT, a.shape[-1])
    hn = mlstm_cell(r3(q), r3(k), r3(v), g_col, g_row, norm_g[None]).reshape(rows, D_M)
    return post_mixer_ml(o, hn, xc, skip[None], w_down.astype(BF16), hj, modtab, ln_g, ln_b, rw_hi, rw_lo)


def _dft_mats(n_time):
    n = 3 * n_time // 2
    half = n // 2
    kk = np.arange(half, dtype=np.int64)[:, None]
    tt = np.arange(n_time, dtype=np.int64)[None, :]
    ang = 2.0 * np.pi * ((kk * tt) % n).astype(np.float64) / n
    top = np.cos(ang)
    bot = -np.sin(ang)
    bot[0] = np.cos(np.pi * tt[0])
    fwd = np.concatenate([top, bot], axis=0)
    mm_ = (np.arange(n_time, dtype=np.int64) + n_time // 2)[:, None]
    ang2 = 2.0 * np.pi * ((mm_ * kk.T) % n).astype(np.float64) / n
    wk = np.full((1, half), 2.0)
    wk[0, 0] = 1.0
    itop = wk * np.cos(ang2) / n
    ibot = -2.0 * np.sin(ang2) / n
    ibot[:, 0] = np.cos(np.pi * mm_[:, 0]) / n
    inv = np.concatenate([itop, ibot], axis=1)
    return jnp.asarray(fwd, dtype=BF16), jnp.asarray(inv, dtype=BF16)


def hyena_filters(n_time, w1, b1, fq1, w2, b2, fq2, w3):
    hp = lax.Precision.HIGHEST
    t01 = jnp.linspace(0.0, 1.0, n_time, dtype=F32)
    bands = jnp.linspace(1e-4, H_BANDS - 1, H_BANDS, dtype=F32)
    ang = (2.0 * math.pi / n_time) * jnp.arange(n_time, dtype=F32)[:, None] * bands[None, :]
    z = jnp.concatenate([t01[:, None], jnp.cos(ang), -jnp.sin(ang)], -1)
    hdn = jnp.sin(fq1 * (jnp.dot(z, w1, precision=hp) + b1))
    hdn = jnp.sin(fq2 * (jnp.dot(hdn, w2, precision=hp) + b2))
    filt = jnp.dot(hdn, w3, precision=hp).reshape(n_time, 2, D_MODEL)
    dist = jnp.abs(jnp.arange(n_time) - n_time // 2).astype(F32) * (2.0 / n_time)
    d_max = math.log(H_DECAY_TARGET) / H_FAST
    d_min = math.log(H_DECAY_TARGET) / H_SLOW
    deltas = jnp.abs(jnp.linspace(d_min, d_max, D_MODEL, dtype=F32))
    window = jnp.exp(-dist[:, None] * deltas[None, :])
    return filt * window[:, None, :]


def hyena_layer(hj, modtab, w_in, b_in, conv_w, conv_b, fparams, skip, w_out, ln_g, ln_b, rw_hi, rw_lo):
    rows = hj.shape[0]
    u3 = mod_proj(hj, modtab, w_in.astype(BF16), b_in[None]).reshape(BATCH, L_JOINT, 3 * D_MODEL)
    parts = []
    for seg_block, n_time in ((0, SEQ), (SEQ // CTX_LEN, CTX_LEN)):
        filt = hyena_filters(n_time, *fparams)
        fm, _ = _dft_mats(n_time)
        hf = bmm_left(fm, filt.reshape(1, n_time, 2 * D_MODEL))[0]
        parts.append(hyena_segment(u3, seg_block, n_time, conv_w, conv_b[None], hf, skip))
    z = jnp.concatenate(parts, axis=1).reshape(rows, D_MODEL)
    return post_mixer(z, w_out.astype(BF16), hj, modtab, ln_g, ln_b, rw_hi, rw_lo)


def kernel(x, c, ctx, c_ctx, router_w, router_b, ada_w, ada_b, ln_g, ln_b, moe_w_gate, moe_w_up, moe_w_down, rg_w_in, rg_conv_w, rg_conv_b, rg_gate_a_w, rg_gate_a_b, rg_gate_x_w, rg_gate_x_b, rg_lambda, rg_w_out, ml_w_up, ml_conv_w, ml_conv_b, ml_w_q, ml_w_k, ml_w_v, ml_w_o, ml_w_if, ml_b_if, ml_norm_g, ml_skip, ml_w_down, hy_w_in, hy_b_in, hy_conv_w, hy_conv_b, hy_f_w1, hy_f_b1, hy_f_freq1, hy_f_w2, hy_f_b2, hy_f_freq2, hy_f_w3, hy_skip, hy_w_out):
    bsz = x.shape[0]
    rows = bsz * L_JOINT
    hx = x + sincos_2d(SEQ)[None]
    hj = jnp.concatenate([hx, ctx], axis=1).reshape(rows, D_MODEL)
    is_ctx = (jnp.arange(L_JOINT) >= SEQ)[None, :, None]

    cond = jnp.concatenate([jax.nn.silu(c), jax.nn.silu(c_ctx)[None],
                            jnp.zeros((16 - bsz - 1, D_MODEL), F32)], axis=0)
    rw_pad = jnp.concatenate([router_w, jnp.zeros((D_MODEL, ROUTER_PAD - N_EXPERTS), F32)], axis=1)
    rw_hi = rw_pad.astype(BF16)
    rw_lo = (rw_pad - rw_hi.astype(F32)).astype(BF16)

    for i in range(DEPTH):
        kind, j = i % N_MIXERS, i // N_MIXERS
        mod = (mm(cond, ada_w[i].astype(BF16), tm=16, tn=D_MODEL) + ada_b[i]).reshape(16, 6, D_MODEL)
        mod_x = mod[:bsz]
        mod_c = jnp.broadcast_to(mod[bsz][None], (bsz, 6, D_MODEL))
        modtab = jnp.stack([mod_x, mod_c], axis=1).reshape(2 * bsz, 6, D_MODEL)
        lng, lnb = ln_g[i, 0][None], ln_b[i, 0][None]
        if kind == 0:
            modtab_t = jnp.stack([jnp.transpose(mod_x, (1, 0, 2)), jnp.transpose(mod_c, (1, 0, 2))])
            hj, f, logits = rglru_layer(hj.reshape(bsz, L_JOINT, D_MODEL), modtab_t, rg_w_in[j], rg_conv_w[j],
                                        rg_conv_b[j], rg_gate_a_w[j], rg_gate_a_b[j], rg_gate_x_w[j],
                                        rg_gate_x_b[j], rg_lambda[j], rg_w_out[j], lng, lnb, rw_hi, rw_lo)
        elif kind == 1:
            hj, f, logits = mlstm_layer(hj, modtab, ml_w_up[j], ml_conv_w[j], ml_conv_b[j], ml_w_q[j], ml_w_k[j],
                                        ml_w_v[j], ml_w_o[j], ml_w_if[j], ml_b_if[j], ml_norm_g[j], ml_skip[j],
                                        ml_w_down[j], lng, lnb, rw_hi, rw_lo)
        else:
            fparams = (hy_f_w1[j], hy_f_b1[j], hy_f_freq1[j], hy_f_w2[j], hy_f_b2[j], hy_f_freq2[j], hy_f_w3[j])
            hj, f, logits = hyena_layer(hj, modtab, hy_w_in[j], hy_b_in[j], hy_conv_w[j], hy_conv_b[j], fparams,
                                        hy_skip[j], hy_w_out[j], lng, lnb, rw_hi, rw_lo)
        ya, yb, wts = moe(i, f, logits, router_b, moe_w_gate, moe_w_up, moe_w_down)
        hj = moe_combine(hj, ya, yb, wts, modtab, ln_g[i, 1][None], ln_b[i, 1][None])
    return hj.reshape(bsz, L_JOINT, D_MODEL)[:, :SEQ]
```

```python
import functools
import math

import numpy as np
import jax
import jax.numpy as jnp
from jax import lax
from jax.experimental import pallas as pl
from jax.experimental.pallas import tpu as pltpu

F32 = jnp.float32
BF16 = jnp.bfloat16

D_MODEL = 1024
BATCH = 8
SEQ = 2048
DEPTH = 4
GRID_W = 64
CTX_LEN = 256
L_JOINT = SEQ + CTX_LEN
N_MIXERS = 3
ALPHA = (2.0 * DEPTH) ** 0.25
LN_EPS = 1e-6

D_RNN = 1408
RG_BLOCKS = 16
RG_BS = D_RNN // RG_BLOCKS
RG_C = 8.0

D_M = 2 * D_MODEL
M_HEADS = 8
M_DK = 128
M_DV = D_M // M_HEADS
CHUNK = 128

H_EMB = 33
H_BANDS = (H_EMB - 1) // 2
H_DECAY_TARGET = 1e-2
H_FAST = 0.3
H_SLOW = 1.5

N_EXPERTS = 16
N_GROUPS = 4
EXP_PER_GROUP = N_EXPERTS // N_GROUPS
TOP_K = 2
D_EXPERT = 512
ROUTER_PAD = 128

VMEM_LIMIT_BYTES = 56 * 1024 * 1024
ROW_TILE = 256
MOE_TILE = 512
RG_TBLK = 64
RG_LANES = D_RNN // 128
RG_CONV = 4
RG_WIN = 3
assert RG_BS <= 128 and RG_WIN * 128 >= 128 + 2 * (RG_BS - 1)
CONV_HALO = 16
ML_HEADS_PER_STEP = 2
HY_TN = 256
TILES_PER_BATCH = L_JOINT // ROW_TILE
LATENT_TILES = SEQ // ROW_TILE

MOD_SH, MOD_SC, MOD_GT, MOD_SH2, MOD_SC2, MOD_GT2 = range(6)


def _params(*sem):
    return pltpu.CompilerParams(dimension_semantics=sem, vmem_limit_bytes=VMEM_LIMIT_BYTES)


def _mod_index(i):
    return (2 * (i // TILES_PER_BATCH) + (i % TILES_PER_BATCH) // LATENT_TILES, 0, 0)


def _layer_norm_rows(r, g, b):
    mu = jnp.mean(r, axis=-1, keepdims=True)
    var = jnp.mean(jnp.square(r - mu), axis=-1, keepdims=True)
    return (r - mu) * lax.rsqrt(var + LN_EPS) * g + b


def _cond_body(c_ref, w_ref, b_ref, o_ref):
    o_ref[...] = jnp.dot(c_ref[...].astype(BF16), w_ref[...].astype(BF16), preferred_element_type=F32) + b_ref[...]


def cond_proj(cond, ada_w, ada_b):
    depth, d, n = ada_w.shape
    tn = d
    return pl.pallas_call(
        _cond_body,
        out_shape=jax.ShapeDtypeStruct((depth, cond.shape[0], n), F32),
        grid=(depth, n // tn),
        in_specs=[pl.BlockSpec(cond.shape, lambda l, j: (0, 0)),
                  pl.BlockSpec((None, d, tn), lambda l, j: (l, 0, j)),
                  pl.BlockSpec((None, 1, tn), lambda l, j: (l, 0, j))],
        out_specs=pl.BlockSpec((None, cond.shape[0], tn), lambda l, j: (l, 0, j)),
        compiler_params=_params("parallel", "parallel"),
        name="cond_proj",
    )(cond, ada_w, ada_b[:, None, :])


def _bmm_body(a_ref, x_ref, o_ref):
    o_ref[...] = jnp.dot(a_ref[...], x_ref[...].astype(BF16), preferred_element_type=F32).astype(o_ref.dtype)


def bmm_left(a, x, *, tn=256, out_dtype=F32):
    mo, k = a.shape
    b, _, d = x.shape
    assert d % tn == 0
    return pl.pallas_call(
        _bmm_body,
        out_shape=jax.ShapeDtypeStruct((b, mo, d), out_dtype),
        grid=(b, d // tn),
        in_specs=[pl.BlockSpec((mo, k), lambda i, j: (0, 0)),
                  pl.BlockSpec((None, k, tn), lambda i, j: (i, 0, j))],
        out_specs=pl.BlockSpec((None, mo, tn), lambda i, j: (i, 0, j)),
        compiler_params=_params("parallel", "parallel"),
        name="dft_mm",
    )(a, x)


def _pack_bf16_pairs(f):
    half = f.shape[1] // 2
    hi = lax.bitcast_convert_type(f[:, :half].astype(BF16).astype(F32), jnp.uint32)
    lo = lax.bitcast_convert_type(f[:, half:].astype(BF16).astype(F32), jnp.uint32)
    return hi | (lo >> 16)


def _unpack_bf16_pairs(u):
    hi = lax.bitcast_convert_type(u & jnp.uint32(0xFFFF0000), F32).astype(BF16)
    lo = lax.bitcast_convert_type(u << 16, F32).astype(BF16)
    return jnp.concatenate([hi, lo], axis=1)


def _post_epilogue(y, h, gt, sc2, sh2, ln_g, ln_b, rw_hi, rw_lo):
    hn = _layer_norm_rows(ALPHA * h + gt * y, ln_g, ln_b)
    f = hn * (1.0 + sc2) + sh2
    f_hi = f.astype(BF16)
    f_lo = (f - f_hi.astype(F32)).astype(BF16)
    lg = (jnp.dot(f_hi, rw_hi, preferred_element_type=F32) + jnp.dot(f_lo, rw_hi, preferred_element_type=F32)
          + jnp.dot(f_hi, rw_lo, preferred_element_type=F32))
    return hn, _pack_bf16_pairs(f), lg


def _post_body(z_ref, w_ref, h_ref, mod_ref, lng_ref, lnb_ref, rwh_ref, rwl_ref, ho_ref, f_ref, lg_ref):
    y = jnp.dot(z_ref[...].astype(BF16), w_ref[...], preferred_element_type=F32)
    hn, fpk, lg = _post_epilogue(y, h_ref[...], mod_ref[MOD_GT:MOD_GT + 1, :], mod_ref[MOD_SC2:MOD_SC2 + 1, :],
                                 mod_ref[MOD_SH2:MOD_SH2 + 1, :], lng_ref[...], lnb_ref[...], rwh_ref[...],
                                 rwl_ref[...])
    ho_ref[...] = hn
    f_ref[...] = fpk
    lg_ref[...] = lg


def post_mixer(z, w_out, h, modtab, ln_g, ln_b, rw_hi, rw_lo):
    t, k = z.shape
    d = w_out.shape[1]
    row = lambda i: (i, 0)
    fixed = lambda i: (0, 0)
    return pl.pallas_call(
        _post_body,
        out_shape=(jax.ShapeDtypeStruct((t, d), F32), jax.ShapeDtypeStruct((t, d // 2), jnp.uint32),
                   jax.ShapeDtypeStruct((t, ROUTER_PAD), F32)),
        grid=(t // ROW_TILE,),
        in_specs=[pl.BlockSpec((ROW_TILE, k), row),
                  pl.BlockSpec((k, d), fixed),
                  pl.BlockSpec((ROW_TILE, d), row),
                  pl.BlockSpec((None, 6, d), _mod_index),
                  pl.BlockSpec((1, d), fixed),
                  pl.BlockSpec((1, d), fixed),
                  pl.BlockSpec((d, ROUTER_PAD), fixed),
                  pl.BlockSpec((d, ROUTER_PAD), fixed)],
        out_specs=(pl.BlockSpec((ROW_TILE, d), row), pl.BlockSpec((ROW_TILE, d // 2), row),
                   pl.BlockSpec((ROW_TILE, ROUTER_PAD), row)),
        compiler_params=_params("parallel"),
        name="post_mixer",
    )(z, w_out, h, modtab, ln_g, ln_b, rw_hi, rw_lo)


def _combine_body(h_ref, ya_ref, yb_ref, w_ref, mod_ref, lng_ref, lnb_ref, o_ref):
    w = w_ref[...]
    ya = _unpack_bf16_pairs(ya_ref[...]).astype(F32)
    yb = _unpack_bf16_pairs(yb_ref[...]).astype(F32)
    y2 = w[:, 0:1] * ya + w[:, 1:2] * yb
    gt2 = mod_ref[MOD_GT2:MOD_GT2 + 1, :]
    o_ref[...] = _layer_norm_rows(ALPHA * h_ref[...] + gt2 * y2, lng_ref[...], lnb_ref[...])


def moe_combine(h, ya, yb, wts, modtab, ln_g, ln_b, latent_only=False):
    t, d = h.shape
    if latent_only:
        n_tiles = BATCH * LATENT_TILES
        row = lambda i: ((i // LATENT_TILES) * TILES_PER_BATCH + i % LATENT_TILES, 0)
        mod_index = lambda i: (2 * (i // LATENT_TILES), 0, 0)
    else:
        n_tiles = t // ROW_TILE
        row = lambda i: (i, 0)
        mod_index = _mod_index
    fixed = lambda i: (0, 0)
    return pl.pallas_call(
        _combine_body,
        out_shape=jax.ShapeDtypeStruct((n_tiles * ROW_TILE, d), F32),
        grid=(n_tiles,),
        in_specs=[pl.BlockSpec((ROW_TILE, d), row), pl.BlockSpec((ROW_TILE, d // 2), row),
                  pl.BlockSpec((ROW_TILE, d // 2), row), pl.BlockSpec((ROW_TILE, TOP_K), row),
                  pl.BlockSpec((None, 6, d), mod_index),
                  pl.BlockSpec((1, d), fixed), pl.BlockSpec((1, d), fixed)],
        out_specs=pl.BlockSpec((ROW_TILE, d), lambda i: (i, 0)),
        compiler_params=_params("parallel"),
        name="moe_combine",
    )(h, ya, yb, wts, modtab, ln_g, ln_b)


def _rg_in_body(h_ref, mod_ref, w_ref, gate_ref, rec_ref):
    tt = h_ref.shape[1]
    sc = mod_ref[MOD_SC][:, None, :]
    sh = mod_ref[MOD_SH][:, None, :]
    inp = (h_ref[...] * (1.0 + sc) + sh).reshape(BATCH * tt, D_MODEL)
    p = jnp.dot(inp.astype(BF16), w_ref[...], preferred_element_type=F32)
    gate_ref[...] = jax.nn.gelu(p[:, :D_RNN]).astype(gate_ref.dtype).reshape(BATCH, tt, D_RNN)
    for b in range(BATCH):
        rec_b = p[b * tt:(b + 1) * tt, D_RNN:]
        for j in range(RG_LANES):
            rec_ref[j, pl.ds(b, tt, stride=BATCH), :] = rec_b[:, j * 128:(j + 1) * 128]


def rg_in(h3, modtab_t, w_in):
    nblk = L_JOINT // RG_TBLK
    return pl.pallas_call(
        _rg_in_body,
        out_shape=(jax.ShapeDtypeStruct((BATCH, L_JOINT, D_RNN), BF16),
                   jax.ShapeDtypeStruct((RG_LANES, L_JOINT * BATCH, 128), F32)),
        grid=(nblk,),
        in_specs=[pl.BlockSpec((BATCH, RG_TBLK, D_MODEL), lambda i: (0, i, 0)),
                  pl.BlockSpec((None, 6, BATCH, D_MODEL), lambda i: (i // (SEQ // RG_TBLK), 0, 0, 0)),
                  pl.BlockSpec((D_MODEL, 2 * D_RNN), lambda i: (0, 0))],
        out_specs=(pl.BlockSpec((BATCH, RG_TBLK, D_RNN), lambda i: (0, i, 0)),
                   pl.BlockSpec((RG_LANES, RG_TBLK * BATCH, 128), lambda i: (0, i, 0))),
        compiler_params=_params("parallel"),
        name="rg_in",
    )(h3, modtab_t, w_in)


def _rg_block(z, i):
    nblk = L_JOINT // RG_TBLK
    nlat = SEQ // RG_TBLK
    return jnp.where(z == 0, (i + nlat) % nblk, nblk - 1 - i)


def _rg_window_start(j):
    return min(max(j - 1, 0), RG_LANES - RG_WIN)


def _rg_scan_body(prev_ref, main_ref, next_ref, cw_ref, cb_ref, wg_ref, ba_ref, bx_ref, lam_ref, ho_ref,
                  a_s, u_s, st_ref):
    z = pl.program_id(0)
    i = pl.program_id(1)
    nblk = L_JOINT // RG_TBLK
    nlat = SEQ // RG_TBLK
    blk = _rg_block(z, i)
    rows = RG_TBLK * BATCH

    @pl.when(i == 0)
    def _():
        st_ref[...] = jnp.zeros_like(st_ref)

    has_prev = jnp.where((blk == 0) | (blk == nlat), 0.0, 1.0)
    has_next = jnp.where((blk == nlat - 1) | (blk == nblk - 1), 0.0, 1.0)
    cols = []
    for j in range(RG_LANES):
        ext = jnp.concatenate([prev_ref[j] * has_prev, main_ref[j], next_ref[j] * has_next], axis=0)
        lane = slice(j * 128, (j + 1) * 128)
        acc = cb_ref[:, lane] + cw_ref[0:1, lane] * ext[0:rows]
        for k in range(1, RG_CONV):
            acc = acc + cw_ref[k:k + 1, lane] * ext[k * BATCH:k * BATCH + rows]
        cols.append(acc)
    xcb = jnp.concatenate(cols, axis=1).astype(BF16)

    log_a_unit = -RG_C * jax.nn.softplus(-lam_ref[...])
    for j in range(RG_LANES):
        lane = slice(j * 128, (j + 1) * 128)
        w0 = _rg_window_start(j) * 128
        pre = jnp.dot(xcb[:, w0:w0 + RG_WIN * 128], wg_ref[j], preferred_element_type=F32)
        r = 0.5 * (jnp.tanh(0.5 * (pre[:, :128] + ba_ref[:, lane])) + 1.0)
        g = 0.5 * (jnp.tanh(0.5 * (pre[:, 128:] + bx_ref[:, lane])) + 1.0)
        a = jnp.exp(log_a_unit[:, lane] * r)
        a_s[:, lane] = a
        u_s[:, lane] = jnp.sqrt(1.0 - a * a) * (g * cols[j])

    def step(t, carry):
        te = jnp.where(z == 0, t, RG_TBLK - 1 - t)
        r0 = pl.multiple_of(te * BATCH, BATCH)
        out = []
        for j in range(RG_LANES):
            lane = slice(j * 128, (j + 1) * 128)
            h = a_s[pl.ds(r0, BATCH), lane] * carry[j] + u_s[pl.ds(r0, BATCH), lane]
            ho_ref[j, pl.ds(r0, BATCH), :] = h
            out.append(h)
        return tuple(out)

    final = lax.fori_loop(0, RG_TBLK, step, tuple(st_ref[j] for j in range(RG_LANES)), unroll=4)
    for j in range(RG_LANES):
        st_ref[j] = final[j]


def rg_scan(rec_tm, conv_w, conv_b, w_gates, ga_b, gx_b, lam):
    nblk = L_JOINT // RG_TBLK
    rows = RG_TBLK * BATCH
    halo_prev = (RG_CONV // 2) * BATCH
    per_prev = rows // halo_prev
    fixed2 = lambda z, i: (0, 0)
    per_dir = lambda z, i: (z, 0, 0)
    return pl.pallas_call(
        _rg_scan_body,
        out_shape=jax.ShapeDtypeStruct((2, RG_LANES, L_JOINT * BATCH, 128), F32),
        grid=(2, nblk),
        in_specs=[pl.BlockSpec((RG_LANES, halo_prev, 128),
                               lambda z, i: (0, jnp.maximum(_rg_block(z, i) * per_prev - 1, 0), 0)),
                  pl.BlockSpec((RG_LANES, rows, 128), lambda z, i: (0, _rg_block(z, i), 0)),
                  pl.BlockSpec((RG_LANES, BATCH, 128),
                               lambda z, i: (0, jnp.minimum((_rg_block(z, i) + 1) * RG_TBLK, L_JOINT - 1), 0)),
                  pl.BlockSpec((RG_CONV, D_RNN), fixed2),
                  pl.BlockSpec((1, D_RNN), fixed2),
                  pl.BlockSpec((None, RG_LANES, RG_WIN * 128, 256), lambda z, i: (z, 0, 0, 0)),
                  pl.BlockSpec((None, 1, D_RNN), per_dir),
                  pl.BlockSpec((None, 1, D_RNN), per_dir),
                  pl.BlockSpec((None, 1, D_RNN), per_dir)],
        out_specs=pl.BlockSpec((None, RG_LANES, rows, 128), lambda z, i: (z, 0, _rg_block(z, i), 0)),
        scratch_shapes=[pltpu.VMEM((rows, D_RNN), F32), pltpu.VMEM((rows, D_RNN), F32),
                        pltpu.VMEM((RG_LANES, BATCH, 128), F32)],
        compiler_params=_params("arbitrary", "arbitrary"),
        name="rg_scan",
    )(rec_tm, rec_tm, rec_tm, conv_w, conv_b, w_gates, ga_b, gx_b, lam)


def _rg_out_body(gate_ref, hf_ref, hb_ref, w_ref, h_ref, mod_ref, lng_ref, lnb_ref, rwh_ref, rwl_ref,
                 ho_ref, f_ref, lg_ref):
    tt = h_ref.shape[1]
    parts = []
    for b in range(BATCH):
        hsum = jnp.concatenate([hf_ref[j, pl.ds(b, tt, stride=BATCH), :] + hb_ref[j, pl.ds(b, tt, stride=BATCH), :]
                                for j in range(RG_LANES)], axis=1)
        parts.append((gate_ref[b].astype(F32) * hsum).astype(BF16))
    zz = jnp.concatenate(parts, axis=0)
    y = jnp.dot(zz, w_ref[...], preferred_element_type=F32)
    h = h_ref[...].reshape(BATCH * tt, D_MODEL)
    rep = lambda m: jnp.broadcast_to(mod_ref[m][:, None, :], (BATCH, tt, D_MODEL)).reshape(BATCH * tt, D_MODEL)
    hn, fpk, lg = _post_epilogue(y, h, rep(MOD_GT), rep(MOD_SC2), rep(MOD_SH2), lng_ref[...], lnb_ref[...],
                                 rwh_ref[...], rwl_ref[...])
    ho_ref[...] = hn.reshape(BATCH, tt, D_MODEL)
    f_ref[...] = fpk.reshape(BATCH, tt, D_MODEL // 2)
    lg_ref[...] = lg.reshape(BATCH, tt, ROUTER_PAD)


def rg_out(gate, h_tm, w_out, h3, modtab_t, ln_g, ln_b, rw_hi, rw_lo):
    nblk = L_JOINT // RG_TBLK
    rows = RG_TBLK * BATCH
    blk3 = lambda i: (0, i, 0)
    fixed = lambda i: (0, 0)
    return pl.pallas_call(
        _rg_out_body,
        out_shape=(jax.ShapeDtypeStruct((BATCH, L_JOINT, D_MODEL), F32),
                   jax.ShapeDtypeStruct((BATCH, L_JOINT, D_MODEL // 2), jnp.uint32),
                   jax.ShapeDtypeStruct((BATCH, L_JOINT, ROUTER_PAD), F32)),
        grid=(nblk,),
        in_specs=[pl.BlockSpec((BATCH, RG_TBLK, D_RNN), blk3),
                  pl.BlockSpec((None, RG_LANES, rows, 128), lambda i: (0, 0, i, 0)),
                  pl.BlockSpec((None, RG_LANES, rows, 128), lambda i: (1, 0, i, 0)),
                  pl.BlockSpec((D_RNN, D_MODEL), fixed),
                  pl.BlockSpec((BATCH, RG_TBLK, D_MODEL), blk3),
                  pl.BlockSpec((None, 6, BATCH, D_MODEL), lambda i: (i // (SEQ // RG_TBLK), 0, 0, 0)),
                  pl.BlockSpec((1, D_MODEL), fixed), pl.BlockSpec((1, D_MODEL), fixed),
                  pl.BlockSpec((D_MODEL, ROUTER_PAD), fixed), pl.BlockSpec((D_MODEL, ROUTER_PAD), fixed)],
        out_specs=(pl.BlockSpec((BATCH, RG_TBLK, D_MODEL), blk3),
                   pl.BlockSpec((BATCH, RG_TBLK, D_MODEL // 2), blk3),
                   pl.BlockSpec((BATCH, RG_TBLK, ROUTER_PAD), blk3)),
        compiler_params=_params("parallel"),
        name="rg_out",
    )(gate, h_tm, h_tm, w_out, h3, modtab_t, ln_g, ln_b, rw_hi, rw_lo)


def _mlstm_body(q_ref, k_ref, v_ref, gc_ref, gr_ref, ng_ref, o_ref, hf_s, hb_s, gb_s, gr_s, ct_ref, n_ref, m_ref):
    row = lax.broadcasted_iota(jnp.int32, (CHUNK, CHUNK), 0)
    col = lax.broadcasted_iota(jnp.int32, (CHUNK, CHUNK), 1)
    lower = col <= row
    upper = col >= row
    tri_lower = jnp.where(lower, 1.0, 0.0).astype(BF16)
    tri_upper = jnp.where(upper, 1.0, 0.0).astype(BF16)
    ones_sq = jnp.ones((CHUNK, CHUNK), BF16)
    hp = ML_HEADS_PER_STEP
    tile_order = [(z, hh, kind) for z in range(2) for hh in range(hp) for kind in range(2)]
    n_src = 12 * hp
    src_row = lax.broadcasted_iota(jnp.int32, (2 * n_src, CHUNK), 0) % n_src
    spread = jnp.concatenate(
        [jnp.where(src_row == (4 * hh + 2 * z + (1 - kind) * (1 + 4 * hp * (1 + z))), 1.0, 0.0)
         for z, hh, kind in tile_order], axis=1).astype(BF16)

    def _hi_lo(x):
        hi = x.astype(BF16)
        return hi, (x - hi.astype(F32)).astype(BF16)

    def chunk(c, z, hh):
        reverse = z == 1
        c0 = pl.multiple_of(c * CHUNK, CHUNK)
        q = q_ref[pl.ds(c0, CHUNK), hh * M_DK:(hh + 1) * M_DK]
        k = k_ref[pl.ds(c0, CHUNK), hh * M_DK:(hh + 1) * M_DK]
        v = v_ref[pl.ds(c0, CHUNK), hh * M_DV:(hh + 1) * M_DV]
        mask = upper if reverse else lower
        bc_b = gb_s[z, hh, 0, pl.ds(c0, CHUNK), :]
        ig_b = gb_s[z, hh, 1, pl.ds(c0, CHUNK), :]
        r0 = 2 * (z * ML_HEADS_PER_STEP + hh)
        bc_r = gr_s[r0:r0 + 1, pl.ds(c0, CHUNK)]
        ig_r = gr_s[r0 + 1:r0 + 2, pl.ds(c0, CHUNK)]
        btot = bc_b[0:1, :] if reverse else bc_b[CHUNK - 1:CHUNK, :]
        m_prev = m_ref[z, hh]
        n_prev = n_ref[z, hh]
        ct_prev = ct_ref[z, hh]
        two = lambda a: jnp.concatenate([a, a], axis=1)
        dlog = jnp.where(mask, bc_b - bc_r + ig_r, -jnp.inf)
        m_inter = bc_b + m_prev
        m_comb = jnp.maximum(m_inter, jnp.max(dlog, axis=1, keepdims=True))
        qk = lax.dot_general(q, k, (((1,), (1,)), ((), ())), preferred_element_type=F32)
        s = qk * jnp.exp(dlog - m_comb)
        inter = jnp.exp(m_inter - m_comb)
        sb = s.astype(BF16)
        num = (jnp.dot(sb, v, preferred_element_type=F32)
               + two(inter) * jnp.dot(q, ct_prev.astype(BF16), preferred_element_type=F32))
        s_sum = jnp.dot(sb, ones_sq, preferred_element_type=F32)
        n_rows = jnp.broadcast_to(n_prev, (CHUNK, M_DK)).astype(BF16)
        qn = lax.dot_general(q, n_rows, (((1,), (1,)), ((), ())), preferred_element_type=F32)
        den = s_sum + inter * qn
        h = num * two(1.0 / jnp.maximum(jnp.abs(den), jnp.exp(-m_comb)))
        wlog = btot - bc_b + ig_b
        mloc = jnp.max(wlog, axis=0, keepdims=True)
        wgt = jnp.exp(wlog - mloc)
        m_new = jnp.maximum(btot + m_prev, mloc)
        sp = jnp.exp(btot + m_prev - m_new)
        sl = jnp.exp(mloc - m_new)
        kf = k.astype(F32)
        vw = (v.astype(F32) * two(wgt)).astype(BF16)
        ct_loc = jnp.dot(kf.T.astype(BF16), vw, preferred_element_type=F32)
        ct_ref[z, hh] = two(sp) * ct_prev + two(sl) * ct_loc
        n_ref[z, hh] = sp * n_prev + sl * jnp.sum(kf * wgt, axis=0, keepdims=True)
        m_ref[z, hh] = m_new
        return c0, h

    ct_ref[...] = jnp.zeros_like(ct_ref)
    n_ref[...] = jnp.zeros_like(n_ref)
    m_ref[...] = jnp.zeros_like(m_ref)

    n_chunks = o_ref.shape[0] // CHUNK
    n_lat = SEQ // CHUNK

    def gate_sums(c, carry):
        c0 = pl.multiple_of(c * CHUNK, CHUNK)
        gc = jnp.concatenate([gc_ref[hh, pl.ds(c0, CHUNK), :] for hh in range(hp)], axis=1)
        gr = jnp.concatenate([gr_ref[hh, :, pl.ds(c0, CHUNK)] for hh in range(hp)], axis=0)
        gc_parts = jnp.concatenate(_hi_lo(gc), axis=0)
        cum_c = [jnp.dot(jnp.concatenate([t, t], axis=1), gc_parts, preferred_element_type=F32)
                 for t in (tri_lower, tri_upper)]
        gr_parts = jnp.concatenate(_hi_lo(gr), axis=1)
        cum_r = [jnp.dot(gr_parts, jnp.concatenate([t, t], axis=0), preferred_element_type=F32)
                 for t in (tri_upper, tri_lower)]
        src = jnp.concatenate([gc, cum_c[0], cum_c[1]], axis=1)
        tiles = jnp.dot(jnp.concatenate(_hi_lo(src), axis=1), spread, preferred_element_type=F32)
        for n, (z, hh, kind) in enumerate(tile_order):
            gb_s[z, hh, kind, pl.ds(c0, CHUNK), :] = tiles[:, n * CHUNK:(n + 1) * CHUNK]
            r0 = 2 * (z * hp + hh) + kind
            col = 4 * hh + 2 * z + 1 - kind
            row = cum_r[z][col:col + 1, :] if kind == 0 else gr[col:col + 1, :]
            gr_s[r0:r0 + 1, pl.ds(c0, CHUNK)] = row
        return carry

    lax.fori_loop(0, n_chunks, gate_sums, 0, unroll=3)

    def both_directions(i, carry):
        for hh in range(ML_HEADS_PER_STEP):
            c0, h = chunk((i + n_lat) % n_chunks, 0, hh)
            hf_s[hh, pl.ds(c0, CHUNK), :] = h
            c0, h = chunk(n_chunks - 1 - i, 1, hh)
            hb_s[hh, pl.ds(c0, CHUNK), :] = h
        return carry

    lax.fori_loop(0, n_chunks, both_directions, 0)

    def head_norm(c, carry):
        c0 = pl.multiple_of(c * CHUNK, CHUNK)
        for hh in range(ML_HEADS_PER_STEP):
            tot = hf_s[hh, pl.ds(c0, CHUNK), :] + hb_s[hh, pl.ds(c0, CHUNK), :]
            mu = jnp.mean(tot, axis=1, keepdims=True)
            var = jnp.mean(jnp.square(tot - mu), axis=1, keepdims=True)
            lanes = slice(hh * M_DV, (hh + 1) * M_DV)
            o_ref[pl.ds(c0, CHUNK), lanes] = ((tot - mu) * lax.rsqrt(var + LN_EPS) * ng_ref[:, lanes]
                                             ).astype(o_ref.dtype)
        return carry

    lax.fori_loop(0, n_chunks, head_norm, 0)


def mlstm_cell(q, k, v, g_col, g_row, norm_g):
    b, lj, _ = q.shape
    hp = ML_HEADS_PER_STEP
    return pl.pallas_call(
        _mlstm_body,
        out_shape=jax.ShapeDtypeStruct((b, lj, M_HEADS * M_DV), BF16),
        grid=(b, M_HEADS // hp),
        in_specs=[pl.BlockSpec((None, lj, hp * M_DK), lambda i, h: (i, 0, h)),
                  pl.BlockSpec((None, lj, hp * M_DK), lambda i, h: (i, 0, h)),
                  pl.BlockSpec((None, lj, hp * M_DV), lambda i, h: (i, 0, h)),
                  pl.BlockSpec((None, hp, lj, 4), lambda i, h: (i, h, 0, 0)),
                  pl.BlockSpec((None, hp, 4, lj), lambda i, h: (i, h, 0, 0)),
                  pl.BlockSpec((1, hp * M_DV), lambda i, h: (0, h))],
        out_specs=pl.BlockSpec((None, lj, hp * M_DV), lambda i, h: (i, 0, h)),
        scratch_shapes=[pltpu.VMEM((hp, lj, M_DV), F32), pltpu.VMEM((hp, lj, M_DV), F32),
                        pltpu.VMEM((2, hp, 2, lj, CHUNK), F32), pltpu.VMEM((4 * hp, lj), F32),
                        pltpu.VMEM((2, hp, M_DK, M_DV), F32), pltpu.VMEM((2, hp, 1, M_DK), F32),
                        pltpu.VMEM((2, hp, 1, CHUNK), F32)],
        compiler_params=_params("parallel", "parallel"),
        name="mlstm_cell",
    )(q, k, v, g_col, g_row, norm_g)


def _tile_neighbours(i):
    r = i % TILES_PER_BATCH
    has_prev = jnp.where((r == 0) | (r == LATENT_TILES), 0.0, 1.0)
    has_next = jnp.where((r == LATENT_TILES - 1) | (r == TILES_PER_BATCH - 1), 0.0, 1.0)
    return has_prev, has_next


def _ml_up_body(h_ref, mod_ref, w_ref, o_ref):
    inp = h_ref[...] * (1.0 + mod_ref[MOD_SC:MOD_SC + 1, :]) + mod_ref[MOD_SH:MOD_SH + 1, :]
    o_ref[...] = jnp.dot(inp.astype(BF16), w_ref[...], preferred_element_type=F32).astype(o_ref.dtype)


def mod_proj(h, modtab, w, bias=None):
    t, d = h.shape
    n = w.shape[1]
    body = _ml_up_body
    args = [h, modtab, w]
    in_specs = [pl.BlockSpec((ROW_TILE, d), lambda i: (i, 0)),
                pl.BlockSpec((None, 6, d), _mod_index),
                pl.BlockSpec((d, n), lambda i: (0, 0))]
    if bias is not None:
        def body(h_ref, mod_ref, w_ref, b_ref, o_ref):
            inp = h_ref[...] * (1.0 + mod_ref[MOD_SC:MOD_SC + 1, :]) + mod_ref[MOD_SH:MOD_SH + 1, :]
            y = jnp.dot(inp.astype(BF16), w_ref[...], preferred_element_type=F32) + b_ref[...]
            o_ref[...] = y.astype(o_ref.dtype)
        args.append(bias)
        in_specs.append(pl.BlockSpec((1, n), lambda i: (0, 0)))
    return pl.pallas_call(
        body,
        out_shape=jax.ShapeDtypeStruct((t, n), BF16),
        grid=(t // ROW_TILE,),
        in_specs=in_specs,
        out_specs=pl.BlockSpec((ROW_TILE, n), lambda i: (i, 0)),
        compiler_params=_params("parallel"),
        name="mod_proj",
    )(*args)


def _ml_qk_body(prev_ref, main_ref, next_ref, cw_ref, cb_ref, w_ref, xc_ref, q_ref, k_ref, ext_s):
    has_prev, has_next = _tile_neighbours(pl.program_id(0))
    ext_s[0:CONV_HALO, :] = prev_ref[...].astype(F32) * has_prev
    ext_s[CONV_HALO:CONV_HALO + ROW_TILE, :] = main_ref[...].astype(F32)
    ext_s[CONV_HALO + ROW_TILE:, :] = next_ref[...].astype(F32) * has_next
    kk = cw_ref.shape[0]
    acc = cb_ref[...] + cw_ref[0:1, :] * ext_s[pl.ds(CONV_HALO - kk // 2, ROW_TILE), :]
    for j in range(1, kk):
        acc = acc + cw_ref[j:j + 1, :] * ext_s[pl.ds(CONV_HALO - kk // 2 + j, ROW_TILE), :]
    xc = (acc * jax.nn.sigmoid(acc)).astype(BF16)
    xc_ref[...] = xc
    qk = jnp.dot(xc, w_ref[...], preferred_element_type=F32)
    nq = q_ref.shape[1]
    q_ref[...] = qk[:, :nq].astype(BF16)
    k_ref[...] = (qk[:, nq:] * (M_DK ** -0.5)).astype(BF16)


def ml_qk(xm, conv_w, conv_b, w_qk):
    t, dm = xm.shape
    nq = M_HEADS * M_DK
    per = ROW_TILE // CONV_HALO
    last = t // CONV_HALO - 1
    row = lambda i: (i, 0)
    fixed = lambda i: (0, 0)
    return pl.pallas_call(
        _ml_qk_body,
        out_shape=(jax.ShapeDtypeStruct((t, dm), BF16), jax.ShapeDtypeStruct((t, nq), BF16),
                   jax.ShapeDtypeStruct((t, nq), BF16)),
        grid=(t // ROW_TILE,),
        in_specs=[pl.BlockSpec((CONV_HALO, dm), lambda i: (jnp.maximum(i * per - 1, 0), 0)),
                  pl.BlockSpec((ROW_TILE, dm), row),
                  pl.BlockSpec((CONV_HALO, dm), lambda i: (jnp.minimum((i + 1) * per, last), 0)),
                  pl.BlockSpec(conv_w.shape, fixed), pl.BlockSpec((1, dm), fixed),
                  pl.BlockSpec((dm, 2 * nq), fixed)],
        out_specs=(pl.BlockSpec((ROW_TILE, dm), row), pl.BlockSpec((ROW_TILE, nq), row),
                   pl.BlockSpec((ROW_TILE, nq), row)),
        scratch_shapes=[pltpu.VMEM((ROW_TILE + 2 * CONV_HALO, dm), F32)],
        compiler_params=_params("parallel"),
        name="ml_qk",
    )(xm, xm, xm, conv_w, conv_b, w_qk)


def _mm_act_body(x_ref, w_ref, o_ref, *, act):
    y = jnp.dot(x_ref[...], w_ref[...], preferred_element_type=F32)
    if act == "sigmoid":
        y = jax.nn.sigmoid(y)
    o_ref[...] = y.astype(o_ref.dtype)


def mm_act(x, w, act=None, *, tm=512, tn=1024):
    m, k = x.shape
    n = w.shape[1]
    return pl.pallas_call(
        functools.partial(_mm_act_body, act=act),
        out_shape=jax.ShapeDtypeStruct((m, n), BF16),
        grid=(n // tn, m // tm),
        in_specs=[pl.BlockSpec((tm, k), lambda j, i: (i, 0)),
                  pl.BlockSpec((k, tn), lambda j, i: (0, j))],
        out_specs=pl.BlockSpec((tm, tn), lambda j, i: (i, j)),
        compiler_params=_params("parallel", "parallel"),
        name="mm_act",
    )(x, w)


def _ml_gates_body(x_ref, w_ref, b_ref, o_ref):
    g = jnp.dot(x_ref[...], w_ref[...], preferred_element_type=F32) + b_ref[...]
    lane = lax.broadcasted_iota(jnp.int32, g.shape, 1)
    is_forget = (lane % (2 * M_HEADS)) >= M_HEADS
    o_ref[...] = jnp.where(is_forget, jax.nn.log_sigmoid(g), g)


def ml_gates(xm, w_g, b_g):
    t, dm = xm.shape
    return pl.pallas_call(
        _ml_gates_body,
        out_shape=jax.ShapeDtypeStruct((t, 128), F32),
        grid=(t // 512,),
        in_specs=[pl.BlockSpec((512, dm), lambda i: (i, 0)), pl.BlockSpec((dm, 128), lambda i: (0, 0)),
                  pl.BlockSpec((1, 128), lambda i: (0, 0))],
        out_specs=pl.BlockSpec((512, 128), lambda i: (i, 0)),
        compiler_params=_params("parallel"),
        name="ml_gates",
    )(xm, w_g, b_g)


def _post_ml_body(o_ref, hn_ref, xc_ref, skip_ref, w_ref, h_ref, mod_ref, lng_ref, lnb_ref, rwh_ref, rwl_ref,
                  ho_ref, f_ref, lg_ref):
    z = o_ref[...].astype(F32) * hn_ref[...].astype(F32) + skip_ref[...] * xc_ref[...].astype(F32)
    y = jnp.dot(z.astype(BF16), w_ref[...], preferred_element_type=F32)
    hn, fpk, lg = _post_epilogue(y, h_ref[...], mod_ref[MOD_GT:MOD_GT + 1, :], mod_ref[MOD_SC2:MOD_SC2 + 1, :],
                                 mod_ref[MOD_SH2:MOD_SH2 + 1, :], lng_ref[...], lnb_ref[...], rwh_ref[...],
                                 rwl_ref[...])
    ho_ref[...] = hn
    f_ref[...] = fpk
    lg_ref[...] = lg


def post_mixer_ml(o, hn, xc, skip, w_out, h, modtab, ln_g, ln_b, rw_hi, rw_lo):
    t, k = o.shape
    d = w_out.shape[1]
    row = lambda i: (i, 0)
    fixed = lambda i: (0, 0)
    return pl.pallas_call(
        _post_ml_body,
        out_shape=(jax.ShapeDtypeStruct((t, d), F32), jax.ShapeDtypeStruct((t, d // 2), jnp.uint32),
                   jax.ShapeDtypeStruct((t, ROUTER_PAD), F32)),
        grid=(t // ROW_TILE,),
        in_specs=[pl.BlockSpec((ROW_TILE, k), row), pl.BlockSpec((ROW_TILE, k), row),
                  pl.BlockSpec((ROW_TILE, k), row), pl.BlockSpec((1, k), fixed),
                  pl.BlockSpec((k, d), fixed),
                  pl.BlockSpec((ROW_TILE, d), row),
                  pl.BlockSpec((None, 6, d), _mod_index),
                  pl.BlockSpec((1, d), fixed), pl.BlockSpec((1, d), fixed),
                  pl.BlockSpec((d, ROUTER_PAD), fixed), pl.BlockSpec((d, ROUTER_PAD), fixed)],
        out_specs=(pl.BlockSpec((ROW_TILE, d), row), pl.BlockSpec((ROW_TILE, d // 2), row),
                   pl.BlockSpec((ROW_TILE, ROUTER_PAD), row)),
        compiler_params=_params("parallel"),
        name="post_mixer_ml",
    )(o, hn, xc, skip, w_out, h, modtab, ln_g, ln_b, rw_hi, rw_lo)


def _hy_conv(u_ref, cw_ref, cb_ref, ext_s):
    n_time = u_ref.shape[0]
    ext_s[0:8, :] = jnp.zeros((8, ext_s.shape[1]), F32)
    ext_s[8:8 + n_time, :] = u_ref[...].astype(F32)
    ext_s[8 + n_time:, :] = jnp.zeros((8, ext_s.shape[1]), F32)
    kk = cw_ref.shape[0]
    acc = cb_ref[...] + cw_ref[0:1, :] * ext_s[pl.ds(8 - kk // 2, n_time), :]
    for j in range(1, kk):
        acc = acc + cw_ref[j:j + 1, :] * ext_s[pl.ds(8 - kk // 2 + j, n_time), :]
    return acc


def _hy_spectrum(fm_ref, sig, hf_ref, y_ref):
    x = jnp.dot(fm_ref[...], sig, preferred_element_type=F32)
    half = x.shape[0] // 2
    xt, xb = x[:half], x[half:]
    ht, hb = hf_ref[0:half, :], hf_ref[half:, :]
    first = lax.broadcasted_iota(jnp.int32, xt.shape, 0) == 0
    y_ref[0:half, :] = jnp.where(first, xt * ht, xt * ht - xb * hb).astype(y_ref.dtype)
    y_ref[half:, :] = jnp.where(first, xb * hb, xt * hb + xb * ht).astype(y_ref.dtype)


def _hy_fwd_conv_body(u_ref, cw_ref, cb_ref, fm_ref, hf_ref, v_ref, y_ref, ext_s):
    v = _hy_conv(u_ref, cw_ref, cb_ref, ext_s).astype(BF16)
    v_ref[...] = v
    _hy_spectrum(fm_ref, v, hf_ref, y_ref)


def _hy_fwd_body(s_ref, fm_ref, hf_ref, y_ref):
    _hy_spectrum(fm_ref, s_ref[...], hf_ref, y_ref)


def _hy_inv_body(y_ref, gm_ref, u_ref, cw_ref, cb_ref, s_ref, skip_ref, z_ref, ext_s):
    y = jnp.dot(gm_ref[...], y_ref[...], preferred_element_type=F32)
    g = _hy_conv(u_ref, cw_ref, cb_ref, ext_s)
    z_ref[...] = (g * (y + s_ref[...].astype(F32) * skip_ref[...])).astype(z_ref.dtype)


def hyena_segment(u3, seg_block, n_time, conv_w, conv_b, hf, skip):
    b = u3.shape[0]
    tn = HY_TN
    nj = D_MODEL // tn
    fm, gm = _dft_mats(n_time)
    n = fm.shape[0]
    grid = (nj, b)
    u_spec = lambda part: pl.BlockSpec((None, n_time, tn), lambda j, i: (i, seg_block, part * nj + j))
    cw_spec = lambda part: pl.BlockSpec((conv_w.shape[0], tn), lambda j, i: (0, part * nj + j))
    cb_spec = lambda part: pl.BlockSpec((1, tn), lambda j, i: (0, part * nj + j))
    fixed = lambda j, i: (0, 0)
    sig_spec = pl.BlockSpec((None, n_time, tn), lambda j, i: (i, 0, j))
    spec_spec = pl.BlockSpec((None, n, tn), lambda j, i: (i, 0, j))
    hf_spec = lambda c: pl.BlockSpec((n, tn), lambda j, i: (0, c * nj + j))
    skip_spec = lambda c: pl.BlockSpec((None, 1, tn), lambda j, i: (c, 0, j))
    ext = pltpu.VMEM((n_time + 16, tn), F32)
    sig_shape = jax.ShapeDtypeStruct((b, n_time, D_MODEL), BF16)
    spec_shape = jax.ShapeDtypeStruct((b, n, D_MODEL), BF16)
    par = _params("parallel", "parallel")
    skip3 = skip[:, None, :]

    v, y1 = pl.pallas_call(
        _hy_fwd_conv_body, out_shape=(sig_shape, spec_shape), grid=grid,
        in_specs=[u_spec(0), cw_spec(0), cb_spec(0), pl.BlockSpec(fm.shape, fixed), hf_spec(0)],
        out_specs=(sig_spec, spec_spec), scratch_shapes=[ext], compiler_params=par, name="hy_fwd_conv",
    )(u3, conv_w, conv_b, fm, hf)
    z1 = pl.pallas_call(
        _hy_inv_body, out_shape=sig_shape, grid=grid,
        in_specs=[spec_spec, pl.BlockSpec(gm.shape, fixed), u_spec(1), cw_spec(1), cb_spec(1), sig_spec,
                  skip_spec(0)],
        out_specs=sig_spec, scratch_shapes=[ext], compiler_params=par, name="hy_inv",
    )(y1, gm, u3, conv_w, conv_b, v, skip3)
    y2 = pl.pallas_call(
        _hy_fwd_body, out_shape=spec_shape, grid=grid,
        in_specs=[sig_spec, pl.BlockSpec(fm.shape, fixed), hf_spec(1)],
        out_specs=spec_spec, compiler_params=par, name="hy_fwd",
    )(z1, fm, hf)
    return pl.pallas_call(
        _hy_inv_body, out_shape=sig_shape, grid=grid,
        in_specs=[spec_spec, pl.BlockSpec(gm.shape, fixed), u_spec(2), cw_spec(2), cb_spec(2), sig_spec,
                  skip_spec(1)],
        out_specs=sig_spec, scratch_shapes=[ext], compiler_params=par, name="hy_inv",
    )(y2, gm, u3, conv_w, conv_b, z1, skip3)


def _moe_body(te_ref, tf_ref, nu_ref, x_ref, wg_ref, wu_ref, wd_ref, o_ref, wg_s, wu_s, wd_s):
    i = pl.program_id(0)

    @pl.when(i < nu_ref[0])
    def _():
        @pl.when(tf_ref[i] == 1)
        def _():
            wg_s[...] = wg_ref[...].astype(BF16)
            wu_s[...] = wu_ref[...].astype(BF16)
            wd_s[...] = wd_ref[...].astype(BF16)

        x = _unpack_bf16_pairs(x_ref[...])
        a = jnp.dot(x, wg_s[...], preferred_element_type=F32)
        u = jnp.dot(x, wu_s[...], preferred_element_type=F32)
        hid = (a * jax.nn.sigmoid(a)) * u
        o_ref[...] = _pack_bf16_pairs(jnp.dot(hid.astype(BF16), wd_s[...], preferred_element_type=F32))

    @pl.when(i >= nu_ref[0])
    def _():
        o_ref[...] = jnp.zeros_like(o_ref)


def moe_experts(layer, tile_expert, tile_first, n_used, xs, w_gate, w_up, w_down):
    n_rows = xs.shape[0]
    d = 2 * xs.shape[1]
    n_tiles = n_rows // MOE_TILE
    wmap_in = lambda i, te, tf, nu: (layer, te[i], 0, 0)
    grid_spec = pltpu.PrefetchScalarGridSpec(
        num_scalar_prefetch=3,
        grid=(n_tiles,),
        in_specs=[pl.BlockSpec((MOE_TILE, d // 2), lambda i, te, tf, nu: (i, 0)),
                  pl.BlockSpec((None, None, d, D_EXPERT), wmap_in),
                  pl.BlockSpec((None, None, d, D_EXPERT), wmap_in),
                  pl.BlockSpec((None, None, D_EXPERT, d), wmap_in)],
        out_specs=pl.BlockSpec((MOE_TILE, d // 2), lambda i, te, tf, nu: (i, 0)),
        scratch_shapes=[pltpu.VMEM((d, D_EXPERT), BF16), pltpu.VMEM((d, D_EXPERT), BF16),
                        pltpu.VMEM((D_EXPERT, d), BF16)],
    )
    return pl.pallas_call(
        _moe_body,
        out_shape=jax.ShapeDtypeStruct((n_rows, d // 2), jnp.uint32),
        grid_spec=grid_spec,
        compiler_params=_params("arbitrary"),
        name="moe_experts",
    )(tile_expert, tile_first, n_used, xs, w_gate, w_up, w_down)


def _first_max4(a):
    m = jnp.maximum(jnp.maximum(a[0], a[1]), jnp.maximum(a[2], a[3]))
    idx = jnp.where(a[0] == m, 0, jnp.where(a[1] == m, 1, jnp.where(a[2] == m, 2, 3))).astype(jnp.int32)
    return m, idx


def _router_body(lg_ref, rb_ref, dest_ref, wts_ref, te_ref, tf_ref, nu_ref, lgt_s, cnt_s, pre_s):
    t = lg_ref.shape[0]
    n_chunk = t // 128

    def transpose(c, carry):
        c0 = pl.multiple_of(c * 128, 128)
        lgt_s[:, pl.ds(c0, 128)] = lg_ref[pl.ds(c0, 128), :].T[0:N_EXPERTS, :]
        return carry

    lax.fori_loop(0, n_chunk, transpose, 0)

    s = jax.nn.sigmoid(lgt_s[...])
    sel = s + rb_ref[...]
    gs, i1s, i2s = [], [], []
    for g in range(N_GROUPS):
        a = [sel[g * EXP_PER_GROUP + j:g * EXP_PER_GROUP + j + 1, :] for j in range(EXP_PER_GROUP)]
        m1, i1 = _first_max4(a)
        m2, i2 = _first_max4([jnp.where(i1 == j, -jnp.inf, a[j]) for j in range(EXP_PER_GROUP)])
        gs.append(m1 + m2)
        i1s.append(i1)
        i2s.append(i2)
    _, gb = _first_max4(gs)
    pick = lambda v: jnp.where(gb == 0, v[0], jnp.where(gb == 1, v[1], jnp.where(gb == 2, v[2], v[3])))
    e1 = gb * EXP_PER_GROUP + pick(i1s)
    e2 = gb * EXP_PER_GROUP + pick(i2s)
    eid = lax.broadcasted_iota(jnp.int32, (N_EXPERTS, t), 0)
    is1 = eid == e1
    is2 = eid == e2
    w1 = jnp.sum(jnp.where(is1, s, 0.0), axis=0, keepdims=True)
    w2 = jnp.sum(jnp.where(is2, s, 0.0), axis=0, keepdims=True)
    tot = w1 + w2
    wts_ref[...] = jnp.concatenate([w1 / tot, w2 / tot, jnp.zeros((6, t), F32)], axis=0)

    cnt_s[...] = jnp.where(is1 | is2, 1.0, 0.0).astype(BF16)
    r = lax.broadcasted_iota(jnp.int32, (128, 128), 0)
    c = lax.broadcasted_iota(jnp.int32, (128, 128), 1)
    before = jnp.where(r < c, 1.0, 0.0).astype(BF16)
    ones = jnp.ones((128, 128), BF16)

    def prefix(ci, carry):
        c0 = pl.multiple_of(ci * 128, 128)
        blk = cnt_s[:, pl.ds(c0, 128)]
        pre_s[:, pl.ds(c0, 128)] = carry + jnp.dot(blk, before, preferred_element_type=F32)
        return carry + jnp.dot(blk, ones, preferred_element_type=F32)

    counts = lax.fori_loop(0, n_chunk, prefix, jnp.zeros((N_EXPERTS, 128), F32))
    tiles_per = jnp.floor((counts + (MOE_TILE - 1)) * (1.0 / MOE_TILE))
    er = lax.broadcasted_iota(jnp.int32, (N_EXPERTS, N_EXPERTS), 0)
    ec = lax.broadcasted_iota(jnp.int32, (N_EXPERTS, N_EXPERTS), 1)
    earlier = jnp.where(ec < er, 1.0, 0.0).astype(BF16)
    tile_start = jnp.dot(earlier, tiles_per.astype(BF16), preferred_element_type=F32)
    tile_end = tile_start + tiles_per
    slot = tile_start[:, 0:1] * MOE_TILE + pre_s[...]
    d1 = jnp.sum(jnp.where(is1, slot, 0.0), axis=0, keepdims=True)
    d2 = jnp.sum(jnp.where(is2, slot, 0.0), axis=0, keepdims=True)
    dest_ref[...] = jnp.concatenate([d1, d2, jnp.zeros((6, t), F32)], axis=0).astype(jnp.int32)

    n_used = tile_end[N_EXPERTS - 1:N_EXPERTS, 0:1]
    tid = lax.broadcasted_iota(jnp.int32, (N_EXPERTS, te_ref.shape[1]), 1).astype(F32)
    tid_c = jnp.minimum(tid, n_used - 1.0)
    t_exp = jnp.sum(jnp.where(tile_end[:, 0:1] <= tid_c, 1.0, 0.0), axis=0, keepdims=True)
    t_exp = jnp.minimum(t_exp, N_EXPERTS - 1.0)
    t_first = jnp.sum(jnp.where((tile_start[:, 0:1] == tid) & (tiles_per[:, 0:1] > 0.0), 1.0, 0.0),
                      axis=0, keepdims=True)
    te_ref[...] = jnp.broadcast_to(t_exp, te_ref.shape).astype(jnp.int32)
    tf_ref[...] = jnp.broadcast_to(t_first, tf_ref.shape).astype(jnp.int32)
    nu_ref[...] = jnp.broadcast_to(n_used, nu_ref.shape).astype(jnp.int32)


def moe_router(logits, router_b):
    t = logits.shape[0]
    n_tab = 256
    assert t * TOP_K // MOE_TILE + N_EXPERTS <= n_tab
    full = lambda shape: pl.BlockSpec(shape, lambda i: (0, 0))
    dest, wts, te, tf, nu = pl.pallas_call(
        _router_body,
        out_shape=(jax.ShapeDtypeStruct((8, t), jnp.int32), jax.ShapeDtypeStruct((8, t), F32),
                   jax.ShapeDtypeStruct((8, n_tab), jnp.int32), jax.ShapeDtypeStruct((8, n_tab), jnp.int32),
                   jax.ShapeDtypeStruct((8, 128), jnp.int32)),
        grid=(1,),
        in_specs=[full((t, ROUTER_PAD)), full((N_EXPERTS, 1))],
        out_specs=(full((8, t)), full((8, t)), full((8, n_tab)), full((8, n_tab)), full((8, 128))),
        scratch_shapes=[pltpu.VMEM((N_EXPERTS, t), F32), pltpu.VMEM((N_EXPERTS, t), BF16),
                        pltpu.VMEM((N_EXPERTS, t), F32)],
        compiler_params=_params("arbitrary"),
        name="moe_router",
    )(logits, router_b[:, None])
    return dest[:TOP_K], wts[:TOP_K], te[0], tf[0], nu[0, :1]


def moe(layer, f, logits, router_b, w_gate, w_up, w_down):
    t = f.shape[0]
    n_tiles = t * TOP_K // MOE_TILE + N_EXPERTS
    n_rows = n_tiles * MOE_TILE
    dest, wts, tile_expert, tile_first, n_used = moe_router(logits, router_b)
    token = jnp.arange(t, dtype=jnp.int32)
    row_token = jnp.zeros((n_rows,), jnp.int32).at[dest.reshape(-1)].set(
        jnp.concatenate([token, token]), unique_indices=True)
    xs = jnp.take(f, row_token, axis=0)
    ys = moe_experts(layer, tile_expert[:n_tiles], tile_first[:n_tiles], n_used, xs, w_gate, w_up, w_down)
    return jnp.take(ys, dest[0], axis=0), jnp.take(ys, dest[1], axis=0), wts.T


def sincos_2d(n_tok):
    rows = n_tok // GRID_W
    quarter = D_MODEL // 4
    omega = 1.0 / (10000.0 ** (jnp.arange(quarter, dtype=F32) / quarter))
    ar = jnp.arange(rows, dtype=F32)[:, None] * omega
    ac = jnp.arange(GRID_W, dtype=F32)[:, None] * omega
    er = jnp.concatenate([jnp.sin(ar), jnp.cos(ar)], -1)
    ec = jnp.concatenate([jnp.sin(ac), jnp.cos(ac)], -1)
    half = D_MODEL // 2
    pos = jnp.concatenate([jnp.broadcast_to(er[:, None], (rows, GRID_W, half)),
                           jnp.broadcast_to(ec[None], (rows, GRID_W, half))], -1)
    return pos.reshape(rows * GRID_W, D_MODEL)


def rglru_layer(h3, modtab_t, w_in, conv_w, conv_b, ga_w, ga_b, gx_w, gx_b, lam, w_out, ln_g, ln_b, rw_hi, rw_lo):
    def block_diag(w):
        eye = jnp.eye(RG_BLOCKS, dtype=w.dtype)
        return jnp.einsum("nkj,nm->nkmj", w, eye).reshape(D_RNN, D_RNN)

    def banded(wa, wx):
        out = []
        for jj in range(RG_LANES):
            r0 = _rg_window_start(jj) * 128
            cs = slice(jj * 128, (jj + 1) * 128)
            out.append(jnp.concatenate([wa[r0:r0 + RG_WIN * 128, cs], wx[r0:r0 + RG_WIN * 128, cs]], axis=1))
        return jnp.stack(out)

    w_gates = jnp.stack([banded(block_diag(ga_w[z]), block_diag(gx_w[z]))
                         for z in range(2)]).astype(BF16)
    gate, rec_tm = rg_in(h3, modtab_t, w_in.astype(BF16))
    h_tm = rg_scan(rec_tm, conv_w, conv_b[None], w_gates, ga_b[:, None], gx_b[:, None], lam[:, None])
    hn, fpk, lg = rg_out(gate, h_tm, w_out.astype(BF16), h3, modtab_t, ln_g, ln_b, rw_hi, rw_lo)
    rows = BATCH * L_JOINT
    return hn.reshape(rows, D_MODEL), fpk.reshape(rows, D_MODEL // 2), lg.reshape(rows, ROUTER_PAD)


def mlstm_layer(hj, modtab, w_up, conv_w, conv_b, w_q, w_k, w_v, w_o, w_if, b_if, norm_g, skip, w_down,
                ln_g, ln_b, rw_hi, rw_lo):
    rows = hj.shape[0]
    xm = mod_proj(hj, modtab, w_up.astype(BF16))
    xc, q, k = ml_qk(xm, conv_w, conv_b[None], jnp.concatenate([w_q, w_k], axis=1).astype(BF16))
    v = mm_act(xm, w_v.astype(BF16))
    o = mm_act(xm, w_o.astype(BF16), "sigmoid")
    n_gate = 4 * M_HEADS
    w_g = jnp.concatenate([w_if[0], w_if[1], jnp.zeros((D_M, 128 - n_gate), F32)], axis=1).astype(BF16)
    b_g = jnp.concatenate([b_if[0], b_if[1], jnp.zeros((128 - n_gate,), F32)])[None]
    g = ml_gates(xm, w_g, b_g)[:, :n_gate].reshape(BATCH, L_JOINT, 2, 2, M_HEADS)
    g4 = jnp.transpose(g, (0, 1, 4, 2, 3)).reshape(BATCH, L_JOINT, M_HEADS, 4)
    g_col = jnp.transpose(g4, (0, 2, 1, 3))
    g_row = jnp.transpose(g4, (0, 2, 3, 1))
    r3 = lambda a: a.reshape(BATCH, L_JOINT, a.shape[-1])
    hn = mlstm_cell(r3(q), r3(k), r3(v), g_col, g_row, norm_g[None]).reshape(rows, D_M)
    return post_mixer_ml(o, hn, xc, skip[None], w_down.astype(BF16), hj, modtab, ln_g, ln_b, rw_hi, rw_lo)


def _dft_mats(n_time):
    n = 3 * n_time // 2
    half = n // 2
    kk = np.arange(half, dtype=np.int64)[:, None]
    tt = np.arange(n_time, dtype=np.int64)[None, :]
    ang = 2.0 * np.pi * ((kk * tt) % n).astype(np.float64) / n
    top = np.cos(ang)
    bot = -np.sin(ang)
    bot[0] = np.cos(np.pi * tt[0])
    fwd = np.concatenate([top, bot], axis=0)
    mm_ = (np.arange(n_time, dtype=np.int64) + n_time // 2)[:, None]
    ang2 = 2.0 * np.pi * ((mm_ * kk.T) % n).astype(np.float64) / n
    wk = np.full((1, half), 2.0)
    wk[0, 0] = 1.0
    itop = wk * np.cos(ang2) / n
    ibot = -2.0 * np.sin(ang2) / n
    ibot[:, 0] = np.cos(np.pi * mm_[:, 0]) / n
    inv = np.concatenate([itop, ibot], axis=1)
    return jnp.asarray(fwd, dtype=BF16), jnp.asarray(inv, dtype=BF16)


def hyena_filters(n_time, w1, b1, fq1, w2, b2, fq2, w3):
    hp = lax.Precision.HIGHEST
    t01 = jnp.linspace(0.0, 1.0, n_time, dtype=F32)
    bands = jnp.linspace(1e-4, H_BANDS - 1, H_BANDS, dtype=F32)
    ang = (2.0 * math.pi / n_time) * jnp.arange(n_time, dtype=F32)[:, None] * bands[None, :]
    z = jnp.concatenate([t01[:, None], jnp.cos(ang), -jnp.sin(ang)], -1)
    hdn = jnp.sin(fq1 * (jnp.dot(z, w1, precision=hp) + b1))
    hdn = jnp.sin(fq2 * (jnp.dot(hdn, w2, precision=hp) + b2))
    filt = jnp.dot(hdn, w3, precision=hp).reshape(n_time, 2, D_MODEL)
    dist = jnp.abs(jnp.arange(n_time) - n_time // 2).astype(F32) * (2.0 / n_time)
    d_max = math.log(H_DECAY_TARGET) / H_FAST
    d_min = math.log(H_DECAY_TARGET) / H_SLOW
    deltas = jnp.abs(jnp.linspace(d_min, d_max, D_MODEL, dtype=F32))
    window = jnp.exp(-dist[:, None] * deltas[None, :])
    return filt * window[:, None, :]


def hyena_layer(hj, modtab, w_in, b_in, conv_w, conv_b, fparams, skip, w_out, ln_g, ln_b, rw_hi, rw_lo):
    rows = hj.shape[0]
    u3 = mod_proj(hj, modtab, w_in.astype(BF16), b_in[None]).reshape(BATCH, L_JOINT, 3 * D_MODEL)
    parts = []
    for seg_block, n_time in ((0, SEQ), (SEQ // CTX_LEN, CTX_LEN)):
        filt = hyena_filters(n_time, *fparams)
        fm, _ = _dft_mats(n_time)
        hf = bmm_left(fm, filt.reshape(1, n_time, 2 * D_MODEL))[0]
        parts.append(hyena_segment(u3, seg_block, n_time, conv_w, conv_b[None], hf, skip))
    z = jnp.concatenate(parts, axis=1).reshape(rows, D_MODEL)
    return post_mixer(z, w_out.astype(BF16), hj, modtab, ln_g, ln_b, rw_hi, rw_lo)


def kernel(x, c, ctx, c_ctx, router_w, router_b, ada_w, ada_b, ln_g, ln_b, moe_w_gate, moe_w_up, moe_w_down, rg_w_in, rg_conv_w, rg_conv_b, rg_gate_a_w, rg_gate_a_b, rg_gate_x_w, rg_gate_x_b, rg_lambda, rg_w_out, ml_w_up, ml_conv_w, ml_conv_b, ml_w_q, ml_w_k, ml_w_v, ml_w_o, ml_w_if, ml_b_if, ml_norm_g, ml_skip, ml_w_down, hy_w_in, hy_b_in, hy_conv_w, hy_conv_b, hy_f_w1, hy_f_b1, hy_f_freq1, hy_f_w2, hy_f_b2, hy_f_freq2, hy_f_w3, hy_skip, hy_w_out):
    bsz = x.shape[0]
    rows = bsz * L_JOINT
    hx = x + sincos_2d(SEQ)[None]
    hj = jnp.concatenate([hx, ctx], axis=1).reshape(rows, D_MODEL)

    cond = jnp.concatenate([jax.nn.silu(c), jax.nn.silu(c_ctx)[None],
                            jnp.zeros((16 - bsz - 1, D_MODEL), F32)], axis=0)
    mod_all = cond_proj(cond, ada_w, ada_b)
    rw_pad = jnp.concatenate([router_w, jnp.zeros((D_MODEL, ROUTER_PAD - N_EXPERTS), F32)], axis=1)
    rw_hi = rw_pad.astype(BF16)
    rw_lo = (rw_pad - rw_hi.astype(F32)).astype(BF16)

    for i in range(DEPTH):
        kind, j = i % N_MIXERS, i // N_MIXERS
        mod = mod_all[i].reshape(16, 6, D_MODEL)
        mod_x = mod[:bsz]
        mod_c = jnp.broadcast_to(mod[bsz][None], (bsz, 6, D_MODEL))
        modtab = jnp.stack([mod_x, mod_c], axis=1).reshape(2 * bsz, 6, D_MODEL)
        lng, lnb = ln_g[i, 0][None], ln_b[i, 0][None]
        if kind == 0:
            modtab_t = jnp.stack([jnp.transpose(mod_x, (1, 0, 2)), jnp.transpose(mod_c, (1, 0, 2))])
            hj, f, logits = rglru_layer(hj.reshape(bsz, L_JOINT, D_MODEL), modtab_t, rg_w_in[j], rg_conv_w[j],
                                        rg_conv_b[j], rg_gate_a_w[j], rg_gate_a_b[j], rg_gate_x_w[j],
                                        rg_gate_x_b[j], rg_lambda[j], rg_w_out[j], lng, lnb, rw_hi, rw_lo)
        elif kind == 1:
            hj, f, logits = mlstm_layer(hj, modtab, ml_w_up[j], ml_conv_w[j], ml_conv_b[j], ml_w_q[j], ml_w_k[j],
                                        ml_w_v[j], ml_w_o[j], ml_w_if[j], ml_b_if[j], ml_norm_g[j], ml_skip[j],
                                        ml_w_down[j], lng, lnb, rw_hi, rw_lo)
        else:
            fparams = (hy_f_w1[j], hy_f_b1[j], hy_f_freq1[j], hy_f_w2[j], hy_f_b2[j], hy_f_freq2[j], hy_f_w3[j])
            hj, f, logits = hyena_layer(hj, modtab, hy_w_in[j], hy_b_in[j], hy_conv_w[j], hy_conv_b[j], fparams,
                                        hy_skip[j], hy_w_out[j], lng, lnb, rw_hi, rw_lo)
        ya, yb, wts = moe(i, f, logits, router_b, moe_w_gate, moe_w_up, moe_w_down)
        hj = moe_combine(hj, ya, yb, wts, modtab, ln_g[i, 1][None], ln_b[i, 1][None], latent_only=i == DEPTH - 1)
    return hj.reshape(bsz, SEQ, D_MODEL)
```

```python
import functools
import math

import numpy as np
import jax
import jax.numpy as jnp
from jax import lax
from jax.experimental import pallas as pl
from jax.experimental.pallas import tpu as pltpu

F32 = jnp.float32
BF16 = jnp.bfloat16

D_MODEL = 1024
BATCH = 8
SEQ = 2048
DEPTH = 4
GRID_W = 64
CTX_LEN = 256
L_JOINT = SEQ + CTX_LEN
N_MIXERS = 3
ALPHA = (2.0 * DEPTH) ** 0.25
LN_EPS = 1e-6

D_RNN = 1408
RG_BLOCKS = 16
RG_BS = D_RNN // RG_BLOCKS
RG_C = 8.0

D_M = 2 * D_MODEL
M_HEADS = 8
M_DK = 128
M_DV = D_M // M_HEADS
CHUNK = 128

H_EMB = 33
H_BANDS = (H_EMB - 1) // 2
H_DECAY_TARGET = 1e-2
H_FAST = 0.3
H_SLOW = 1.5

N_EXPERTS = 16
N_GROUPS = 4
EXP_PER_GROUP = N_EXPERTS // N_GROUPS
TOP_K = 2
D_EXPERT = 512
ROUTER_PAD = 128

VMEM_LIMIT_BYTES = 56 * 1024 * 1024
ROW_TILE = 256
MOE_TILE = 512
RG_TBLK = 64
RG_LANES = D_RNN // 128
RG_CONV = 4
RG_WIN = 3
assert RG_BS <= 128 and RG_WIN * 128 >= 128 + 2 * (RG_BS - 1)
CONV_HALO = 16
ML_HEADS_PER_STEP = 2
HY_TN = 256
TILES_PER_BATCH = L_JOINT // ROW_TILE
LATENT_TILES = SEQ // ROW_TILE

MOD_SH, MOD_SC, MOD_GT, MOD_SH2, MOD_SC2, MOD_GT2 = range(6)


def _params(*sem):
    return pltpu.CompilerParams(dimension_semantics=sem, vmem_limit_bytes=VMEM_LIMIT_BYTES)


def _mod_index(i):
    return (2 * (i // TILES_PER_BATCH) + (i % TILES_PER_BATCH) // LATENT_TILES, 0, 0)


def _layer_norm_rows(r, g, b):
    mu = jnp.mean(r, axis=-1, keepdims=True)
    var = jnp.mean(jnp.square(r - mu), axis=-1, keepdims=True)
    return (r - mu) * lax.rsqrt(var + LN_EPS) * g + b


def _cond_body(c_ref, w_ref, b_ref, o_ref):
    o_ref[...] = jnp.dot(c_ref[...].astype(BF16), w_ref[...].astype(BF16), preferred_element_type=F32) + b_ref[...]


def cond_proj(cond, ada_w, ada_b):
    depth, d, n = ada_w.shape
    tn = d
    return pl.pallas_call(
        _cond_body,
        out_shape=jax.ShapeDtypeStruct((depth, cond.shape[0], n), F32),
        grid=(depth, n // tn),
        in_specs=[pl.BlockSpec(cond.shape, lambda l, j: (0, 0)),
                  pl.BlockSpec((None, d, tn), lambda l, j: (l, 0, j)),
                  pl.BlockSpec((None, 1, tn), lambda l, j: (l, 0, j))],
        out_specs=pl.BlockSpec((None, cond.shape[0], tn), lambda l, j: (l, 0, j)),
        compiler_params=_params("parallel", "parallel"),
        name="cond_proj",
    )(cond, ada_w, ada_b[:, None, :])


def _bmm_body(a_ref, x_ref, o_ref):
    o_ref[...] = jnp.dot(a_ref[...], x_ref[...].astype(BF16), preferred_element_type=F32).astype(o_ref.dtype)


def bmm_left(a, x, *, tn=256, out_dtype=F32):
    mo, k = a.shape
    b, _, d = x.shape
    assert d % tn == 0
    return pl.pallas_call(
        _bmm_body,
        out_shape=jax.ShapeDtypeStruct((b, mo, d), out_dtype),
        grid=(b, d // tn),
        in_specs=[pl.BlockSpec((mo, k), lambda i, j: (0, 0)),
                  pl.BlockSpec((None, k, tn), lambda i, j: (i, 0, j))],
        out_specs=pl.BlockSpec((None, mo, tn), lambda i, j: (i, 0, j)),
        compiler_params=_params("parallel", "parallel"),
        name="dft_mm",
    )(a, x)


def _pack_bf16_pairs(f):
    half = f.shape[1] // 2
    hi = lax.bitcast_convert_type(f[:, :half].astype(BF16).astype(F32), jnp.uint32)
    lo = lax.bitcast_convert_type(f[:, half:].astype(BF16).astype(F32), jnp.uint32)
    return hi | (lo >> 16)


def _unpack_bf16_pairs(u):
    hi = lax.bitcast_convert_type(u & jnp.uint32(0xFFFF0000), F32).astype(BF16)
    lo = lax.bitcast_convert_type(u << 16, F32).astype(BF16)
    return jnp.concatenate([hi, lo], axis=1)


def _post_epilogue(y, h, gt, sc2, sh2, ln_g, ln_b, rw_hi, rw_lo):
    hn = _layer_norm_rows(ALPHA * h + gt * y, ln_g, ln_b)
    f = hn * (1.0 + sc2) + sh2
    f_hi = f.astype(BF16)
    f_lo = (f - f_hi.astype(F32)).astype(BF16)
    lg = (jnp.dot(f_hi, rw_hi, preferred_element_type=F32) + jnp.dot(f_lo, rw_hi, preferred_element_type=F32)
          + jnp.dot(f_hi, rw_lo, preferred_element_type=F32))
    return hn, _pack_bf16_pairs(f), lg


def _post_body(z_ref, w_ref, h_ref, mod_ref, lng_ref, lnb_ref, rwh_ref, rwl_ref, ho_ref, f_ref, lg_ref):
    y = jnp.dot(z_ref[...].astype(BF16), w_ref[...], preferred_element_type=F32)
    hn, fpk, lg = _post_epilogue(y, h_ref[...], mod_ref[MOD_GT:MOD_GT + 1, :], mod_ref[MOD_SC2:MOD_SC2 + 1, :],
                                 mod_ref[MOD_SH2:MOD_SH2 + 1, :], lng_ref[...], lnb_ref[...], rwh_ref[...],
                                 rwl_ref[...])
    ho_ref[...] = hn
    f_ref[...] = fpk
    lg_ref[...] = lg


def post_mixer(z, w_out, h, modtab, ln_g, ln_b, rw_hi, rw_lo):
    t, k = z.shape
    d = w_out.shape[1]
    row = lambda i: (i, 0)
    fixed = lambda i: (0, 0)
    return pl.pallas_call(
        _post_body,
        out_shape=(jax.ShapeDtypeStruct((t, d), F32), jax.ShapeDtypeStruct((t, d // 2), jnp.uint32),
                   jax.ShapeDtypeStruct((t, ROUTER_PAD), F32)),
        grid=(t // ROW_TILE,),
        in_specs=[pl.BlockSpec((ROW_TILE, k), row),
                  pl.BlockSpec((k, d), fixed),
                  pl.BlockSpec((ROW_TILE, d), row),
                  pl.BlockSpec((None, 6, d), _mod_index),
                  pl.BlockSpec((1, d), fixed),
                  pl.BlockSpec((1, d), fixed),
                  pl.BlockSpec((d, ROUTER_PAD), fixed),
                  pl.BlockSpec((d, ROUTER_PAD), fixed)],
        out_specs=(pl.BlockSpec((ROW_TILE, d), row), pl.BlockSpec((ROW_TILE, d // 2), row),
                   pl.BlockSpec((ROW_TILE, ROUTER_PAD), row)),
        compiler_params=_params("parallel"),
        name="post_mixer",
    )(z, w_out, h, modtab, ln_g, ln_b, rw_hi, rw_lo)


def _combine_body(h_ref, ya_ref, yb_ref, w_ref, mod_ref, lng_ref, lnb_ref, o_ref):
    w = w_ref[...]
    ya = _unpack_bf16_pairs(ya_ref[...]).astype(F32)
    yb = _unpack_bf16_pairs(yb_ref[...]).astype(F32)
    y2 = w[:, 0:1] * ya + w[:, 1:2] * yb
    gt2 = mod_ref[MOD_GT2:MOD_GT2 + 1, :]
    o_ref[...] = _layer_norm_rows(ALPHA * h_ref[...] + gt2 * y2, lng_ref[...], lnb_ref[...])


def moe_combine(h, ya, yb, wts, modtab, ln_g, ln_b, latent_only=False):
    t, d = h.shape
    n_tiles = t // ROW_TILE
    row = lambda i: (i, 0)
    mod_index = (lambda i: (2 * (i // LATENT_TILES), 0, 0)) if latent_only else _mod_index
    fixed = lambda i: (0, 0)
    return pl.pallas_call(
        _combine_body,
        out_shape=jax.ShapeDtypeStruct((n_tiles * ROW_TILE, d), F32),
        grid=(n_tiles,),
        in_specs=[pl.BlockSpec((ROW_TILE, d), row), pl.BlockSpec((ROW_TILE, d // 2), row),
                  pl.BlockSpec((ROW_TILE, d // 2), row), pl.BlockSpec((ROW_TILE, TOP_K), row),
                  pl.BlockSpec((None, 6, d), mod_index),
                  pl.BlockSpec((1, d), fixed), pl.BlockSpec((1, d), fixed)],
        out_specs=pl.BlockSpec((ROW_TILE, d), lambda i: (i, 0)),
        compiler_params=_params("parallel"),
        name="moe_combine",
    )(h, ya, yb, wts, modtab, ln_g, ln_b)


def _rg_in_body(h_ref, mod_ref, w_ref, gate_ref, rec_ref):
    tt = h_ref.shape[1]
    sc = mod_ref[MOD_SC][:, None, :]
    sh = mod_ref[MOD_SH][:, None, :]
    inp = (h_ref[...] * (1.0 + sc) + sh).reshape(BATCH * tt, D_MODEL)
    p = jnp.dot(inp.astype(BF16), w_ref[...], preferred_element_type=F32)
    gate_ref[...] = jax.nn.gelu(p[:, :D_RNN]).astype(gate_ref.dtype).reshape(BATCH, tt, D_RNN)
    for b in range(BATCH):
        rec_b = p[b * tt:(b + 1) * tt, D_RNN:]
        for j in range(RG_LANES):
            rec_ref[j, pl.ds(b, tt, stride=BATCH), :] = rec_b[:, j * 128:(j + 1) * 128]


def rg_in(h3, modtab_t, w_in):
    nblk = L_JOINT // RG_TBLK
    return pl.pallas_call(
        _rg_in_body,
        out_shape=(jax.ShapeDtypeStruct((BATCH, L_JOINT, D_RNN), BF16),
                   jax.ShapeDtypeStruct((RG_LANES, L_JOINT * BATCH, 128), F32)),
        grid=(nblk,),
        in_specs=[pl.BlockSpec((BATCH, RG_TBLK, D_MODEL), lambda i: (0, i, 0)),
                  pl.BlockSpec((None, 6, BATCH, D_MODEL), lambda i: (i // (SEQ // RG_TBLK), 0, 0, 0)),
                  pl.BlockSpec((D_MODEL, 2 * D_RNN), lambda i: (0, 0))],
        out_specs=(pl.BlockSpec((BATCH, RG_TBLK, D_RNN), lambda i: (0, i, 0)),
                   pl.BlockSpec((RG_LANES, RG_TBLK * BATCH, 128), lambda i: (0, i, 0))),
        compiler_params=_params("parallel"),
        name="rg_in",
    )(h3, modtab_t, w_in)


def _rg_block(z, i):
    nblk = L_JOINT // RG_TBLK
    nlat = SEQ // RG_TBLK
    return jnp.where(z == 0, (i + nlat) % nblk, nblk - 1 - i)


def _rg_window_start(j):
    return min(max(j - 1, 0), RG_LANES - RG_WIN)


def _rg_scan_body(prev_ref, main_ref, next_ref, cw_ref, cb_ref, wg_ref, ba_ref, bx_ref, lam_ref, ho_ref,
                  a_s, u_s, st_ref):
    z = pl.program_id(0)
    i = pl.program_id(1)
    nblk = L_JOINT // RG_TBLK
    nlat = SEQ // RG_TBLK
    blk = _rg_block(z, i)
    rows = RG_TBLK * BATCH

    @pl.when(i == 0)
    def _():
        st_ref[...] = jnp.zeros_like(st_ref)

    has_prev = jnp.where((blk == 0) | (blk == nlat), 0.0, 1.0)
    has_next = jnp.where((blk == nlat - 1) | (blk == nblk - 1), 0.0, 1.0)
    cols = []
    for j in range(RG_LANES):
        ext = jnp.concatenate([prev_ref[j] * has_prev, main_ref[j], next_ref[j] * has_next], axis=0)
        lane = slice(j * 128, (j + 1) * 128)
        acc = cb_ref[:, lane] + cw_ref[0:1, lane] * ext[0:rows]
        for k in range(1, RG_CONV):
            acc = acc + cw_ref[k:k + 1, lane] * ext[k * BATCH:k * BATCH + rows]
        cols.append(acc)
    xcb = jnp.concatenate(cols, axis=1).astype(BF16)

    log_a_unit = -RG_C * jax.nn.softplus(-lam_ref[...])
    for j in range(RG_LANES):
        lane = slice(j * 128, (j + 1) * 128)
        w0 = _rg_window_start(j) * 128
        pre = jnp.dot(xcb[:, w0:w0 + RG_WIN * 128], wg_ref[j], preferred_element_type=F32)
        r = 0.5 * (jnp.tanh(0.5 * (pre[:, :128] + ba_ref[:, lane])) + 1.0)
        g = 0.5 * (jnp.tanh(0.5 * (pre[:, 128:] + bx_ref[:, lane])) + 1.0)
        a = jnp.exp(log_a_unit[:, lane] * r)
        a_s[:, lane] = a
        u_s[:, lane] = jnp.sqrt(1.0 - a * a) * (g * cols[j])

    def step(t, carry):
        te = jnp.where(z == 0, t, RG_TBLK - 1 - t)
        r0 = pl.multiple_of(te * BATCH, BATCH)
        out = []
        for j in range(RG_LANES):
            lane = slice(j * 128, (j + 1) * 128)
            h = a_s[pl.ds(r0, BATCH), lane] * carry[j] + u_s[pl.ds(r0, BATCH), lane]
            ho_ref[j, pl.ds(r0, BATCH), :] = h
            out.append(h)
        return tuple(out)

    final = lax.fori_loop(0, RG_TBLK, step, tuple(st_ref[j] for j in range(RG_LANES)), unroll=4)
    for j in range(RG_LANES):
        st_ref[j] = final[j]


def rg_scan(rec_tm, conv_w, conv_b, w_gates, ga_b, gx_b, lam):
    nblk = L_JOINT // RG_TBLK
    rows = RG_TBLK * BATCH
    halo_prev = (RG_CONV // 2) * BATCH
    per_prev = rows // halo_prev
    fixed2 = lambda z, i: (0, 0)
    per_dir = lambda z, i: (z, 0, 0)
    return pl.pallas_call(
        _rg_scan_body,
        out_shape=jax.ShapeDtypeStruct((2, RG_LANES, L_JOINT * BATCH, 128), F32),
        grid=(2, nblk),
        in_specs=[pl.BlockSpec((RG_LANES, halo_prev, 128),
                               lambda z, i: (0, jnp.maximum(_rg_block(z, i) * per_prev - 1, 0), 0)),
                  pl.BlockSpec((RG_LANES, rows, 128), lambda z, i: (0, _rg_block(z, i), 0)),
                  pl.BlockSpec((RG_LANES, BATCH, 128),
                               lambda z, i: (0, jnp.minimum((_rg_block(z, i) + 1) * RG_TBLK, L_JOINT - 1), 0)),
                  pl.BlockSpec((RG_CONV, D_RNN), fixed2),
                  pl.BlockSpec((1, D_RNN), fixed2),
                  pl.BlockSpec((None, RG_LANES, RG_WIN * 128, 256), lambda z, i: (z, 0, 0, 0)),
                  pl.BlockSpec((None, 1, D_RNN), per_dir),
                  pl.BlockSpec((None, 1, D_RNN), per_dir),
                  pl.BlockSpec((None, 1, D_RNN), per_dir)],
        out_specs=pl.BlockSpec((None, RG_LANES, rows, 128), lambda z, i: (z, 0, _rg_block(z, i), 0)),
        scratch_shapes=[pltpu.VMEM((rows, D_RNN), F32), pltpu.VMEM((rows, D_RNN), F32),
                        pltpu.VMEM((RG_LANES, BATCH, 128), F32)],
        compiler_params=_params("arbitrary", "arbitrary"),
        name="rg_scan",
    )(rec_tm, rec_tm, rec_tm, conv_w, conv_b, w_gates, ga_b, gx_b, lam)


def _rg_out_body(gate_ref, hf_ref, hb_ref, w_ref, h_ref, mod_ref, lng_ref, lnb_ref, rwh_ref, rwl_ref,
                 ho_ref, f_ref, lg_ref):
    tt = h_ref.shape[1]
    parts = []
    for b in range(BATCH):
        hsum = jnp.concatenate([hf_ref[j, pl.ds(b, tt, stride=BATCH), :] + hb_ref[j, pl.ds(b, tt, stride=BATCH), :]
                                for j in range(RG_LANES)], axis=1)
        parts.append((gate_ref[b].astype(F32) * hsum).astype(BF16))
    zz = jnp.concatenate(parts, axis=0)
    y = jnp.dot(zz, w_ref[...], preferred_element_type=F32)
    h = h_ref[...].reshape(BATCH * tt, D_MODEL)
    rep = lambda m: jnp.broadcast_to(mod_ref[m][:, None, :], (BATCH, tt, D_MODEL)).reshape(BATCH * tt, D_MODEL)
    hn, fpk, lg = _post_epilogue(y, h, rep(MOD_GT), rep(MOD_SC2), rep(MOD_SH2), lng_ref[...], lnb_ref[...],
                                 rwh_ref[...], rwl_ref[...])
    ho_ref[...] = hn.reshape(BATCH, tt, D_MODEL)
    f_ref[...] = fpk.reshape(BATCH, tt, D_MODEL // 2)
    lg_ref[...] = lg.reshape(BATCH, tt, ROUTER_PAD)


def rg_out(gate, h_tm, w_out, h3, modtab_t, ln_g, ln_b, rw_hi, rw_lo, latent_only=False):
    n_time = SEQ if latent_only else L_JOINT
    nblk = n_time // RG_TBLK
    rows = RG_TBLK * BATCH
    blk3 = lambda i: (0, i, 0)
    fixed = lambda i: (0, 0)
    return pl.pallas_call(
        _rg_out_body,
        out_shape=(jax.ShapeDtypeStruct((BATCH, n_time, D_MODEL), F32),
                   jax.ShapeDtypeStruct((BATCH, n_time, D_MODEL // 2), jnp.uint32),
                   jax.ShapeDtypeStruct((BATCH, n_time, ROUTER_PAD), F32)),
        grid=(nblk,),
        in_specs=[pl.BlockSpec((BATCH, RG_TBLK, D_RNN), blk3),
                  pl.BlockSpec((None, RG_LANES, rows, 128), lambda i: (0, 0, i, 0)),
                  pl.BlockSpec((None, RG_LANES, rows, 128), lambda i: (1, 0, i, 0)),
                  pl.BlockSpec((D_RNN, D_MODEL), fixed),
                  pl.BlockSpec((BATCH, RG_TBLK, D_MODEL), blk3),
                  pl.BlockSpec((None, 6, BATCH, D_MODEL), lambda i: (i // (SEQ // RG_TBLK), 0, 0, 0)),
                  pl.BlockSpec((1, D_MODEL), fixed), pl.BlockSpec((1, D_MODEL), fixed),
                  pl.BlockSpec((D_MODEL, ROUTER_PAD), fixed), pl.BlockSpec((D_MODEL, ROUTER_PAD), fixed)],
        out_specs=(pl.BlockSpec((BATCH, RG_TBLK, D_MODEL), blk3),
                   pl.BlockSpec((BATCH, RG_TBLK, D_MODEL // 2), blk3),
                   pl.BlockSpec((BATCH, RG_TBLK, ROUTER_PAD), blk3)),
        compiler_params=_params("parallel"),
        name="rg_out",
    )(gate, h_tm, h_tm, w_out, h3, modtab_t, ln_g, ln_b, rw_hi, rw_lo)


def _mlstm_body(q_ref, k_ref, v_ref, gc_ref, gr_ref, ng_ref, o_ref, hf_s, hb_s, gb_s, gr_s, ct_ref, n_ref, m_ref):
    row = lax.broadcasted_iota(jnp.int32, (CHUNK, CHUNK), 0)
    col = lax.broadcasted_iota(jnp.int32, (CHUNK, CHUNK), 1)
    lower = col <= row
    upper = col >= row
    tri_lower = jnp.where(lower, 1.0, 0.0).astype(BF16)
    tri_upper = jnp.where(upper, 1.0, 0.0).astype(BF16)
    ones_sq = jnp.ones((CHUNK, CHUNK), BF16)
    hp = ML_HEADS_PER_STEP
    tile_order = [(z, hh, kind) for z in range(2) for hh in range(hp) for kind in range(2)]
    n_src = 12 * hp
    src_row = lax.broadcasted_iota(jnp.int32, (2 * n_src, CHUNK), 0) % n_src
    spread = jnp.concatenate(
        [jnp.where(src_row == (4 * hh + 2 * z + (1 - kind) * (1 + 4 * hp * (1 + z))), 1.0, 0.0)
         for z, hh, kind in tile_order], axis=1).astype(BF16)

    def _hi_lo(x):
        hi = x.astype(BF16)
        return hi, (x - hi.astype(F32)).astype(BF16)

    def chunk(c, z, hh):
        reverse = z == 1
        c0 = pl.multiple_of(c * CHUNK, CHUNK)
        q = q_ref[pl.ds(c0, CHUNK), hh * M_DK:(hh + 1) * M_DK]
        k = k_ref[pl.ds(c0, CHUNK), hh * M_DK:(hh + 1) * M_DK]
        v = v_ref[pl.ds(c0, CHUNK), hh * M_DV:(hh + 1) * M_DV]
        mask = upper if reverse else lower
        bc_b = gb_s[z, hh, 0, pl.ds(c0, CHUNK), :]
        ig_b = gb_s[z, hh, 1, pl.ds(c0, CHUNK), :]
        r0 = 2 * (z * ML_HEADS_PER_STEP + hh)
        bc_r = gr_s[r0:r0 + 1, pl.ds(c0, CHUNK)]
        ig_r = gr_s[r0 + 1:r0 + 2, pl.ds(c0, CHUNK)]
        btot = bc_b[0:1, :] if reverse else bc_b[CHUNK - 1:CHUNK, :]
        m_prev = m_ref[z, hh]
        n_prev = n_ref[z, hh]
        ct_prev = ct_ref[z, hh]
        two = lambda a: jnp.concatenate([a, a], axis=1)
        dlog = jnp.where(mask, bc_b - bc_r + ig_r, -jnp.inf)
        m_inter = bc_b + m_prev
        m_comb = jnp.maximum(m_inter, jnp.max(dlog, axis=1, keepdims=True))
        qk = lax.dot_general(q, k, (((1,), (1,)), ((), ())), preferred_element_type=F32)
        s = qk * jnp.exp(dlog - m_comb)
        inter = jnp.exp(m_inter - m_comb)
        sb = s.astype(BF16)
        num = (jnp.dot(sb, v, preferred_element_type=F32)
               + two(inter) * jnp.dot(q, ct_prev.astype(BF16), preferred_element_type=F32))
        s_sum = jnp.dot(sb, ones_sq, preferred_element_type=F32)
        n_rows = jnp.broadcast_to(n_prev, (CHUNK, M_DK)).astype(BF16)
        qn = lax.dot_general(q, n_rows, (((1,), (1,)), ((), ())), preferred_element_type=F32)
        den = s_sum + inter * qn
        h = num * two(1.0 / jnp.maximum(jnp.abs(den), jnp.exp(-m_comb)))
        wlog = btot - bc_b + ig_b
        mloc = jnp.max(wlog, axis=0, keepdims=True)
        wgt = jnp.exp(wlog - mloc)
        m_new = jnp.maximum(btot + m_prev, mloc)
        sp = jnp.exp(btot + m_prev - m_new)
        sl = jnp.exp(mloc - m_new)
        kf = k.astype(F32)
        vw = (v.astype(F32) * two(wgt)).astype(BF16)
        ct_loc = jnp.dot(kf.T.astype(BF16), vw, preferred_element_type=F32)
        ct_ref[z, hh] = two(sp) * ct_prev + two(sl) * ct_loc
        n_ref[z, hh] = sp * n_prev + sl * jnp.sum(kf * wgt, axis=0, keepdims=True)
        m_ref[z, hh] = m_new
        return c0, h

    ct_ref[...] = jnp.zeros_like(ct_ref)
    n_ref[...] = jnp.zeros_like(n_ref)
    m_ref[...] = jnp.zeros_like(m_ref)

    n_chunks = o_ref.shape[0] // CHUNK
    n_lat = SEQ // CHUNK

    def gate_sums(c, carry):
        c0 = pl.multiple_of(c * CHUNK, CHUNK)
        gc = jnp.concatenate([gc_ref[hh, pl.ds(c0, CHUNK), :] for hh in range(hp)], axis=1)
        gr = jnp.concatenate([gr_ref[hh, :, pl.ds(c0, CHUNK)] for hh in range(hp)], axis=0)
        gc_parts = jnp.concatenate(_hi_lo(gc), axis=0)
        cum_c = [jnp.dot(jnp.concatenate([t, t], axis=1), gc_parts, preferred_element_type=F32)
                 for t in (tri_lower, tri_upper)]
        gr_parts = jnp.concatenate(_hi_lo(gr), axis=1)
        cum_r = [jnp.dot(gr_parts, jnp.concatenate([t, t], axis=0), preferred_element_type=F32)
                 for t in (tri_upper, tri_lower)]
        src = jnp.concatenate([gc, cum_c[0], cum_c[1]], axis=1)
        tiles = jnp.dot(jnp.concatenate(_hi_lo(src), axis=1), spread, preferred_element_type=F32)
        for n, (z, hh, kind) in enumerate(tile_order):
            gb_s[z, hh, kind, pl.ds(c0, CHUNK), :] = tiles[:, n * CHUNK:(n + 1) * CHUNK]
            r0 = 2 * (z * hp + hh) + kind
            col = 4 * hh + 2 * z + 1 - kind
            row = cum_r[z][col:col + 1, :] if kind == 0 else gr[col:col + 1, :]
            gr_s[r0:r0 + 1, pl.ds(c0, CHUNK)] = row
        return carry

    lax.fori_loop(0, n_chunks, gate_sums, 0, unroll=3)

    def both_directions(i, carry):
        for hh in range(ML_HEADS_PER_STEP):
            c0, h = chunk((i + n_lat) % n_chunks, 0, hh)
            hf_s[hh, pl.ds(c0, CHUNK), :] = h
            c0, h = chunk(n_chunks - 1 - i, 1, hh)
            hb_s[hh, pl.ds(c0, CHUNK), :] = h
        return carry

    lax.fori_loop(0, n_chunks, both_directions, 0)

    def head_norm(c, carry):
        c0 = pl.multiple_of(c * CHUNK, CHUNK)
        for hh in range(ML_HEADS_PER_STEP):
            tot = hf_s[hh, pl.ds(c0, CHUNK), :] + hb_s[hh, pl.ds(c0, CHUNK), :]
            mu = jnp.mean(tot, axis=1, keepdims=True)
            var = jnp.mean(jnp.square(tot - mu), axis=1, keepdims=True)
            lanes = slice(hh * M_DV, (hh + 1) * M_DV)
            o_ref[pl.ds(c0, CHUNK), lanes] = ((tot - mu) * lax.rsqrt(var + LN_EPS) * ng_ref[:, lanes]
                                             ).astype(o_ref.dtype)
        return carry

    lax.fori_loop(0, n_chunks, head_norm, 0)


def mlstm_cell(q, k, v, g_col, g_row, norm_g):
    b, lj, _ = q.shape
    hp = ML_HEADS_PER_STEP
    return pl.pallas_call(
        _mlstm_body,
        out_shape=jax.ShapeDtypeStruct((b, lj, M_HEADS * M_DV), BF16),
        grid=(b, M_HEADS // hp),
        in_specs=[pl.BlockSpec((None, lj, hp * M_DK), lambda i, h: (i, 0, h)),
                  pl.BlockSpec((None, lj, hp * M_DK), lambda i, h: (i, 0, h)),
                  pl.BlockSpec((None, lj, hp * M_DV), lambda i, h: (i, 0, h)),
                  pl.BlockSpec((None, hp, lj, 4), lambda i, h: (i, h, 0, 0)),
                  pl.BlockSpec((None, hp, 4, lj), lambda i, h: (i, h, 0, 0)),
                  pl.BlockSpec((1, hp * M_DV), lambda i, h: (0, h))],
        out_specs=pl.BlockSpec((None, lj, hp * M_DV), lambda i, h: (i, 0, h)),
        scratch_shapes=[pltpu.VMEM((hp, lj, M_DV), F32), pltpu.VMEM((hp, lj, M_DV), F32),
                        pltpu.VMEM((2, hp, 2, lj, CHUNK), F32), pltpu.VMEM((4 * hp, lj), F32),
                        pltpu.VMEM((2, hp, M_DK, M_DV), F32), pltpu.VMEM((2, hp, 1, M_DK), F32),
                        pltpu.VMEM((2, hp, 1, CHUNK), F32)],
        compiler_params=_params("parallel", "parallel"),
        name="mlstm_cell",
    )(q, k, v, g_col, g_row, norm_g)


def _tile_neighbours(i):
    r = i % TILES_PER_BATCH
    has_prev = jnp.where((r == 0) | (r == LATENT_TILES), 0.0, 1.0)
    has_next = jnp.where((r == LATENT_TILES - 1) | (r == TILES_PER_BATCH - 1), 0.0, 1.0)
    return has_prev, has_next


def _ml_up_body(h_ref, mod_ref, w_ref, o_ref):
    inp = h_ref[...] * (1.0 + mod_ref[MOD_SC:MOD_SC + 1, :]) + mod_ref[MOD_SH:MOD_SH + 1, :]
    o_ref[...] = jnp.dot(inp.astype(BF16), w_ref[...], preferred_element_type=F32).astype(o_ref.dtype)


def mod_proj(h, modtab, w, bias=None):
    t, d = h.shape
    n = w.shape[1]
    body = _ml_up_body
    args = [h, modtab, w]
    in_specs = [pl.BlockSpec((ROW_TILE, d), lambda i: (i, 0)),
                pl.BlockSpec((None, 6, d), _mod_index),
                pl.BlockSpec((d, n), lambda i: (0, 0))]
    if bias is not None:
        def body(h_ref, mod_ref, w_ref, b_ref, o_ref):
            inp = h_ref[...] * (1.0 + mod_ref[MOD_SC:MOD_SC + 1, :]) + mod_ref[MOD_SH:MOD_SH + 1, :]
            y = jnp.dot(inp.astype(BF16), w_ref[...], preferred_element_type=F32) + b_ref[...]
            o_ref[...] = y.astype(o_ref.dtype)
        args.append(bias)
        in_specs.append(pl.BlockSpec((1, n), lambda i: (0, 0)))
    return pl.pallas_call(
        body,
        out_shape=jax.ShapeDtypeStruct((t, n), BF16),
        grid=(t // ROW_TILE,),
        in_specs=in_specs,
        out_specs=pl.BlockSpec((ROW_TILE, n), lambda i: (i, 0)),
        compiler_params=_params("parallel"),
        name="mod_proj",
    )(*args)


def _ml_qk_body(prev_ref, main_ref, next_ref, cw_ref, cb_ref, w_ref, xc_ref, q_ref, k_ref, ext_s):
    has_prev, has_next = _tile_neighbours(pl.program_id(0))
    ext_s[0:CONV_HALO, :] = prev_ref[...].astype(F32) * has_prev
    ext_s[CONV_HALO:CONV_HALO + ROW_TILE, :] = main_ref[...].astype(F32)
    ext_s[CONV_HALO + ROW_TILE:, :] = next_ref[...].astype(F32) * has_next
    kk = cw_ref.shape[0]
    acc = cb_ref[...] + cw_ref[0:1, :] * ext_s[pl.ds(CONV_HALO - kk // 2, ROW_TILE), :]
    for j in range(1, kk):
        acc = acc + cw_ref[j:j + 1, :] * ext_s[pl.ds(CONV_HALO - kk // 2 + j, ROW_TILE), :]
    xc = (acc * jax.nn.sigmoid(acc)).astype(BF16)
    xc_ref[...] = xc
    qk = jnp.dot(xc, w_ref[...], preferred_element_type=F32)
    nq = q_ref.shape[1]
    q_ref[...] = qk[:, :nq].astype(BF16)
    k_ref[...] = (qk[:, nq:] * (M_DK ** -0.5)).astype(BF16)


def ml_qk(xm, conv_w, conv_b, w_qk):
    t, dm = xm.shape
    nq = M_HEADS * M_DK
    per = ROW_TILE // CONV_HALO
    last = t // CONV_HALO - 1
    row = lambda i: (i, 0)
    fixed = lambda i: (0, 0)
    return pl.pallas_call(
        _ml_qk_body,
        out_shape=(jax.ShapeDtypeStruct((t, dm), BF16), jax.ShapeDtypeStruct((t, nq), BF16),
                   jax.ShapeDtypeStruct((t, nq), BF16)),
        grid=(t // ROW_TILE,),
        in_specs=[pl.BlockSpec((CONV_HALO, dm), lambda i: (jnp.maximum(i * per - 1, 0), 0)),
                  pl.BlockSpec((ROW_TILE, dm), row),
                  pl.BlockSpec((CONV_HALO, dm), lambda i: (jnp.minimum((i + 1) * per, last), 0)),
                  pl.BlockSpec(conv_w.shape, fixed), pl.BlockSpec((1, dm), fixed),
                  pl.BlockSpec((dm, 2 * nq), fixed)],
        out_specs=(pl.BlockSpec((ROW_TILE, dm), row), pl.BlockSpec((ROW_TILE, nq), row),
                   pl.BlockSpec((ROW_TILE, nq), row)),
        scratch_shapes=[pltpu.VMEM((ROW_TILE + 2 * CONV_HALO, dm), F32)],
        compiler_params=_params("parallel"),
        name="ml_qk",
    )(xm, xm, xm, conv_w, conv_b, w_qk)


def _mm_act_body(x_ref, w_ref, o_ref, *, act):
    y = jnp.dot(x_ref[...], w_ref[...], preferred_element_type=F32)
    if act == "sigmoid":
        y = jax.nn.sigmoid(y)
    o_ref[...] = y.astype(o_ref.dtype)


def mm_act(x, w, act=None, *, tm=512, tn=1024):
    m, k = x.shape
    n = w.shape[1]
    return pl.pallas_call(
        functools.partial(_mm_act_body, act=act),
        out_shape=jax.ShapeDtypeStruct((m, n), BF16),
        grid=(n // tn, m // tm),
        in_specs=[pl.BlockSpec((tm, k), lambda j, i: (i, 0)),
                  pl.BlockSpec((k, tn), lambda j, i: (0, j))],
        out_specs=pl.BlockSpec((tm, tn), lambda j, i: (i, j)),
        compiler_params=_params("parallel", "parallel"),
        name="mm_act",
    )(x, w)


def _ml_gates_body(x_ref, w_ref, b_ref, o_ref):
    g = jnp.dot(x_ref[...], w_ref[...], preferred_element_type=F32) + b_ref[...]
    lane = lax.broadcasted_iota(jnp.int32, g.shape, 1)
    is_forget = (lane % (2 * M_HEADS)) >= M_HEADS
    o_ref[...] = jnp.where(is_forget, jax.nn.log_sigmoid(g), g)


def ml_gates(xm, w_g, b_g):
    t, dm = xm.shape
    return pl.pallas_call(
        _ml_gates_body,
        out_shape=jax.ShapeDtypeStruct((t, 128), F32),
        grid=(t // 512,),
        in_specs=[pl.BlockSpec((512, dm), lambda i: (i, 0)), pl.BlockSpec((dm, 128), lambda i: (0, 0)),
                  pl.BlockSpec((1, 128), lambda i: (0, 0))],
        out_specs=pl.BlockSpec((512, 128), lambda i: (i, 0)),
        compiler_params=_params("parallel"),
        name="ml_gates",
    )(xm, w_g, b_g)


def _post_ml_body(o_ref, hn_ref, xc_ref, skip_ref, w_ref, h_ref, mod_ref, lng_ref, lnb_ref, rwh_ref, rwl_ref,
                  ho_ref, f_ref, lg_ref):
    z = o_ref[...].astype(F32) * hn_ref[...].astype(F32) + skip_ref[...] * xc_ref[...].astype(F32)
    y = jnp.dot(z.astype(BF16), w_ref[...], preferred_element_type=F32)
    hn, fpk, lg = _post_epilogue(y, h_ref[...], mod_ref[MOD_GT:MOD_GT + 1, :], mod_ref[MOD_SC2:MOD_SC2 + 1, :],
                                 mod_ref[MOD_SH2:MOD_SH2 + 1, :], lng_ref[...], lnb_ref[...], rwh_ref[...],
                                 rwl_ref[...])
    ho_ref[...] = hn
    f_ref[...] = fpk
    lg_ref[...] = lg


def post_mixer_ml(o, hn, xc, skip, w_out, h, modtab, ln_g, ln_b, rw_hi, rw_lo):
    t, k = o.shape
    d = w_out.shape[1]
    row = lambda i: (i, 0)
    fixed = lambda i: (0, 0)
    return pl.pallas_call(
        _post_ml_body,
        out_shape=(jax.ShapeDtypeStruct((t, d), F32), jax.ShapeDtypeStruct((t, d // 2), jnp.uint32),
                   jax.ShapeDtypeStruct((t, ROUTER_PAD), F32)),
        grid=(t // ROW_TILE,),
        in_specs=[pl.BlockSpec((ROW_TILE, k), row), pl.BlockSpec((ROW_TILE, k), row),
                  pl.BlockSpec((ROW_TILE, k), row), pl.BlockSpec((1, k), fixed),
                  pl.BlockSpec((k, d), fixed),
                  pl.BlockSpec((ROW_TILE, d), row),
                  pl.BlockSpec((None, 6, d), _mod_index),
                  pl.BlockSpec((1, d), fixed), pl.BlockSpec((1, d), fixed),
                  pl.BlockSpec((d, ROUTER_PAD), fixed), pl.BlockSpec((d, ROUTER_PAD), fixed)],
        out_specs=(pl.BlockSpec((ROW_TILE, d), row), pl.BlockSpec((ROW_TILE, d // 2), row),
                   pl.BlockSpec((ROW_TILE, ROUTER_PAD), row)),
        compiler_params=_params("parallel"),
        name="post_mixer_ml",
    )(o, hn, xc, skip, w_out, h, modtab, ln_g, ln_b, rw_hi, rw_lo)


def _hy_conv(u_ref, cw_ref, cb_ref, ext_s):
    n_time = u_ref.shape[0]
    ext_s[0:8, :] = jnp.zeros((8, ext_s.shape[1]), F32)
    ext_s[8:8 + n_time, :] = u_ref[...].astype(F32)
    ext_s[8 + n_time:, :] = jnp.zeros((8, ext_s.shape[1]), F32)
    kk = cw_ref.shape[0]
    acc = cb_ref[...] + cw_ref[0:1, :] * ext_s[pl.ds(8 - kk // 2, n_time), :]
    for j in range(1, kk):
        acc = acc + cw_ref[j:j + 1, :] * ext_s[pl.ds(8 - kk // 2 + j, n_time), :]
    return acc


def _hy_spectrum(fm_ref, sig, hf_ref, y_ref):
    x = jnp.dot(fm_ref[...], sig, preferred_element_type=F32)
    half = x.shape[0] // 2
    xt, xb = x[:half], x[half:]
    ht, hb = hf_ref[0:half, :], hf_ref[half:, :]
    first = lax.broadcasted_iota(jnp.int32, xt.shape, 0) == 0
    y_ref[0:half, :] = jnp.where(first, xt * ht, xt * ht - xb * hb).astype(y_ref.dtype)
    y_ref[half:, :] = jnp.where(first, xb * hb, xt * hb + xb * ht).astype(y_ref.dtype)


def _hy_fwd_conv_body(u_ref, cw_ref, cb_ref, fm_ref, hf_ref, v_ref, y_ref, ext_s):
    v = _hy_conv(u_ref, cw_ref, cb_ref, ext_s).astype(BF16)
    v_ref[...] = v
    _hy_spectrum(fm_ref, v, hf_ref, y_ref)


def _hy_fwd_body(s_ref, fm_ref, hf_ref, y_ref):
    _hy_spectrum(fm_ref, s_ref[...], hf_ref, y_ref)


def _hy_inv_body(y_ref, gm_ref, u_ref, cw_ref, cb_ref, s_ref, skip_ref, z_ref, ext_s):
    y = jnp.dot(gm_ref[...], y_ref[...], preferred_element_type=F32)
    g = _hy_conv(u_ref, cw_ref, cb_ref, ext_s)
    z_ref[...] = (g * (y + s_ref[...].astype(F32) * skip_ref[...])).astype(z_ref.dtype)


def hyena_segment(u3, seg_block, n_time, conv_w, conv_b, hf, skip):
    b = u3.shape[0]
    tn = HY_TN
    nj = D_MODEL // tn
    fm, gm = _dft_mats(n_time)
    n = fm.shape[0]
    grid = (nj, b)
    u_spec = lambda part: pl.BlockSpec((None, n_time, tn), lambda j, i: (i, seg_block, part * nj + j))
    cw_spec = lambda part: pl.BlockSpec((conv_w.shape[0], tn), lambda j, i: (0, part * nj + j))
    cb_spec = lambda part: pl.BlockSpec((1, tn), lambda j, i: (0, part * nj + j))
    fixed = lambda j, i: (0, 0)
    sig_spec = pl.BlockSpec((None, n_time, tn), lambda j, i: (i, 0, j))
    spec_spec = pl.BlockSpec((None, n, tn), lambda j, i: (i, 0, j))
    hf_spec = lambda c: pl.BlockSpec((n, tn), lambda j, i: (0, c * nj + j))
    skip_spec = lambda c: pl.BlockSpec((None, 1, tn), lambda j, i: (c, 0, j))
    ext = pltpu.VMEM((n_time + 16, tn), F32)
    sig_shape = jax.ShapeDtypeStruct((b, n_time, D_MODEL), BF16)
    spec_shape = jax.ShapeDtypeStruct((b, n, D_MODEL), BF16)
    par = _params("parallel", "parallel")
    skip3 = skip[:, None, :]

    v, y1 = pl.pallas_call(
        _hy_fwd_conv_body, out_shape=(sig_shape, spec_shape), grid=grid,
        in_specs=[u_spec(0), cw_spec(0), cb_spec(0), pl.BlockSpec(fm.shape, fixed), hf_spec(0)],
        out_specs=(sig_spec, spec_spec), scratch_shapes=[ext], compiler_params=par, name="hy_fwd_conv",
    )(u3, conv_w, conv_b, fm, hf)
    z1 = pl.pallas_call(
        _hy_inv_body, out_shape=sig_shape, grid=grid,
        in_specs=[spec_spec, pl.BlockSpec(gm.shape, fixed), u_spec(1), cw_spec(1), cb_spec(1), sig_spec,
                  skip_spec(0)],
        out_specs=sig_spec, scratch_shapes=[ext], compiler_params=par, name="hy_inv",
    )(y1, gm, u3, conv_w, conv_b, v, skip3)
    y2 = pl.pallas_call(
        _hy_fwd_body, out_shape=spec_shape, grid=grid,
        in_specs=[sig_spec, pl.BlockSpec(fm.shape, fixed), hf_spec(1)],
        out_specs=spec_spec, compiler_params=par, name="hy_fwd",
    )(z1, fm, hf)
    return pl.pallas_call(
        _hy_inv_body, out_shape=sig_shape, grid=grid,
        in_specs=[spec_spec, pl.BlockSpec(gm.shape, fixed), u_spec(2), cw_spec(2), cb_spec(2), sig_spec,
                  skip_spec(1)],
        out_specs=sig_spec, scratch_shapes=[ext], compiler_params=par, name="hy_inv",
    )(y2, gm, u3, conv_w, conv_b, z1, skip3)


def _moe_body(te_ref, tf_ref, nu_ref, x_ref, wg_ref, wu_ref, wd_ref, o_ref, wg_s, wu_s, wd_s):
    i = pl.program_id(0)

    @pl.when(i < nu_ref[0])
    def _():
        @pl.when(tf_ref[i] == 1)
        def _():
            wg_s[...] = wg_ref[...].astype(BF16)
            wu_s[...] = wu_ref[...].astype(BF16)
            wd_s[...] = wd_ref[...].astype(BF16)

        x = _unpack_bf16_pairs(x_ref[...])
        a = jnp.dot(x, wg_s[...], preferred_element_type=F32)
        u = jnp.dot(x, wu_s[...], preferred_element_type=F32)
        hid = (a * jax.nn.sigmoid(a)) * u
        o_ref[...] = _pack_bf16_pairs(jnp.dot(hid.astype(BF16), wd_s[...], preferred_element_type=F32))

    @pl.when(i >= nu_ref[0])
    def _():
        o_ref[...] = jnp.zeros_like(o_ref)


def moe_experts(layer, tile_expert, tile_first, n_used, xs, w_gate, w_up, w_down):
    n_rows = xs.shape[0]
    d = 2 * xs.shape[1]
    n_tiles = n_rows // MOE_TILE
    wmap_in = lambda i, te, tf, nu: (layer, te[i], 0, 0)
    grid_spec = pltpu.PrefetchScalarGridSpec(
        num_scalar_prefetch=3,
        grid=(n_tiles,),
        in_specs=[pl.BlockSpec((MOE_TILE, d // 2), lambda i, te, tf, nu: (i, 0)),
                  pl.BlockSpec((None, None, d, D_EXPERT), wmap_in),
                  pl.BlockSpec((None, None, d, D_EXPERT), wmap_in),
                  pl.BlockSpec((None, None, D_EXPERT, d), wmap_in)],
        out_specs=pl.BlockSpec((MOE_TILE, d // 2), lambda i, te, tf, nu: (i, 0)),
        scratch_shapes=[pltpu.VMEM((d, D_EXPERT), BF16), pltpu.VMEM((d, D_EXPERT), BF16),
                        pltpu.VMEM((D_EXPERT, d), BF16)],
    )
    return pl.pallas_call(
        _moe_body,
        out_shape=jax.ShapeDtypeStruct((n_rows, d // 2), jnp.uint32),
        grid_spec=grid_spec,
        compiler_params=_params("arbitrary"),
        name="moe_experts",
    )(tile_expert, tile_first, n_used, xs, w_gate, w_up, w_down)


def _first_max4(a):
    m = jnp.maximum(jnp.maximum(a[0], a[1]), jnp.maximum(a[2], a[3]))
    idx = jnp.where(a[0] == m, 0, jnp.where(a[1] == m, 1, jnp.where(a[2] == m, 2, 3))).astype(jnp.int32)
    return m, idx


def _router_body(lg_ref, rb_ref, dest_ref, wts_ref, te_ref, tf_ref, nu_ref, lgt_s, cnt_s, pre_s):
    t = lg_ref.shape[0]
    n_chunk = t // 128

    def transpose(c, carry):
        c0 = pl.multiple_of(c * 128, 128)
        lgt_s[:, pl.ds(c0, 128)] = lg_ref[pl.ds(c0, 128), :].T[0:N_EXPERTS, :]
        return carry

    lax.fori_loop(0, n_chunk, transpose, 0)

    s = jax.nn.sigmoid(lgt_s[...])
    sel = s + rb_ref[...]
    gs, i1s, i2s = [], [], []
    for g in range(N_GROUPS):
        a = [sel[g * EXP_PER_GROUP + j:g * EXP_PER_GROUP + j + 1, :] for j in range(EXP_PER_GROUP)]
        m1, i1 = _first_max4(a)
        m2, i2 = _first_max4([jnp.where(i1 == j, -jnp.inf, a[j]) for j in range(EXP_PER_GROUP)])
        gs.append(m1 + m2)
        i1s.append(i1)
        i2s.append(i2)
    _, gb = _first_max4(gs)
    pick = lambda v: jnp.where(gb == 0, v[0], jnp.where(gb == 1, v[1], jnp.where(gb == 2, v[2], v[3])))
    e1 = gb * EXP_PER_GROUP + pick(i1s)
    e2 = gb * EXP_PER_GROUP + pick(i2s)
    eid = lax.broadcasted_iota(jnp.int32, (N_EXPERTS, t), 0)
    is1 = eid == e1
    is2 = eid == e2
    w1 = jnp.sum(jnp.where(is1, s, 0.0), axis=0, keepdims=True)
    w2 = jnp.sum(jnp.where(is2, s, 0.0), axis=0, keepdims=True)
    tot = w1 + w2
    wts_ref[...] = jnp.concatenate([w1 / tot, w2 / tot, jnp.zeros((6, t), F32)], axis=0)

    cnt_s[...] = jnp.where(is1 | is2, 1.0, 0.0).astype(BF16)
    r = lax.broadcasted_iota(jnp.int32, (128, 128), 0)
    c = lax.broadcasted_iota(jnp.int32, (128, 128), 1)
    before = jnp.where(r < c, 1.0, 0.0).astype(BF16)
    ones = jnp.ones((128, 128), BF16)

    def prefix(ci, carry):
        c0 = pl.multiple_of(ci * 128, 128)
        blk = cnt_s[:, pl.ds(c0, 128)]
        pre_s[:, pl.ds(c0, 128)] = carry + jnp.dot(blk, before, preferred_element_type=F32)
        return carry + jnp.dot(blk, ones, preferred_element_type=F32)

    counts = lax.fori_loop(0, n_chunk, prefix, jnp.zeros((N_EXPERTS, 128), F32))
    tiles_per = jnp.floor((counts + (MOE_TILE - 1)) * (1.0 / MOE_TILE))
    er = lax.broadcasted_iota(jnp.int32, (N_EXPERTS, N_EXPERTS), 0)
    ec = lax.broadcasted_iota(jnp.int32, (N_EXPERTS, N_EXPERTS), 1)
    earlier = jnp.where(ec < er, 1.0, 0.0).astype(BF16)
    tile_start = jnp.dot(earlier, tiles_per.astype(BF16), preferred_element_type=F32)
    tile_end = tile_start + tiles_per
    slot = tile_start[:, 0:1] * MOE_TILE + pre_s[...]
    d1 = jnp.sum(jnp.where(is1, slot, 0.0), axis=0, keepdims=True)
    d2 = jnp.sum(jnp.where(is2, slot, 0.0), axis=0, keepdims=True)
    dest_ref[...] = jnp.concatenate([d1, d2, jnp.zeros((6, t), F32)], axis=0).astype(jnp.int32)

    n_used = tile_end[N_EXPERTS - 1:N_EXPERTS, 0:1]
    tid = lax.broadcasted_iota(jnp.int32, (N_EXPERTS, te_ref.shape[1]), 1).astype(F32)
    tid_c = jnp.minimum(tid, n_used - 1.0)
    t_exp = jnp.sum(jnp.where(tile_end[:, 0:1] <= tid_c, 1.0, 0.0), axis=0, keepdims=True)
    t_exp = jnp.minimum(t_exp, N_EXPERTS - 1.0)
    t_first = jnp.sum(jnp.where((tile_start[:, 0:1] == tid) & (tiles_per[:, 0:1] > 0.0), 1.0, 0.0),
                      axis=0, keepdims=True)
    te_ref[...] = jnp.broadcast_to(t_exp, te_ref.shape).astype(jnp.int32)
    tf_ref[...] = jnp.broadcast_to(t_first, tf_ref.shape).astype(jnp.int32)
    nu_ref[...] = jnp.broadcast_to(n_used, nu_ref.shape).astype(jnp.int32)


def moe_router(logits, router_b):
    t = logits.shape[0]
    n_tab = 256
    assert t * TOP_K // MOE_TILE + N_EXPERTS <= n_tab
    full = lambda shape: pl.BlockSpec(shape, lambda i: (0, 0))
    dest, wts, te, tf, nu = pl.pallas_call(
        _router_body,
        out_shape=(jax.ShapeDtypeStruct((8, t), jnp.int32), jax.ShapeDtypeStruct((8, t), F32),
                   jax.ShapeDtypeStruct((8, n_tab), jnp.int32), jax.ShapeDtypeStruct((8, n_tab), jnp.int32),
                   jax.ShapeDtypeStruct((8, 128), jnp.int32)),
        grid=(1,),
        in_specs=[full((t, ROUTER_PAD)), full((N_EXPERTS, 1))],
        out_specs=(full((8, t)), full((8, t)), full((8, n_tab)), full((8, n_tab)), full((8, 128))),
        scratch_shapes=[pltpu.VMEM((N_EXPERTS, t), F32), pltpu.VMEM((N_EXPERTS, t), BF16),
                        pltpu.VMEM((N_EXPERTS, t), F32)],
        compiler_params=_params("arbitrary"),
        name="moe_router",
    )(logits, router_b[:, None])
    return dest[:TOP_K], wts[:TOP_K], te[0], tf[0], nu[0, :1]


def moe(layer, f, logits, router_b, w_gate, w_up, w_down):
    t = f.shape[0]
    n_tiles = t * TOP_K // MOE_TILE + N_EXPERTS
    n_rows = n_tiles * MOE_TILE
    dest, wts, tile_expert, tile_first, n_used = moe_router(logits, router_b)
    token = jnp.arange(t, dtype=jnp.int32)
    row_token = jnp.zeros((n_rows,), jnp.int32).at[dest.reshape(-1)].set(
        jnp.concatenate([token, token]), unique_indices=True)
    xs = jnp.take(f, row_token, axis=0)
    ys = moe_experts(layer, tile_expert[:n_tiles], tile_first[:n_tiles], n_used, xs, w_gate, w_up, w_down)
    return jnp.take(ys, dest[0], axis=0), jnp.take(ys, dest[1], axis=0), wts.T


def sincos_2d(n_tok):
    rows = n_tok // GRID_W
    quarter = D_MODEL // 4
    omega = 1.0 / (10000.0 ** (jnp.arange(quarter, dtype=F32) / quarter))
    ar = jnp.arange(rows, dtype=F32)[:, None] * omega
    ac = jnp.arange(GRID_W, dtype=F32)[:, None] * omega
    er = jnp.concatenate([jnp.sin(ar), jnp.cos(ar)], -1)
    ec = jnp.concatenate([jnp.sin(ac), jnp.cos(ac)], -1)
    half = D_MODEL // 2
    pos = jnp.concatenate([jnp.broadcast_to(er[:, None], (rows, GRID_W, half)),
                           jnp.broadcast_to(ec[None], (rows, GRID_W, half))], -1)
    return pos.reshape(rows * GRID_W, D_MODEL)


def rglru_layer(h3, modtab_t, w_in, conv_w, conv_b, ga_w, ga_b, gx_w, gx_b, lam, w_out, ln_g, ln_b, rw_hi, rw_lo,
                latent_only=False):
    def block_diag(w):
        eye = jnp.eye(RG_BLOCKS, dtype=w.dtype)
        return jnp.einsum("nkj,nm->nkmj", w, eye).reshape(D_RNN, D_RNN)

    def banded(wa, wx):
        out = []
        for jj in range(RG_LANES):
            r0 = _rg_window_start(jj) * 128
            cs = slice(jj * 128, (jj + 1) * 128)
            out.append(jnp.concatenate([wa[r0:r0 + RG_WIN * 128, cs], wx[r0:r0 + RG_WIN * 128, cs]], axis=1))
        return jnp.stack(out)

    w_gates = jnp.stack([banded(block_diag(ga_w[z]), block_diag(gx_w[z]))
                         for z in range(2)]).astype(BF16)
    gate, rec_tm = rg_in(h3, modtab_t, w_in.astype(BF16))
    h_tm = rg_scan(rec_tm, conv_w, conv_b[None], w_gates, ga_b[:, None], gx_b[:, None], lam[:, None])
    hn, fpk, lg = rg_out(gate, h_tm, w_out.astype(BF16), h3, modtab_t, ln_g, ln_b, rw_hi, rw_lo, latent_only)
    rows = BATCH * hn.shape[1]
    return hn.reshape(rows, D_MODEL), fpk.reshape(rows, D_MODEL // 2), lg.reshape(rows, ROUTER_PAD)


def mlstm_layer(hj, modtab, w_up, conv_w, conv_b, w_q, w_k, w_v, w_o, w_if, b_if, norm_g, skip, w_down,
                ln_g, ln_b, rw_hi, rw_lo):
    rows = hj.shape[0]
    xm = mod_proj(hj, modtab, w_up.astype(BF16))
    xc, q, k = ml_qk(xm, conv_w, conv_b[None], jnp.concatenate([w_q, w_k], axis=1).astype(BF16))
    v = mm_act(xm, w_v.astype(BF16))
    o = mm_act(xm, w_o.astype(BF16), "sigmoid")
    n_gate = 4 * M_HEADS
    w_g = jnp.concatenate([w_if[0], w_if[1], jnp.zeros((D_M, 128 - n_gate), F32)], axis=1).astype(BF16)
    b_g = jnp.concatenate([b_if[0], b_if[1], jnp.zeros((128 - n_gate,), F32)])[None]
    g = ml_gates(xm, w_g, b_g)[:, :n_gate].reshape(BATCH, L_JOINT, 2, 2, M_HEADS)
    g4 = jnp.transpose(g, (0, 1, 4, 2, 3)).reshape(BATCH, L_JOINT, M_HEADS, 4)
    g_col = jnp.transpose(g4, (0, 2, 1, 3))
    g_row = jnp.transpose(g4, (0, 2, 3, 1))
    r3 = lambda a: a.reshape(BATCH, L_JOINT, a.shape[-1])
    hn = mlstm_cell(r3(q), r3(k), r3(v), g_col, g_row, norm_g[None]).reshape(rows, D_M)
    return post_mixer_ml(o, hn, xc, skip[None], w_down.astype(BF16), hj, modtab, ln_g, ln_b, rw_hi, rw_lo)


def _dft_mats(n_time):
    n = 3 * n_time // 2
    half = n // 2
    kk = np.arange(half, dtype=np.int64)[:, None]
    tt = np.arange(n_time, dtype=np.int64)[None, :]
    ang = 2.0 * np.pi * ((kk * tt) % n).astype(np.float64) / n
    top = np.cos(ang)
    bot = -np.sin(ang)
    bot[0] = np.cos(np.pi * tt[0])
    fwd = np.concatenate([top, bot], axis=0)
    mm_ = (np.arange(n_time, dtype=np.int64) + n_time // 2)[:, None]
    ang2 = 2.0 * np.pi * ((mm_ * kk.T) % n).astype(np.float64) / n
    wk = np.full((1, half), 2.0)
    wk[0, 0] = 1.0
    itop = wk * np.cos(ang2) / n
    ibot = -2.0 * np.sin(ang2) / n
    ibot[:, 0] = np.cos(np.pi * mm_[:, 0]) / n
    inv = np.concatenate([itop, ibot], axis=1)
    return jnp.asarray(fwd, dtype=BF16), jnp.asarray(inv, dtype=BF16)


def hyena_filters(n_time, w1, b1, fq1, w2, b2, fq2, w3):
    hp = lax.Precision.HIGHEST
    t01 = jnp.linspace(0.0, 1.0, n_time, dtype=F32)
    bands = jnp.linspace(1e-4, H_BANDS - 1, H_BANDS, dtype=F32)
    ang = (2.0 * math.pi / n_time) * jnp.arange(n_time, dtype=F32)[:, None] * bands[None, :]
    z = jnp.concatenate([t01[:, None], jnp.cos(ang), -jnp.sin(ang)], -1)
    hdn = jnp.sin(fq1 * (jnp.dot(z, w1, precision=hp) + b1))
    hdn = jnp.sin(fq2 * (jnp.dot(hdn, w2, precision=hp) + b2))
    filt = jnp.dot(hdn, w3, precision=hp).reshape(n_time, 2, D_MODEL)
    dist = jnp.abs(jnp.arange(n_time) - n_time // 2).astype(F32) * (2.0 / n_time)
    d_max = math.log(H_DECAY_TARGET) / H_FAST
    d_min = math.log(H_DECAY_TARGET) / H_SLOW
    deltas = jnp.abs(jnp.linspace(d_min, d_max, D_MODEL, dtype=F32))
    window = jnp.exp(-dist[:, None] * deltas[None, :])
    return filt * window[:, None, :]


def hyena_layer(hj, modtab, w_in, b_in, conv_w, conv_b, fparams, skip, w_out, ln_g, ln_b, rw_hi, rw_lo):
    rows = hj.shape[0]
    u3 = mod_proj(hj, modtab, w_in.astype(BF16), b_in[None]).reshape(BATCH, L_JOINT, 3 * D_MODEL)
    parts = []
    for seg_block, n_time in ((0, SEQ), (SEQ // CTX_LEN, CTX_LEN)):
        filt = hyena_filters(n_time, *fparams)
        fm, _ = _dft_mats(n_time)
        hf = bmm_left(fm, filt.reshape(1, n_time, 2 * D_MODEL))[0]
        parts.append(hyena_segment(u3, seg_block, n_time, conv_w, conv_b[None], hf, skip))
    z = jnp.concatenate(parts, axis=1).reshape(rows, D_MODEL)
    return post_mixer(z, w_out.astype(BF16), hj, modtab, ln_g, ln_b, rw_hi, rw_lo)


def kernel(x, c, ctx, c_ctx, router_w, router_b, ada_w, ada_b, ln_g, ln_b, moe_w_gate, moe_w_up, moe_w_down, rg_w_in, rg_conv_w, rg_conv_b, rg_gate_a_w, rg_gate_a_b, rg_gate_x_w, rg_gate_x_b, rg_lambda, rg_w_out, ml_w_up, ml_conv_w, ml_conv_b, ml_w_q, ml_w_k, ml_w_v, ml_w_o, ml_w_if, ml_b_if, ml_norm_g, ml_skip, ml_w_down, hy_w_in, hy_b_in, hy_conv_w, hy_conv_b, hy_f_w1, hy_f_b1, hy_f_freq1, hy_f_w2, hy_f_b2, hy_f_freq2, hy_f_w3, hy_skip, hy_w_out):
    bsz = x.shape[0]
    rows = bsz * L_JOINT
    hx = x + sincos_2d(SEQ)[None]
    hj = jnp.concatenate([hx, ctx], axis=1).reshape(rows, D_MODEL)

    cond = jnp.concatenate([jax.nn.silu(c), jax.nn.silu(c_ctx)[None],
                            jnp.zeros((16 - bsz - 1, D_MODEL), F32)], axis=0)
    mod_all = cond_proj(cond, ada_w, ada_b)
    rw_pad = jnp.concatenate([router_w, jnp.zeros((D_MODEL, ROUTER_PAD - N_EXPERTS), F32)], axis=1)
    rw_hi = rw_pad.astype(BF16)
    rw_lo = (rw_pad - rw_hi.astype(F32)).astype(BF16)

    for i in range(DEPTH):
        kind, j = i % N_MIXERS, i // N_MIXERS
        mod = mod_all[i].reshape(16, 6, D_MODEL)
        mod_x = mod[:bsz]
        mod_c = jnp.broadcast_to(mod[bsz][None], (bsz, 6, D_MODEL))
        modtab = jnp.stack([mod_x, mod_c], axis=1).reshape(2 * bsz, 6, D_MODEL)
        lng, lnb = ln_g[i, 0][None], ln_b[i, 0][None]
        latent_only = i == DEPTH - 1 and kind == 0
        if kind == 0:
            modtab_t = jnp.stack([jnp.transpose(mod_x, (1, 0, 2)), jnp.transpose(mod_c, (1, 0, 2))])
            hj, f, logits = rglru_layer(hj.reshape(bsz, L_JOINT, D_MODEL), modtab_t, rg_w_in[j], rg_conv_w[j],
                                        rg_conv_b[j], rg_gate_a_w[j], rg_gate_a_b[j], rg_gate_x_w[j],
                                        rg_gate_x_b[j], rg_lambda[j], rg_w_out[j], lng, lnb, rw_hi, rw_lo,
                                        latent_only)
        elif kind == 1:
            hj, f, logits = mlstm_layer(hj, modtab, ml_w_up[j], ml_conv_w[j], ml_conv_b[j], ml_w_q[j], ml_w_k[j],
                                        ml_w_v[j], ml_w_o[j], ml_w_if[j], ml_b_if[j], ml_norm_g[j], ml_skip[j],
                                        ml_w_down[j], lng, lnb, rw_hi, rw_lo)
        else:
            fparams = (hy_f_w1[j], hy_f_b1[j], hy_f_freq1[j], hy_f_w2[j], hy_f_b2[j], hy_f_freq2[j], hy_f_w3[j])
            hj, f, logits = hyena_layer(hj, modtab, hy_w_in[j], hy_b_in[j], hy_conv_w[j], hy_conv_b[j], fparams,
                                        hy_skip[j], hy_w_out[j], lng, lnb, rw_hi, rw_lo)
        ya, yb, wts = moe(i, f, logits, router_b, moe_w_gate, moe_w_up, moe_w_down)
        hj = moe_combine(hj, ya, yb, wts, modtab, ln_g[i, 1][None], ln_b[i, 1][None], latent_only)
    return hj.reshape(bsz, -1, D_MODEL)[:, :SEQ]
```
